```python
import math
import jax, jax.numpy as jnp
from jax import lax
import numpy as np

D_MODEL = 1024
BATCH = 8
SEQ = 4096
DEPTH = 4

N_MIXERS = 3
D_PLE = 256
CONV_K = 4
CHUNK = 64
LN_EPS = 1e-5
DN_HEADS = 8
DN_DK = 128
DN_DV = 128
DN_QK = DN_HEADS * DN_DK
DN_V = DN_HEADS * DN_DV
DN_WIDTHS = (DN_QK, DN_QK, DN_V, DN_V, DN_HEADS, DN_HEADS)
DN_COLS = sum(DN_WIDTHS)
RW_HEAD = 64
RW_HEADS = D_MODEL // RW_HEAD
RW_W = RW_HEADS * RW_HEAD
RW_DECAY_LORA = 64
RW_A_LORA = 64
RW_GN_EPS = 64e-5
RW_WIDTHS = (RW_W, RW_DECAY_LORA, RW_W, RW_W, RW_A_LORA, RW_W)
RW_COLS = sum(RW_WIDTHS)
ML_HEADS = 8
ML_DQK = 64
ML_DV = 128
ML_QK = ML_HEADS * ML_DQK
ML_V = ML_HEADS * ML_DV
ML_WIDTHS = (ML_QK, ML_QK, ML_V, ML_V, ML_V, ML_HEADS, ML_HEADS)
ML_COLS = sum(ML_WIDTHS)
N_DN = (DEPTH + 2) // 3
N_RW = (DEPTH + 1) // 3
N_ML = DEPTH // 3
DEEPNORM_ALPHA = (2.0 * DEPTH) ** 0.25
DEEPNORM_BETA = (8.0 * DEPTH) ** -0.25

kernel_name = "hybrid_deltanet_rwkv7_mlstm_deepnorm"


def split_cols(h, widths):
    return jnp.split(h, [int(c) for c in np.cumsum(widths)[:-1]], axis=-1)


def layer_norm(x, g, b):
    xf = x.astype(jnp.float32)
    mu = jnp.mean(xf, -1, keepdims=True)
    var = jnp.mean(jnp.square(xf - mu), -1, keepdims=True)
    return ((xf - mu) * lax.rsqrt(var + LN_EPS) * g + b).astype(x.dtype)


def rms_norm(x, g, eps=1e-6):
    xf = x.astype(jnp.float32)
    return xf * lax.rsqrt(jnp.mean(xf * xf, -1, keepdims=True) + eps) * g


def l2_normalize(x, eps=1e-6):
    xf = x.astype(jnp.float32)
    return xf * lax.rsqrt(jnp.sum(xf * xf, -1, keepdims=True) + eps)


def group_norm(h, w, b, eps):
    mu = jnp.mean(h, -1, keepdims=True)
    var = jnp.mean(jnp.square(h - mu), -1, keepdims=True)
    out = ((h - mu) * lax.rsqrt(var + eps)).reshape(h.shape[0], h.shape[1], -1) * w
    return out if b is None else out + b


def causal_conv(x, w):
    K, S = w.shape[0], x.shape[1]
    xp = jnp.pad(x, ((0, 0), (K - 1, 0), (0, 0)))
    return sum(w[j] * xp[:, j:j + S] for j in range(K))


def token_shift(x):
    return jnp.pad(x, ((0, 0), (1, 0), (0, 0)))[:, :-1]


def gated_delta_rule_chunked(q, k, v, beta, g):
    B_, H, S, dk = q.shape
    dv = v.shape[-1]
    n, c = S // CHUNK, CHUNK
    q = q * dk ** -0.5
    to_chunks = lambda t: t.reshape(B_, H, n, c, *t.shape[3:])
    q, k, v, beta, g = map(to_chunks, (q, k, v, beta, g))
    G = jnp.cumsum(g, axis=-1)
    causal = jnp.tril(jnp.ones((c, c), dtype=bool))
    strict = jnp.tril(jnp.ones((c, c), dtype=bool), -1)
    diff = G[..., :, None] - G[..., None, :]
    decay_mat = jnp.where(causal, jnp.exp(jnp.where(causal, diff, 0.0)), 0.0)
    k_beta = k * beta[..., None]
    A = jnp.where(strict, jnp.einsum("bhnid,bhnjd->bhnij", k_beta, k) * decay_mat, 0.0)
    eye = jnp.eye(c, dtype=A.dtype)
    T = lax.linalg.triangular_solve(A + eye, jnp.broadcast_to(eye, A.shape),
                                    left_side=True, lower=True, unit_diagonal=True)
    u = jnp.einsum("bhnij,bhnjd->bhnid", T, v * beta[..., None])
    w = jnp.einsum("bhnij,bhnjd->bhnid", T, k_beta * jnp.exp(G)[..., None])
    qk = jnp.einsum("bhnid,bhnjd->bhnij", q, k) * decay_mat
    q_dec = q * jnp.exp(G)[..., None]
    k_dec = k * jnp.exp(G[..., -1:] - G)[..., None]
    g_last = jnp.exp(G[..., -1])

    def step(state, xs):
        qk_c, qd_c, kd_c, u_c, w_c, gl_c = xs
        v_new = u_c - jnp.einsum("bhid,bhde->bhie", w_c, state)
        o = jnp.einsum("bhid,bhde->bhie", qd_c, state) + jnp.einsum("bhij,bhje->bhie", qk_c, v_new)
        state = state * gl_c[..., None, None] + jnp.einsum("bhid,bhie->bhde", kd_c, v_new)
        return state, o

    xs = tuple(jnp.moveaxis(t, 2, 0) for t in (qk, q_dec, k_dec, u, w, g_last))
    state0 = jnp.zeros((B_, H, dk, dv), jnp.float32)
    _, o = lax.scan(step, state0, xs)
    return jnp.moveaxis(o, 0, 2).reshape(B_, H, S, dv)


def gated_deltanet_mixer(x, w_in, conv_w, a_log, dt_bias, norm_w, w_out):
    B_, S, _ = x.shape
    h = x @ w_in
    qkv, z, b_pre, a_pre = split_cols(h, (2 * DN_QK + DN_V, DN_V, DN_HEADS, DN_HEADS))
    qkv = jax.nn.silu(causal_conv(qkv, conv_w))
    q, k, v = split_cols(qkv, (DN_QK, DN_QK, DN_V))
    heads = lambda t, d: t.reshape(B_, S, DN_HEADS, d).transpose(0, 2, 1, 3)
    q = l2_normalize(heads(q, DN_DK))
    k = l2_normalize(heads(k, DN_DK))
    v = heads(v, DN_DV).astype(jnp.float32)
    beta = jax.nn.sigmoid(b_pre.astype(jnp.float32)).transpose(0, 2, 1)
    g = -(jnp.exp(a_log.astype(jnp.float32))
          * jax.nn.softplus(a_pre.astype(jnp.float32) + dt_bias)).transpose(0, 2, 1)
    o = gated_delta_rule_chunked(q, k, v, beta, g)
    o = rms_norm(o, norm_w).transpose(0, 2, 1, 3).reshape(B_, S, DN_V)
    y = (o * jax.nn.silu(z.astype(jnp.float32))).astype(x.dtype)
    return y @ w_out


def rwkv7_mixer(x, w_in, mu, w0, w_lora_up, a0, a_lora_up, k_k, k_a, r_k, gn_w, gn_b, w_out):
    B_, S, D = x.shape
    mu_cols = jnp.repeat(mu, np.array(RW_WIDTHS), axis=0, total_repeat_length=RW_COLS).T
    h = x @ w_in + (token_shift(x) - x) @ (mu_cols * w_in)
    r, w_lo, k, v, a_lo, z = split_cols(h, RW_WIDTHS)
    w_log = -jax.nn.softplus(-(w0 + jnp.tanh(w_lo) @ w_lora_up)) - 0.5
    a = jax.nn.sigmoid(a0 + a_lo @ a_lora_up)
    heads = lambda t: t.astype(jnp.float32).reshape(B_, S, RW_HEADS, RW_HEAD)
    r, k, v, a, w_log = map(heads, (r, k, v, a, w_log))
    kk = l2_normalize(k * k_k.reshape(RW_HEADS, RW_HEAD))
    k = k * (1.0 + (a - 1.0) * k_a.reshape(RW_HEADS, RW_HEAD))
    decay = jnp.exp(-jnp.exp(w_log))

    def step(state, xs):
        r_t, d_t, k_t, v_t, kk_t, a_t = xs
        sa = jnp.einsum("bhvk,bhk->bhv", state, -kk_t)
        state = (state * d_t[:, :, None, :] + sa[..., :, None] * (kk_t * a_t)[..., None, :]
                 + v_t[..., :, None] * k_t[..., None, :])
        return state, jnp.einsum("bhvk,bhk->bhv", state, r_t)

    xs = tuple(jnp.moveaxis(t, 1, 0) for t in (r, decay, k, v, kk, a))
    state0 = jnp.zeros((B_, RW_HEADS, RW_HEAD, RW_HEAD), jnp.float32)
    _, y = lax.scan(step, state0, xs)
    y = group_norm(jnp.moveaxis(y, 0, 1), gn_w, gn_b, RW_GN_EPS)
    bonus = jnp.sum(r * k * r_k, -1, keepdims=True) * v
    y = (y + bonus.reshape(B_, S, RW_W)) * jax.nn.silu(z.astype(jnp.float32))
    return y.astype(x.dtype) @ w_out


def mlstm_chunked(q, k, v, i_log, f_log):
    B_, H, S, dk = q.shape
    dv = v.shape[-1]
    n, c = S // CHUNK, CHUNK
    k = k * dk ** -0.5
    to_chunks = lambda t: t.reshape(B_, H, n, c, *t.shape[3:])
    q, k, v, i_log, f_log = map(to_chunks, (q, k, v, i_log, f_log))
    b = jnp.cumsum(f_log, axis=-1)
    causal = jnp.tril(jnp.ones((c, c), dtype=bool))
    d_log = jnp.where(causal, b[..., :, None] - b[..., None, :] + i_log[..., None, :], -jnp.inf)
    m_intra = jnp.max(d_log, axis=-1)
    qk = jnp.einsum("bhnid,bhnjd->bhnij", q, k)
    key_log = b[..., -1:] - b + i_log

    def step(carry, xs):
        C, nrm, m = carry
        qk_c, d_c, mi_c, b_c, kl_c, q_c, k_c, v_c = xs
        m_state = m[..., None] + b_c
        m_t = jnp.maximum(m_state, mi_c)
        inter = jnp.exp(m_state - m_t)
        w_qk = jnp.exp(d_c - m_t[..., None]) * qk_c
        num = (inter[..., None] * jnp.einsum("bhid,bhde->bhie", q_c, C)
               + jnp.einsum("bhij,bhje->bhie", w_qk, v_c))
        den = inter * jnp.einsum("bhid,bhd->bhi", q_c, nrm) + jnp.sum(w_qk, -1)
        h = num / jnp.maximum(jnp.abs(den), jnp.exp(-m_t))[..., None]
        m_new = m_t[..., -1]
        carry_scale = jnp.exp(m + b_c[..., -1] - m_new)
        k_w = k_c * jnp.exp(kl_c - m_new[..., None])[..., None]
        C = C * carry_scale[..., None, None] + jnp.einsum("bhid,bhie->bhde", k_w, v_c)
        nrm = nrm * carry_scale[..., None] + jnp.sum(k_w, axis=-2)
        return (C, nrm, m_new), h

    xs = tuple(jnp.moveaxis(t, 2, 0) for t in (qk, d_log, m_intra, b, key_log, q, k, v))
    carry0 = (jnp.zeros((B_, H, dk, dv), jnp.float32), jnp.zeros((B_, H, dk), jnp.float32),
              jnp.full((B_, H), -jnp.inf, jnp.float32))
    _, h = lax.scan(step, carry0, xs)
    return jnp.moveaxis(h, 0, 2).reshape(B_, H, S, dv)


def mlstm_mixer(x, w_in, conv_w, i_bias, f_bias, gn_w, w_out):
    B_, S, _ = x.shape
    h = x @ w_in
    qk, v, o, z, i_pre, f_pre = split_cols(h, (2 * ML_QK, ML_V, ML_V, ML_V, ML_HEADS, ML_HEADS))
    qk = jax.nn.silu(causal_conv(qk, conv_w))
    q, k = split_cols(qk, (ML_QK, ML_QK))
    heads = lambda t, d: t.astype(jnp.float32).reshape(B_, S, ML_HEADS, d).transpose(0, 2, 1, 3)
    q, k, v = heads(q, ML_DQK), heads(k, ML_DQK), heads(v, ML_DV)
    i_log = (i_pre.astype(jnp.float32) + i_bias).transpose(0, 2, 1)
    f_log = jax.nn.log_sigmoid(f_pre.astype(jnp.float32) + f_bias).transpose(0, 2, 1)
    h_tilde = mlstm_chunked(q, k, v, i_log, f_log).transpose(0, 2, 1, 3)
    og = jax.nn.sigmoid(o.astype(jnp.float32)).reshape(B_, S, ML_HEADS, ML_DV)
    hh = group_norm(og * h_tilde, gn_w, None, 1e-6)
    y = hh * jax.nn.silu(z.astype(jnp.float32))
    return y.astype(x.dtype) @ w_out


def setup_inputs(seed: int = 0) -> dict:
    key = jax.random.key(seed)
    ks = iter(jax.random.split(key, 40))
    nk = lambda: next(ks)
    nrm = lambda shape, s: jax.random.normal(nk(), shape, jnp.float32) * s
    uni = lambda shape, lo, hi: jax.random.uniform(nk(), shape, jnp.float32, lo, hi)
    D = D_MODEL

    def col_scale(widths, scales):
        return jnp.asarray(np.concatenate([np.full(wd, sc, np.float32) for wd, sc in zip(widths, scales)]))

    x = nrm((BATCH, SEQ, D), 1.0)
    p = nrm((DEPTH, BATCH, SEQ, D_PLE), 1.0)
    ln_g = 1.0 + nrm((DEPTH, D), 0.02)
    ln_b = nrm((DEPTH, D), 0.02)
    ple_w_proj = nrm((DEPTH, D_PLE, D), D_PLE ** -0.5)
    ple_norm_w = 1.0 + nrm((DEPTH, D), 0.02)
    ple_w_gate = nrm((DEPTH, D, D), D ** -0.5)

    s_in = D ** -0.5
    dn_w_in = nrm((N_DN, D, DN_COLS), 1.0) * col_scale(DN_WIDTHS, (s_in,) * 5 + (0.1 * s_in,))
    dn_conv_w = nrm((N_DN, CONV_K, 2 * DN_QK + DN_V), CONV_K ** -0.5)
    dn_a_log = jnp.log(uni((N_DN, DN_HEADS), 1.0, 16.0))
    dt = jnp.exp(uni((N_DN, DN_HEADS), math.log(1e-3), math.log(1e-1)))
    dn_dt_bias = dt + jnp.log(-jnp.expm1(-dt))
    dn_norm_w = 1.0 + nrm((N_DN, DN_DV), 0.02)
    dn_w_out = nrm((N_DN, DN_V, D), DN_V ** -0.5 * DEEPNORM_BETA)

    rw_w_in = nrm((N_RW, D, RW_COLS), s_in)
    rw_mu = uni((N_RW, len(RW_WIDTHS), D), 0.0, 1.0)
    rw_w0 = uni((N_RW, RW_W), -5.0, -1.0)
    rw_w_lora_up = nrm((N_RW, RW_DECAY_LORA, RW_W), 0.1 * RW_DECAY_LORA ** -0.5)
    rw_a0 = nrm((N_RW, RW_W), 0.1)
    rw_a_lora_up = nrm((N_RW, RW_A_LORA, RW_W), 0.1 * RW_A_LORA ** -0.5)
    rw_k_k = 0.85 + nrm((N_RW, RW_W), 0.05)
    rw_k_a = 1.0 + nrm((N_RW, RW_W), 0.05)
    rw_r_k = nrm((N_RW, RW_HEADS, RW_HEAD), 0.2)
    rw_gn_w = 1.0 + nrm((N_RW, RW_W), 0.02)
    rw_gn_b = nrm((N_RW, RW_W), 0.02)
    rw_w_out = nrm((N_RW, RW_W, D), RW_W ** -0.5 * DEEPNORM_BETA)

    ml_w_in = nrm((N_ML, D, ML_COLS), 1.0) * col_scale(ML_WIDTHS, (s_in,) * 5 + (0.1 * s_in,) * 2)
    ml_conv_w = nrm((N_ML, CONV_K, 2 * ML_QK), CONV_K ** -0.5)
    ml_i_bias = nrm((N_ML, ML_HEADS), 0.1)
    ml_f_bias = uni((N_ML, ML_HEADS), 3.0, 6.0)
    ml_gn_w = 1.0 + nrm((N_ML, ML_V), 0.02)
    ml_w_out = nrm((N_ML, ML_V, D), ML_V ** -0.5 * DEEPNORM_BETA)

    return {"x": x, "p": p, "ln_g": ln_g, "ln_b": ln_b, "ple_w_proj": ple_w_proj,
            "ple_norm_w": ple_norm_w, "ple_w_gate": ple_w_gate,
            "dn_w_in": dn_w_in, "dn_conv_w": dn_conv_w, "dn_a_log": dn_a_log, "dn_dt_bias": dn_dt_bias,
            "dn_norm_w": dn_norm_w, "dn_w_out": dn_w_out,
            "rw_w_in": rw_w_in, "rw_mu": rw_mu, "rw_w0": rw_w0, "rw_w_lora_up": rw_w_lora_up,
            "rw_a0": rw_a0, "rw_a_lora_up": rw_a_lora_up, "rw_k_k": rw_k_k, "rw_k_a": rw_k_a,
            "rw_r_k": rw_r_k, "rw_gn_w": rw_gn_w, "rw_gn_b": rw_gn_b, "rw_w_out": rw_w_out,
            "ml_w_in": ml_w_in, "ml_conv_w": ml_conv_w, "ml_i_bias": ml_i_bias, "ml_f_bias": ml_f_bias,
            "ml_gn_w": ml_gn_w, "ml_w_out": ml_w_out}


def reference(x, p, ln_g, ln_b, ple_w_proj, ple_norm_w, ple_w_gate,
              dn_w_in, dn_conv_w, dn_a_log, dn_dt_bias, dn_norm_w, dn_w_out,
              rw_w_in, rw_mu, rw_w0, rw_w_lora_up, rw_a0, rw_a_lora_up, rw_k_k, rw_k_a,
              rw_r_k, rw_gn_w, rw_gn_b, rw_w_out,
              ml_w_in, ml_conv_w, ml_i_bias, ml_f_bias, ml_gn_w, ml_w_out):
    for i in range(DEPTH):
        kind, j = i % N_MIXERS, i // N_MIXERS
        if kind == 0:
            y = gated_deltanet_mixer(x, dn_w_in[j], dn_conv_w[j], dn_a_log[j], dn_dt_bias[j],
                                     dn_norm_w[j], dn_w_out[j])
        elif kind == 1:
            y = rwkv7_mixer(x, rw_w_in[j], rw_mu[j], rw_w0[j], rw_w_lora_up[j], rw_a0[j], rw_a_lora_up[j],
                            rw_k_k[j], rw_k_a[j], rw_r_k[j], rw_gn_w[j], rw_gn_b[j], rw_w_out[j])
        else:
            y = mlstm_mixer(x, ml_w_in[j], ml_conv_w[j], ml_i_bias[j], ml_f_bias[j], ml_gn_w[j], ml_w_out[j])
        x = layer_norm(DEEPNORM_ALPHA * x + y, ln_g[i], ln_b[i])
        gate = jax.nn.sigmoid((x @ ple_w_gate[i]).astype(jnp.float32))
        x = (x.astype(jnp.float32) + gate * rms_norm(p[i] @ ple_w_proj[i], ple_norm_w[i])).astype(x.dtype)
    return x
```

```python
import functools

import jax
import jax.numpy as jnp
import numpy as np
from jax import lax
from jax.experimental import pallas as pl
from jax.experimental.pallas import tpu as pltpu

F32 = jnp.float32
MXU_DTYPE = jnp.bfloat16

LANES = 128
SUBLANES = 8
VMEM_LIMIT = 48 * 1024 * 1024

DEPTH = 4
D_MODEL = 1024
D_PLE = 256
CONV_K = 4
CHUNK = 64
LN_EPS = 1e-5
DN_HEADS, DN_DK, DN_DV = 8, 128, 128
DN_QK = DN_HEADS * DN_DK
DN_V = DN_HEADS * DN_DV
RW_HEAD = 64
RW_HEADS = D_MODEL // RW_HEAD
RW_W = RW_HEADS * RW_HEAD
RW_LORA = 64
RW_GN_EPS = 64e-5
ML_HEADS, ML_DQK, ML_DV = 8, 64, 128
ML_QK = ML_HEADS * ML_DQK
ML_V = ML_HEADS * ML_DV
DEEPNORM_ALPHA = (2.0 * DEPTH) ** 0.25
HIST = SUBLANES


def _dot(a, b):
    return lax.dot_general(a.astype(MXU_DTYPE), b.astype(MXU_DTYPE), (((1,), (0,)), ((), ())),
                           preferred_element_type=F32)


def _dot_nt(a, b):
    return lax.dot_general(a.astype(MXU_DTYPE), b.astype(MXU_DTYPE), (((1,), (1,)), ((), ())),
                           preferred_element_type=F32)


def _dot_tn(a, b):
    return lax.dot_general(a.astype(MXU_DTYPE), b.astype(MXU_DTYPE), (((0,), (0,)), ((), ())),
                           preferred_element_type=F32)


def _dot_f32(a, b):
    return lax.dot_general(a, b, (((1,), (0,)), ((), ())), precision=lax.Precision.HIGHEST,
                           preferred_element_type=F32)


def _silu(x):
    return x * jax.nn.sigmoid(x)


def _chunk_iota(n):
    row = lax.broadcasted_iota(jnp.int32, (n, n), 0)
    col = lax.broadcasted_iota(jnp.int32, (n, n), 1)
    return row, col


def _doubling_masks(row, col, n):
    masks = []
    s, shift = 1, 0
    while s < n:
        same = (row >> (shift + 1)) == (col >> (shift + 1))
        masks.append(same & ((row & s) != 0) & ((col & s) == 0))
        s, shift = 2 * s, shift + 1
    return masks


def _unit_lower_inverse(a, eye, masks):
    t = eye - jnp.where(masks[0], a, 0.0)
    for m in masks[1:]:
        am = jnp.where(m, a, 0.0)
        t = t - _dot(_dot(t, am), t)
    return t


def _causal_conv_silu(hist_ref, x, w_ref, rows):
    hist_ref[HIST:HIST + rows, :] = x
    acc = w_ref[CONV_K - 1:CONV_K, :] * x
    for j in range(CONV_K - 1):
        acc = acc + w_ref[j:j + 1, :] * hist_ref[pl.ds(HIST - (CONV_K - 1) + j, rows), :]
    hist_ref[0:HIST, :] = hist_ref[rows:rows + HIST, :]
    return _silu(acc)


def _proj_kernel(x_ref, w_ref, o_ref):
    o_ref[...] = _dot(x_ref[...], w_ref[...])


def _project(x2d, w, tm):
    m, k = x2d.shape
    n = w.shape[1]
    return pl.pallas_call(
        _proj_kernel,
        grid=(m // tm,),
        in_specs=[pl.BlockSpec((tm, k), lambda i: (i, 0)),
                  pl.BlockSpec((k, n), lambda i: (0, 0))],
        out_specs=pl.BlockSpec((tm, n), lambda i: (i, 0)),
        out_shape=jax.ShapeDtypeStruct((m, n), F32),
        compiler_params=pltpu.CompilerParams(dimension_semantics=("parallel",),
                                             vmem_limit_bytes=VMEM_LIMIT),
        name="in_proj",
    )(x2d, w)


def _rw_proj_kernel(x_ref, prev_ref, w_ref, mu_ref, o_ref, buf, *, tm):
    x = x_ref[0]
    buf[HIST:HIST + tm, :] = x
    buf[0:HIST, :] = jnp.where(pl.program_id(1) == 0, 0.0, prev_ref[0])
    dx = buf[pl.ds(HIST - 1, tm), :] - x
    for g in range(4):
        lhs = x + mu_ref[g:g + 1, :] * dx
        o_ref[0, :, g * RW_W:(g + 1) * RW_W] = _dot(lhs, w_ref[:, g * RW_W:(g + 1) * RW_W])
    w_lo = w_ref[:, 4 * RW_W:4 * RW_W + LANES]
    lo_w = _dot(x + mu_ref[4:5, :] * dx, w_lo)
    lo_a = _dot(x + mu_ref[5:6, :] * dx, w_lo)
    lane = lax.broadcasted_iota(jnp.int32, lo_w.shape, 1)
    o_ref[0, :, 4 * RW_W:4 * RW_W + LANES] = jnp.where(lane < RW_LORA, lo_w, lo_a)


def _rw_project(x, w, mu, tm):
    b, s, d = x.shape
    n = w.shape[1]
    per = tm // HIST
    return pl.pallas_call(
        functools.partial(_rw_proj_kernel, tm=tm),
        grid=(b, s // tm),
        in_specs=[pl.BlockSpec((1, tm, d), lambda i, j: (i, j, 0)),
                  pl.BlockSpec((1, HIST, d), lambda i, j: (i, jnp.maximum(j * per - 1, 0), 0)),
                  pl.BlockSpec((d, n), lambda i, j: (0, 0)),
                  pl.BlockSpec(mu.shape, lambda i, j: (0, 0))],
        out_specs=pl.BlockSpec((1, tm, n), lambda i, j: (i, j, 0)),
        out_shape=jax.ShapeDtypeStruct((b, s, n), F32),
        scratch_shapes=[pltpu.VMEM((tm + HIST, d), F32)],
        compiler_params=pltpu.CompilerParams(dimension_semantics=("parallel", "parallel"),
                                             vmem_limit_bytes=VMEM_LIMIT),
        name="rw_in_proj",
    )(x, x, w, mu)


def _dn_kernel(h_ref, convw_ref, alog_ref, dtb_ref, normw_ref, y_ref, hist, state):
    c = CHUNK

    @pl.when(pl.program_id(1) == 0)
    def _():
        hist[0:HIST, :] = jnp.zeros((HIST, hist.shape[1]), F32)
        state[...] = jnp.zeros(state.shape, F32)

    qkv = _causal_conv_silu(hist, h_ref[0, :, 0:2 * DN_QK + DN_V], convw_ref, c)

    gates = h_ref[0, :, 2 * DN_QK + 2 * DN_V:2 * DN_QK + 2 * DN_V + LANES]
    beta_t = jax.nn.sigmoid(gates)
    g_t = -(jnp.exp(alog_ref[...]) * jax.nn.softplus(gates + dtb_ref[...]))
    row, col = _chunk_iota(c)
    causal = row >= col
    strict = row > col
    eye = (row == col).astype(F32)
    masks = _doubling_masks(row, col, c)
    gcum = _dot_f32(causal.astype(F32), g_t)
    gcum_t = gcum.T

    for h in range(DN_HEADS):
        q = qkv[:, h * DN_DK:(h + 1) * DN_DK]
        k = qkv[:, DN_QK + h * DN_DK:DN_QK + (h + 1) * DN_DK]
        v = qkv[:, 2 * DN_QK + h * DN_DV:2 * DN_QK + (h + 1) * DN_DV]
        z = h_ref[0, :, 2 * DN_QK + DN_V + h * DN_DV:2 * DN_QK + DN_V + (h + 1) * DN_DV]
        q = q * (lax.rsqrt(jnp.sum(q * q, -1, keepdims=True) + 1e-6) * DN_DK ** -0.5)
        k = k * lax.rsqrt(jnp.sum(k * k, -1, keepdims=True) + 1e-6)
        beta = beta_t[:, h:h + 1]
        g_col = gcum[:, DN_HEADS + h:DN_HEADS + h + 1]
        g_row = gcum_t[DN_HEADS + h:DN_HEADS + h + 1, :]
        g_last = g_col[c - 1:c, :]
        decay = jnp.exp(jnp.where(causal, g_col - g_row, -jnp.inf))
        exp_g = jnp.exp(g_col)
        k_beta = k * beta
        a = jnp.where(strict, _dot_nt(k_beta, k) * decay, 0.0)
        t = _unit_lower_inverse(a, eye, masks)
        uw = _dot(t, jnp.concatenate([v * beta, k_beta * exp_g], axis=1))
        u, w = uw[:, :DN_DV], uw[:, DN_DV:]
        qk = _dot_nt(q, k) * decay
        s0 = state[h]
        v_new = u - _dot(w, s0)
        o = _dot(q * exp_g, s0) + _dot(qk, v_new)
        state[h] = s0 * jnp.exp(g_last) + _dot_tn(k * jnp.exp(g_last - g_col), v_new)
        o = o * lax.rsqrt(jnp.mean(o * o, -1, keepdims=True) + 1e-6) * normw_ref[...]
        y_ref[0, :, h * DN_DV:(h + 1) * DN_DV] = o * _silu(z)


def _dn_mixer(h, conv_w, alog_row, dtb_row, norm_w):
    b, s, n = h.shape
    ncols = 2 * DN_QK + DN_V
    return pl.pallas_call(
        _dn_kernel,
        grid=(b, s // CHUNK),
        in_specs=[pl.BlockSpec((1, CHUNK, n), lambda i, j: (i, j, 0)),
                  pl.BlockSpec(conv_w.shape, lambda i, j: (0, 0)),
                  pl.BlockSpec(alog_row.shape, lambda i, j: (0, 0)),
                  pl.BlockSpec(dtb_row.shape, lambda i, j: (0, 0)),
                  pl.BlockSpec(norm_w.shape, lambda i, j: (0, 0))],
        out_specs=pl.BlockSpec((1, CHUNK, DN_V), lambda i, j: (i, j, 0)),
        out_shape=jax.ShapeDtypeStruct((b, s, DN_V), F32),
        scratch_shapes=[pltpu.VMEM((CHUNK + HIST, ncols), F32),
                        pltpu.VMEM((DN_HEADS, DN_DK, DN_DV), F32)],
        compiler_params=pltpu.CompilerParams(dimension_semantics=("parallel", "arbitrary"),
                                             vmem_limit_bytes=VMEM_LIMIT),
        name="dn_mixer",
    )(h, conv_w, alog_row, dtb_row, norm_w)


def _rw_kernel(h_ref, wup_ref, aup_ref, w0_ref, a0_ref, kk_ref, ka_ref, rk_ref, gnw_ref, gnb_ref,
               y_ref, state):
    c = CHUNK
    d = RW_HEAD

    @pl.when(pl.program_id(1) == 0)
    def _():
        state[...] = jnp.zeros(state.shape, F32)

    r_all = h_ref[0, :, 0:RW_W]
    k_all = h_ref[0, :, RW_W:2 * RW_W]
    v_all = h_ref[0, :, 2 * RW_W:3 * RW_W]
    z_all = h_ref[0, :, 3 * RW_W:4 * RW_W]
    lo = h_ref[0, :, 4 * RW_W:4 * RW_W + LANES]
    w_log = -jax.nn.softplus(-(w0_ref[...] + _dot(jnp.tanh(lo[:, :RW_LORA]), wup_ref[...]))) - 0.5
    a_all = jax.nn.sigmoid(a0_ref[...] + _dot(lo[:, RW_LORA:], aup_ref[...]))
    logd = -jnp.exp(w_log)
    row, col = _chunk_iota(c)
    causal = row >= col
    strict = row > col
    eye = (row == col).astype(F32)
    masks = _doubling_masks(row, col, c)
    lcum_all = _dot_f32(causal.astype(F32), logd)
    kk_all = k_all * kk_ref[...]
    k2_all = k_all * (1.0 + (a_all - 1.0) * ka_ref[...])
    rkr_all = r_all * k2_all * rk_ref[...]

    for h in range(RW_HEADS):
        sl = slice(h * d, (h + 1) * d)
        r, k2, v, av = r_all[:, sl], k2_all[:, sl], v_all[:, sl], a_all[:, sl]
        kk = kk_all[:, sl]
        kk = kk * lax.rsqrt(jnp.sum(kk * kk, -1, keepdims=True) + 1e-6)
        lcum = lcum_all[:, sl]
        l_last = lcum[c - 1:c, :]
        p_incl = jnp.exp(lcum)
        p_excl = jnp.exp(lcum - logd[:, sl])
        p_inv = jnp.exp(-lcum)
        p_tail = jnp.exp(l_last - lcum)
        alpha = -(kk * av)
        lhs = jnp.concatenate([kk * p_excl, r * p_incl], axis=0)
        rhs = jnp.concatenate([alpha * p_inv, k2 * p_inv], axis=0)
        m = _dot_nt(lhs, rhs)
        a_ab = jnp.where(strict, m[:c, :c], 0.0)
        a_ak = jnp.where(strict, m[:c, c:], 0.0)
        a_r = jnp.concatenate([jnp.where(causal, m[c:, :c], 0.0), jnp.where(causal, m[c:, c:], 0.0)], axis=1)
        t = _unit_lower_inverse(-a_ab, eye, masks)
        s0 = state[h]
        lhs_s = _dot_nt(lhs, s0)
        u = _dot(t, lhs_s[:c] + _dot(a_ak, v))
        uv = jnp.concatenate([u, v], axis=0)
        y = lhs_s[c:] + _dot(a_r, uv)
        tail = jnp.concatenate([alpha * p_tail, k2 * p_tail], axis=0)
        state[h] = s0 * jnp.exp(l_last) + _dot_tn(uv, tail)
        mu = jnp.mean(y, -1, keepdims=True)
        var = jnp.mean(jnp.square(y - mu), -1, keepdims=True)
        yn = (y - mu) * lax.rsqrt(var + RW_GN_EPS) * gnw_ref[:, sl] + gnb_ref[:, sl]
        bonus = jnp.sum(rkr_all[:, sl], -1, keepdims=True) * v
        y_ref[0, :, sl] = (yn + bonus) * _silu(z_all[:, sl])


def _rw_mixer(h, wup, aup, w0, a0, k_k, k_a, r_k, gn_w, gn_b):
    b, s, n = h.shape
    full = lambda arr: pl.BlockSpec(arr.shape, lambda i, j: (0,) * arr.ndim)
    params = (wup, aup, w0, a0, k_k, k_a, r_k, gn_w, gn_b)
    return pl.pallas_call(
        _rw_kernel,
        grid=(b, s // CHUNK),
        in_specs=[pl.BlockSpec((1, CHUNK, n), lambda i, j: (i, j, 0))] + [full(a) for a in params],
        out_specs=pl.BlockSpec((1, CHUNK, RW_W), lambda i, j: (i, j, 0)),
        out_shape=jax.ShapeDtypeStruct((b, s, RW_W), F32),
        scratch_shapes=[pltpu.VMEM((RW_HEADS, RW_HEAD, RW_HEAD), F32)],
        compiler_params=pltpu.CompilerParams(dimension_semantics=("parallel", "arbitrary"),
                                             vmem_limit_bytes=VMEM_LIMIT),
        name="rw_mixer",
    )(h, *params)


def _ml_kernel(h_ref, convw_ref, ib_ref, fb_ref, gnw_ref, y_ref, hist, cstate, nstate, mstate):
    c = CHUNK

    @pl.when(pl.program_id(1) == 0)
    def _():
        hist[0:HIST, :] = jnp.zeros((HIST, hist.shape[1]), F32)
        cstate[...] = jnp.zeros(cstate.shape, F32)
        nstate[...] = jnp.zeros(nstate.shape, F32)
        mstate[...] = jnp.full(mstate.shape, -jnp.inf, F32)

    qk_all = _causal_conv_silu(hist, h_ref[0, :, 0:2 * ML_QK], convw_ref, c)
    gates = h_ref[0, :, 2 * ML_QK + 3 * ML_V:2 * ML_QK + 3 * ML_V + LANES]
    i_t = gates + ib_ref[...]
    f_t = jax.nn.log_sigmoid(gates + fb_ref[...])
    row, col = _chunk_iota(c)
    causal = row >= col
    bcum = _dot_f32(causal.astype(F32), f_t)
    bcum_t = bcum.T
    i_tt = i_t.T

    for h in range(ML_HEADS):
        q = qk_all[:, h * ML_DQK:(h + 1) * ML_DQK]
        k = qk_all[:, ML_QK + h * ML_DQK:ML_QK + (h + 1) * ML_DQK] * ML_DQK ** -0.5
        base = 2 * ML_QK + h * ML_DV
        v = h_ref[0, :, base:base + ML_DV]
        og = h_ref[0, :, base + ML_V:base + ML_V + ML_DV]
        z = h_ref[0, :, base + 2 * ML_V:base + 2 * ML_V + ML_DV]
        b_col = bcum[:, ML_HEADS + h:ML_HEADS + h + 1]
        b_row = bcum_t[ML_HEADS + h:ML_HEADS + h + 1, :]
        i_col = i_t[:, h:h + 1]
        i_row = i_tt[h:h + 1, :]
        b_last = b_col[c - 1:c, :]
        d_log = jnp.where(causal, b_col - b_row + i_row, -jnp.inf)
        m_intra = jnp.max(d_log, -1, keepdims=True)
        m_prev = mstate[h][0:1, 0:1]
        m_state = m_prev + b_col
        m_t = jnp.maximum(m_state, m_intra)
        inter = jnp.exp(m_state - m_t)
        w_qk = jnp.exp(d_log - m_t) * _dot_nt(q, k)
        c0 = cstate[h]
        n0 = nstate[h]
        num = inter * _dot(q, c0) + _dot(w_qk, v)
        den = inter * jnp.sum(q * n0, -1, keepdims=True) + jnp.sum(w_qk, -1, keepdims=True)
        h_tilde = num / jnp.maximum(jnp.abs(den), jnp.exp(-m_t))
        m_new = m_t[c - 1:c, :]
        carry = jnp.exp(m_prev + b_last - m_new)
        k_w = k * jnp.exp(b_last - b_col + i_col - m_new)
        cstate[h] = c0 * carry + _dot_tn(k_w, v)
        nstate[h] = n0 * carry + jnp.sum(k_w, 0, keepdims=True)
        mstate[h] = jnp.broadcast_to(m_new, mstate.shape[1:])
        xg = jax.nn.sigmoid(og) * h_tilde
        mu = jnp.mean(xg, -1, keepdims=True)
        var = jnp.mean(jnp.square(xg - mu), -1, keepdims=True)
        xn = (xg - mu) * lax.rsqrt(var + 1e-6) * gnw_ref[:, h * ML_DV:(h + 1) * ML_DV]
        y_ref[0, :, h * ML_DV:(h + 1) * ML_DV] = xn * _silu(z)


def _ml_mixer(h, conv_w, ib_row, fb_row, gn_w):
    b, s, n = h.shape
    return pl.pallas_call(
        _ml_kernel,
        grid=(b, s // CHUNK),
        in_specs=[pl.BlockSpec((1, CHUNK, n), lambda i, j: (i, j, 0)),
                  pl.BlockSpec(conv_w.shape, lambda i, j: (0, 0)),
                  pl.BlockSpec(ib_row.shape, lambda i, j: (0, 0)),
                  pl.BlockSpec(fb_row.shape, lambda i, j: (0, 0)),
                  pl.BlockSpec(gn_w.shape, lambda i, j: (0, 0))],
        out_specs=pl.BlockSpec((1, CHUNK, ML_V), lambda i, j: (i, j, 0)),
        out_shape=jax.ShapeDtypeStruct((b, s, ML_V), F32),
        scratch_shapes=[pltpu.VMEM((CHUNK + HIST, 2 * ML_QK), F32),
                        pltpu.VMEM((ML_HEADS, ML_DQK, ML_DV), F32),
                        pltpu.VMEM((ML_HEADS, 1, ML_DQK), F32),
                        pltpu.VMEM((ML_HEADS, SUBLANES, LANES), F32)],
        compiler_params=pltpu.CompilerParams(dimension_semantics=("parallel", "arbitrary"),
                                             vmem_limit_bytes=VMEM_LIMIT),
        name="ml_mixer",
    )(h, conv_w, ib_row, fb_row, gn_w)


def _post_kernel(x_ref, y_ref, p_ref, wout_ref, wg_ref, wp_ref, lng_ref, lnb_ref, pnw_ref, o_ref):
    r = DEEPNORM_ALPHA * x_ref[...] + _dot(y_ref[...], wout_ref[...])
    mu = jnp.mean(r, -1, keepdims=True)
    var = jnp.mean(jnp.square(r - mu), -1, keepdims=True)
    xn = (r - mu) * lax.rsqrt(var + LN_EPS) * lng_ref[...] + lnb_ref[...]
    gate = jax.nn.sigmoid(_dot(xn, wg_ref[...]))
    pp = _dot(p_ref[...], wp_ref[...])
    pn = pp * lax.rsqrt(jnp.mean(pp * pp, -1, keepdims=True) + 1e-6) * pnw_ref[...]
    o_ref[...] = xn + gate * pn


def _post(x2d, y2d, p2d, w_out, w_gate, w_proj, ln_g, ln_b, pn_w, tm):
    m, d = x2d.shape
    tile = lambda arr: pl.BlockSpec((tm, arr.shape[1]), lambda i: (i, 0))
    full = lambda arr: pl.BlockSpec(arr.shape, lambda i: (0, 0))
    params = (w_out, w_gate, w_proj, ln_g, ln_b, pn_w)
    return pl.pallas_call(
        _post_kernel,
        grid=(m // tm,),
        in_specs=[tile(x2d), tile(y2d), tile(p2d)] + [full(a) for a in params],
        out_specs=pl.BlockSpec((tm, d), lambda i: (i, 0)),
        out_shape=jax.ShapeDtypeStruct((m, d), F32),
        compiler_params=pltpu.CompilerParams(dimension_semantics=("parallel",),
                                             vmem_limit_bytes=VMEM_LIMIT),
        name="post_block",
    )(x2d, y2d, p2d, *params)


def _pad_cols(w, n):
    return jnp.pad(w, ((0, 0), (0, n - w.shape[1])))


def _lane_row(vec, offset):
    return jnp.zeros((1, LANES), F32).at[0, offset:offset + vec.shape[0]].set(vec.astype(F32))


def _row_tile(m):
    for t in (512, 256, 128, 64):
        if m % t == 0:
            return t
    raise ValueError(f"token count {m} must be a multiple of {CHUNK}")


def kernel(x, p, ln_g, ln_b, ple_w_proj, ple_norm_w, ple_w_gate, dn_w_in, dn_conv_w, dn_a_log, dn_dt_bias, dn_norm_w, dn_w_out, rw_w_in, rw_mu, rw_w0, rw_w_lora_up, rw_a0, rw_a_lora_up, rw_k_k, rw_k_a, rw_r_k, rw_gn_w, rw_gn_b, rw_w_out, ml_w_in, ml_conv_w, ml_i_bias, ml_f_bias, ml_gn_w, ml_w_out):
    b, s, d = x.shape
    assert d == D_MODEL and s % CHUNK == 0
    m = b * s
    tm_proj = _row_tile(m) // 2 if _row_tile(m) > CHUNK else CHUNK
    tm_post = _row_tile(m)
    tm_rw = _row_tile(s) // 2 if _row_tile(s) > CHUNK else CHUNK
    bf = lambda w: w.astype(MXU_DTYPE)
    row = lambda v: v.reshape(1, -1).astype(F32)
    x2d = x.reshape(m, d)
    for i in range(DEPTH):
        kind, j = i % 3, i // 3
        if kind == 0:
            n_pad = 2 * DN_QK + 2 * DN_V + LANES
            h = _project(x2d, bf(_pad_cols(dn_w_in[j], n_pad)), tm_proj).reshape(b, s, n_pad)
            y = _dn_mixer(h, dn_conv_w[j], _lane_row(dn_a_log[j], DN_HEADS), _lane_row(dn_dt_bias[j], DN_HEADS),
                          row(dn_norm_w[j]))
            w_out = dn_w_out[j]
        elif kind == 1:
            r_w, wl_w, k_w, v_w, al_w, z_w = jnp.split(
                rw_w_in[j], np.cumsum([RW_W, RW_LORA, RW_W, RW_W, RW_LORA])[:].tolist(), axis=1)
            w_cat = jnp.concatenate([r_w, k_w, v_w, z_w, wl_w, al_w], axis=1)
            mu = rw_mu[j]
            mu_cat = jnp.stack([mu[0], mu[2], mu[3], mu[5], mu[1], mu[4], mu[0], mu[0]], axis=0)
            h = _rw_project(x2d.reshape(b, s, d), bf(w_cat), mu_cat, tm_rw)
            y = _rw_mixer(h, bf(rw_w_lora_up[j]), bf(rw_a_lora_up[j]), row(rw_w0[j]), row(rw_a0[j]),
                          row(rw_k_k[j]), row(rw_k_a[j]), row(rw_r_k[j]), row(rw_gn_w[j]), row(rw_gn_b[j]))
            w_out = rw_w_out[j]
        else:
            n_pad = 2 * ML_QK + 3 * ML_V + LANES
            h = _project(x2d, bf(_pad_cols(ml_w_in[j], n_pad)), tm_proj).reshape(b, s, n_pad)
            y = _ml_mixer(h, ml_conv_w[j], _lane_row(ml_i_bias[j], 0), _lane_row(ml_f_bias[j], ML_HEADS),
                          row(ml_gn_w[j]))
            w_out = ml_w_out[j]
        x2d = _post(x2d, y.reshape(m, -1), p[i].reshape(m, D_PLE), bf(w_out), bf(ple_w_gate[i]),
                    bf(ple_w_proj[i]), row(ln_g[i]), row(ln_b[i]), row(ple_norm_w[i]), tm_post)
    return x2d.reshape(b, s, d)
```

```python
import functools

import jax
import jax.numpy as jnp
import numpy as np
from jax import lax
from jax.experimental import pallas as pl
from jax.experimental.pallas import tpu as pltpu

F32 = jnp.float32
MXU_DTYPE = jnp.bfloat16

LANES = 128
SUBLANES = 8
VMEM_LIMIT = 48 * 1024 * 1024

DEPTH = 4
D_MODEL = 1024
D_PLE = 256
CONV_K = 4
CHUNK = 64
LN_EPS = 1e-5
DN_HEADS, DN_DK, DN_DV = 8, 128, 128
DN_QK = DN_HEADS * DN_DK
DN_V = DN_HEADS * DN_DV
RW_HEAD = 64
RW_HEADS = D_MODEL // RW_HEAD
RW_W = RW_HEADS * RW_HEAD
RW_LORA = 64
RW_GN_EPS = 64e-5
ML_HEADS, ML_DQK, ML_DV = 8, 64, 128
ML_QK = ML_HEADS * ML_DQK
ML_V = ML_HEADS * ML_DV
DEEPNORM_ALPHA = (2.0 * DEPTH) ** 0.25
HIST = SUBLANES


def _dot(a, b):
    return lax.dot_general(a.astype(MXU_DTYPE), b.astype(MXU_DTYPE), (((1,), (0,)), ((), ())),
                           preferred_element_type=F32)


def _dot_nt(a, b):
    return lax.dot_general(a.astype(MXU_DTYPE), b.astype(MXU_DTYPE), (((1,), (1,)), ((), ())),
                           preferred_element_type=F32)


def _dot_tn(a, b):
    return lax.dot_general(a.astype(MXU_DTYPE), b.astype(MXU_DTYPE), (((0,), (0,)), ((), ())),
                           preferred_element_type=F32)


def _dot_f32(a, b):
    return lax.dot_general(a, b, (((1,), (0,)), ((), ())), precision=lax.Precision.HIGHEST,
                           preferred_element_type=F32)


def _silu(x):
    return x * jax.nn.sigmoid(x)


def _chunk_iota(n):
    row = lax.broadcasted_iota(jnp.int32, (n, n), 0)
    col = lax.broadcasted_iota(jnp.int32, (n, n), 1)
    return row, col


def _doubling_masks(row, col, n):
    masks = []
    s, shift = 1, 0
    while s < n:
        same = (row >> (shift + 1)) == (col >> (shift + 1))
        masks.append(same & ((row & s) != 0) & ((col & s) == 0))
        s, shift = 2 * s, shift + 1
    return masks


def _unit_lower_inverse(a, eye, masks):
    t = eye - jnp.where(masks[0], a, 0.0)
    for m in masks[1:]:
        am = jnp.where(m, a, 0.0)
        t = t - _dot(_dot(t, am), t)
    return t


def _unit_lower_inverse_many(a_list, eye, masks):
    t = [eye - jnp.where(masks[0], a, 0.0) for a in a_list]
    for m in masks[1:]:
        ta = [_dot(ti, jnp.where(m, a, 0.0)) for ti, a in zip(t, a_list)]
        t = [ti - _dot(tai, ti) for ti, tai in zip(t, ta)]
    return t


def _causal_conv_silu(hist_ref, x, w_ref, rows):
    hist_ref[HIST:HIST + rows, :] = x
    acc = w_ref[CONV_K - 1:CONV_K, :] * x
    for j in range(CONV_K - 1):
        acc = acc + w_ref[j:j + 1, :] * hist_ref[pl.ds(HIST - (CONV_K - 1) + j, rows), :]
    hist_ref[0:HIST, :] = hist_ref[rows:rows + HIST, :]
    return _silu(acc)


def _proj_kernel(x_ref, w_ref, o_ref):
    o_ref[...] = _dot(x_ref[...], w_ref[...])


def _project(x2d, w, tm):
    m, k = x2d.shape
    n = w.shape[1]
    return pl.pallas_call(
        _proj_kernel,
        grid=(m // tm,),
        in_specs=[pl.BlockSpec((tm, k), lambda i: (i, 0)),
                  pl.BlockSpec((k, n), lambda i: (0, 0))],
        out_specs=pl.BlockSpec((tm, n), lambda i: (i, 0)),
        out_shape=jax.ShapeDtypeStruct((m, n), F32),
        compiler_params=pltpu.CompilerParams(dimension_semantics=("parallel",),
                                             vmem_limit_bytes=VMEM_LIMIT),
        name="in_proj",
    )(x2d, w)


def _rw_proj_kernel(x_ref, prev_ref, w_ref, mu_ref, o_ref, buf, *, tm):
    x = x_ref[0]
    buf[HIST:HIST + tm, :] = x
    buf[0:HIST, :] = jnp.where(pl.program_id(1) == 0, 0.0, prev_ref[0])
    dx = buf[pl.ds(HIST - 1, tm), :] - x
    for g in range(4):
        lhs = x + mu_ref[g:g + 1, :] * dx
        o_ref[0, :, g * RW_W:(g + 1) * RW_W] = _dot(lhs, w_ref[:, g * RW_W:(g + 1) * RW_W])
    w_lo = w_ref[:, 4 * RW_W:4 * RW_W + LANES]
    lo_w = _dot(x + mu_ref[4:5, :] * dx, w_lo)
    lo_a = _dot(x + mu_ref[5:6, :] * dx, w_lo)
    lane = lax.broadcasted_iota(jnp.int32, lo_w.shape, 1)
    o_ref[0, :, 4 * RW_W:4 * RW_W + LANES] = jnp.where(lane < RW_LORA, lo_w, lo_a)


def _rw_project(x, w, mu, tm):
    b, s, d = x.shape
    n = w.shape[1]
    per = tm // HIST
    return pl.pallas_call(
        functools.partial(_rw_proj_kernel, tm=tm),
        grid=(b, s // tm),
        in_specs=[pl.BlockSpec((1, tm, d), lambda i, j: (i, j, 0)),
                  pl.BlockSpec((1, HIST, d), lambda i, j: (i, jnp.maximum(j * per - 1, 0), 0)),
                  pl.BlockSpec((d, n), lambda i, j: (0, 0)),
                  pl.BlockSpec(mu.shape, lambda i, j: (0, 0))],
        out_specs=pl.BlockSpec((1, tm, n), lambda i, j: (i, j, 0)),
        out_shape=jax.ShapeDtypeStruct((b, s, n), F32),
        scratch_shapes=[pltpu.VMEM((tm + HIST, d), F32)],
        compiler_params=pltpu.CompilerParams(dimension_semantics=("parallel", "parallel"),
                                             vmem_limit_bytes=VMEM_LIMIT),
        name="rw_in_proj",
    )(x, x, w, mu)


def _dn_kernel(h_ref, convw_ref, alog_ref, dtb_ref, normw_ref, y_ref, hist, state):
    c = CHUNK

    @pl.when(pl.program_id(1) == 0)
    def _():
        hist[0:HIST, :] = jnp.zeros((HIST, hist.shape[1]), F32)
        state[...] = jnp.zeros(state.shape, F32)

    qkv = _causal_conv_silu(hist, h_ref[0, :, 0:2 * DN_QK + DN_V], convw_ref, c)

    gates = h_ref[0, :, 2 * DN_QK + 2 * DN_V:2 * DN_QK + 2 * DN_V + LANES]
    beta_t = jax.nn.sigmoid(gates)
    g_t = -(jnp.exp(alog_ref[...]) * jax.nn.softplus(gates + dtb_ref[...]))
    row, col = _chunk_iota(c)
    causal = row >= col
    strict = row > col
    eye = (row == col).astype(F32)
    masks = _doubling_masks(row, col, c)
    gcum = _dot_f32(causal.astype(F32), g_t)
    gcum_t = gcum.T

    heads = range(DN_HEADS)
    q = [qkv[:, h * DN_DK:(h + 1) * DN_DK] for h in heads]
    k = [qkv[:, DN_QK + h * DN_DK:DN_QK + (h + 1) * DN_DK] for h in heads]
    v = [qkv[:, 2 * DN_QK + h * DN_DV:2 * DN_QK + (h + 1) * DN_DV] for h in heads]
    q = [x * (lax.rsqrt(jnp.sum(x * x, -1, keepdims=True) + 1e-6) * DN_DK ** -0.5) for x in q]
    k = [x * lax.rsqrt(jnp.sum(x * x, -1, keepdims=True) + 1e-6) for x in k]
    beta = [beta_t[:, h:h + 1] for h in heads]
    g_col = [gcum[:, DN_HEADS + h:DN_HEADS + h + 1] for h in heads]
    g_row = [gcum_t[DN_HEADS + h:DN_HEADS + h + 1, :] for h in heads]
    g_last = [g[c - 1:c, :] for g in g_col]
    decay = [jnp.exp(jnp.where(causal, g_col[h] - g_row[h], -jnp.inf)) for h in heads]
    exp_g = [jnp.exp(g) for g in g_col]
    k_beta = [k[h] * beta[h] for h in heads]
    a = [jnp.where(strict, _dot_nt(k_beta[h], k[h]) * decay[h], 0.0) for h in heads]
    qk = [_dot_nt(q[h], k[h]) * decay[h] for h in heads]
    t = _unit_lower_inverse_many(a, eye, masks)
    uw = [_dot(t[h], jnp.concatenate([v[h] * beta[h], k_beta[h] * exp_g[h]], axis=1)) for h in heads]
    s0 = [state[h] for h in heads]
    v_new = [uw[h][:, :DN_DV] - _dot(uw[h][:, DN_DV:], s0[h]) for h in heads]
    o = [_dot(q[h] * exp_g[h], s0[h]) + _dot(qk[h], v_new[h]) for h in heads]
    s1 = [s0[h] * jnp.exp(g_last[h]) + _dot_tn(k[h] * jnp.exp(g_last[h] - g_col[h]), v_new[h]) for h in heads]
    for h in heads:
        state[h] = s1[h]
    for h in heads:
        z = h_ref[0, :, 2 * DN_QK + DN_V + h * DN_DV:2 * DN_QK + DN_V + (h + 1) * DN_DV]
        on = o[h] * lax.rsqrt(jnp.mean(o[h] * o[h], -1, keepdims=True) + 1e-6) * normw_ref[...]
        y_ref[0, :, h * DN_DV:(h + 1) * DN_DV] = on * _silu(z)


def _dn_mixer(h, conv_w, alog_row, dtb_row, norm_w):
    b, s, n = h.shape
    ncols = 2 * DN_QK + DN_V
    return pl.pallas_call(
        _dn_kernel,
        grid=(b, s // CHUNK),
        in_specs=[pl.BlockSpec((1, CHUNK, n), lambda i, j: (i, j, 0)),
                  pl.BlockSpec(conv_w.shape, lambda i, j: (0, 0)),
                  pl.BlockSpec(alog_row.shape, lambda i, j: (0, 0)),
                  pl.BlockSpec(dtb_row.shape, lambda i, j: (0, 0)),
                  pl.BlockSpec(norm_w.shape, lambda i, j: (0, 0))],
        out_specs=pl.BlockSpec((1, CHUNK, DN_V), lambda i, j: (i, j, 0)),
        out_shape=jax.ShapeDtypeStruct((b, s, DN_V), F32),
        scratch_shapes=[pltpu.VMEM((CHUNK + HIST, ncols), F32),
                        pltpu.VMEM((DN_HEADS, DN_DK, DN_DV), F32)],
        compiler_params=pltpu.CompilerParams(dimension_semantics=("parallel", "arbitrary"),
                                             vmem_limit_bytes=VMEM_LIMIT),
        name="dn_mixer",
    )(h, conv_w, alog_row, dtb_row, norm_w)


def _rw_kernel(h_ref, wup_ref, aup_ref, w0_ref, a0_ref, kk_ref, ka_ref, rk_ref, gnw_ref, gnb_ref,
               y_ref, state):
    c = CHUNK
    d = RW_HEAD

    @pl.when(pl.program_id(1) == 0)
    def _():
        state[...] = jnp.zeros(state.shape, F32)

    r_all = h_ref[0, :, 0:RW_W]
    k_all = h_ref[0, :, RW_W:2 * RW_W]
    v_all = h_ref[0, :, 2 * RW_W:3 * RW_W]
    z_all = h_ref[0, :, 3 * RW_W:4 * RW_W]
    lo = h_ref[0, :, 4 * RW_W:4 * RW_W + LANES]
    w_log = -jax.nn.softplus(-(w0_ref[...] + _dot(jnp.tanh(lo[:, :RW_LORA]), wup_ref[...]))) - 0.5
    a_all = jax.nn.sigmoid(a0_ref[...] + _dot(lo[:, RW_LORA:], aup_ref[...]))
    logd = -jnp.exp(w_log)
    row, col = _chunk_iota(c)
    causal = row >= col
    strict = row > col
    eye = (row == col).astype(F32)
    masks = _doubling_masks(row, col, c)
    lcum_all = _dot_f32(causal.astype(F32), logd)
    kk_all = k_all * kk_ref[...]
    k2_all = k_all * (1.0 + (a_all - 1.0) * ka_ref[...])
    rkr_all = r_all * k2_all * rk_ref[...]

    heads = range(RW_HEADS)
    sl = [slice(h * d, (h + 1) * d) for h in heads]
    v = [v_all[:, s] for s in sl]
    kk = [kk_all[:, s] for s in sl]
    kk = [x * lax.rsqrt(jnp.sum(x * x, -1, keepdims=True) + 1e-6) for x in kk]
    lcum = [lcum_all[:, s] for s in sl]
    l_last = [x[c - 1:c, :] for x in lcum]
    alpha = [-(kk[h] * a_all[:, sl[h]]) for h in heads]
    p_inv = [jnp.exp(-x) for x in lcum]
    lhs = [jnp.concatenate([kk[h] * jnp.exp(lcum[h] - logd[:, sl[h]]), r_all[:, sl[h]] * jnp.exp(lcum[h])],
                           axis=0) for h in heads]
    rhs = [jnp.concatenate([alpha[h] * p_inv[h], k2_all[:, sl[h]] * p_inv[h]], axis=0)
           for h in heads]
    m = [_dot_nt(lhs[h], rhs[h]) for h in heads]
    a_ab = [jnp.where(strict, -x[:c, :c], 0.0) for x in m]
    a_ak = [jnp.where(strict, x[:c, c:], 0.0) for x in m]
    a_r = [jnp.concatenate([jnp.where(causal, x[c:, :c], 0.0), jnp.where(causal, x[c:, c:], 0.0)], axis=1)
           for x in m]
    t = _unit_lower_inverse_many(a_ab, eye, masks)
    akv = [_dot(a_ak[h], v[h]) for h in heads]
    s0 = [state[h] for h in heads]
    lhs_s = [_dot_nt(lhs[h], s0[h]) for h in heads]
    u = [_dot(t[h], lhs_s[h][:c] + akv[h]) for h in heads]
    uv = [jnp.concatenate([u[h], v[h]], axis=0) for h in heads]
    y = [lhs_s[h][c:] + _dot(a_r[h], uv[h]) for h in heads]
    p_tail = [jnp.exp(l_last[h] - lcum[h]) for h in heads]
    tail = [jnp.concatenate([alpha[h] * p_tail[h], k2_all[:, sl[h]] * p_tail[h]], axis=0)
            for h in heads]
    s1 = [s0[h] * jnp.exp(l_last[h]) + _dot_tn(uv[h], tail[h]) for h in heads]
    for h in heads:
        state[h] = s1[h]
    for h in heads:
        mu = jnp.mean(y[h], -1, keepdims=True)
        var = jnp.mean(jnp.square(y[h] - mu), -1, keepdims=True)
        yn = (y[h] - mu) * lax.rsqrt(var + RW_GN_EPS) * gnw_ref[:, sl[h]] + gnb_ref[:, sl[h]]
        bonus = jnp.sum(rkr_all[:, sl[h]], -1, keepdims=True) * v[h]
        y_ref[0, :, sl[h]] = (yn + bonus) * _silu(z_all[:, sl[h]])


def _rw_mixer(h, wup, aup, w0, a0, k_k, k_a, r_k, gn_w, gn_b):
    b, s, n = h.shape
    full = lambda arr: pl.BlockSpec(arr.shape, lambda i, j: (0,) * arr.ndim)
    params = (wup, aup, w0, a0, k_k, k_a, r_k, gn_w, gn_b)
    return pl.pallas_call(
        _rw_kernel,
        grid=(b, s // CHUNK),
        in_specs=[pl.BlockSpec((1, CHUNK, n), lambda i, j: (i, j, 0))] + [full(a) for a in params],
        out_specs=pl.BlockSpec((1, CHUNK, RW_W), lambda i, j: (i, j, 0)),
        out_shape=jax.ShapeDtypeStruct((b, s, RW_W), F32),
        scratch_shapes=[pltpu.VMEM((RW_HEADS, RW_HEAD, RW_HEAD), F32)],
        compiler_params=pltpu.CompilerParams(dimension_semantics=("parallel", "arbitrary"),
                                             vmem_limit_bytes=VMEM_LIMIT),
        name="rw_mixer",
    )(h, *params)


def _ml_kernel(h_ref, convw_ref, ib_ref, fb_ref, gnw_ref, y_ref, hist, cstate, nstate, mstate):
    c = CHUNK

    @pl.when(pl.program_id(1) == 0)
    def _():
        hist[0:HIST, :] = jnp.zeros((HIST, hist.shape[1]), F32)
        cstate[...] = jnp.zeros(cstate.shape, F32)
        nstate[...] = jnp.zeros(nstate.shape, F32)
        mstate[...] = jnp.full(mstate.shape, -jnp.inf, F32)

    qk_all = _causal_conv_silu(hist, h_ref[0, :, 0:2 * ML_QK], convw_ref, c)
    gates = h_ref[0, :, 2 * ML_QK + 3 * ML_V:2 * ML_QK + 3 * ML_V + LANES]
    i_t = gates + ib_ref[...]
    f_t = jax.nn.log_sigmoid(gates + fb_ref[...])
    row, col = _chunk_iota(c)
    causal = row >= col
    bcum = _dot_f32(causal.astype(F32), f_t)
    bcum_t = bcum.T
    i_tt = i_t.T

    heads = range(ML_HEADS)
    q = [qk_all[:, h * ML_DQK:(h + 1) * ML_DQK] for h in heads]
    k = [qk_all[:, ML_QK + h * ML_DQK:ML_QK + (h + 1) * ML_DQK] * ML_DQK ** -0.5 for h in heads]
    base = [2 * ML_QK + h * ML_DV for h in heads]
    v = [h_ref[0, :, base[h]:base[h] + ML_DV] for h in heads]
    b_col = [bcum[:, ML_HEADS + h:ML_HEADS + h + 1] for h in heads]
    b_row = [bcum_t[ML_HEADS + h:ML_HEADS + h + 1, :] for h in heads]
    i_col = [i_t[:, h:h + 1] for h in heads]
    i_row = [i_tt[h:h + 1, :] for h in heads]
    b_last = [x[c - 1:c, :] for x in b_col]
    d_log = [jnp.where(causal, b_col[h] - b_row[h] + i_row[h], -jnp.inf) for h in heads]
    m_intra = [jnp.max(x, -1, keepdims=True) for x in d_log]
    qk = [_dot_nt(q[h], k[h]) for h in heads]
    m_prev = [mstate[h][0:1, 0:1] for h in heads]
    c0 = [cstate[h] for h in heads]
    n0 = [nstate[h] for h in heads]
    m_state = [m_prev[h] + b_col[h] for h in heads]
    m_t = [jnp.maximum(m_state[h], m_intra[h]) for h in heads]
    inter = [jnp.exp(m_state[h] - m_t[h]) for h in heads]
    w_qk = [jnp.exp(d_log[h] - m_t[h]) * qk[h] for h in heads]
    num = [inter[h] * _dot(q[h], c0[h]) + _dot(w_qk[h], v[h]) for h in heads]
    den = [inter[h] * jnp.sum(q[h] * n0[h], -1, keepdims=True) + jnp.sum(w_qk[h], -1, keepdims=True)
           for h in heads]
    h_tilde = [num[h] / jnp.maximum(jnp.abs(den[h]), jnp.exp(-m_t[h])) for h in heads]
    m_new = [x[c - 1:c, :] for x in m_t]
    carry = [jnp.exp(m_prev[h] + b_last[h] - m_new[h]) for h in heads]
    k_w = [k[h] * jnp.exp(b_last[h] - b_col[h] + i_col[h] - m_new[h]) for h in heads]
    c1 = [c0[h] * carry[h] + _dot_tn(k_w[h], v[h]) for h in heads]
    n1 = [n0[h] * carry[h] + jnp.sum(k_w[h], 0, keepdims=True) for h in heads]
    for h in heads:
        cstate[h] = c1[h]
        nstate[h] = n1[h]
        mstate[h] = jnp.broadcast_to(m_new[h], mstate.shape[1:])
    for h in heads:
        og = h_ref[0, :, base[h] + ML_V:base[h] + ML_V + ML_DV]
        z = h_ref[0, :, base[h] + 2 * ML_V:base[h] + 2 * ML_V + ML_DV]
        xg = jax.nn.sigmoid(og) * h_tilde[h]
        mu = jnp.mean(xg, -1, keepdims=True)
        var = jnp.mean(jnp.square(xg - mu), -1, keepdims=True)
        xn = (xg - mu) * lax.rsqrt(var + 1e-6) * gnw_ref[:, h * ML_DV:(h + 1) * ML_DV]
        y_ref[0, :, h * ML_DV:(h + 1) * ML_DV] = xn * _silu(z)


def _ml_mixer(h, conv_w, ib_row, fb_row, gn_w):
    b, s, n = h.shape
    return pl.pallas_call(
        _ml_kernel,
        grid=(b, s // CHUNK),
        in_specs=[pl.BlockSpec((1, CHUNK, n), lambda i, j: (i, j, 0)),
                  pl.BlockSpec(conv_w.shape, lambda i, j: (0, 0)),
                  pl.BlockSpec(ib_row.shape, lambda i, j: (0, 0)),
                  pl.BlockSpec(fb_row.shape, lambda i, j: (0, 0)),
                  pl.BlockSpec(gn_w.shape, lambda i, j: (0, 0))],
        out_specs=pl.BlockSpec((1, CHUNK, ML_V), lambda i, j: (i, j, 0)),
        out_shape=jax.ShapeDtypeStruct((b, s, ML_V), F32),
        scratch_shapes=[pltpu.VMEM((CHUNK + HIST, 2 * ML_QK), F32),
                        pltpu.VMEM((ML_HEADS, ML_DQK, ML_DV), F32),
                        pltpu.VMEM((ML_HEADS, 1, ML_DQK), F32),
                        pltpu.VMEM((ML_HEADS, SUBLANES, LANES), F32)],
        compiler_params=pltpu.CompilerParams(dimension_semantics=("parallel", "arbitrary"),
                                             vmem_limit_bytes=VMEM_LIMIT),
        name="ml_mixer",
    )(h, conv_w, ib_row, fb_row, gn_w)


def _post_kernel(x_ref, y_ref, p_ref, wout_ref, wg_ref, wp_ref, lng_ref, lnb_ref, pnw_ref, o_ref):
    r = DEEPNORM_ALPHA * x_ref[...] + _dot(y_ref[...], wout_ref[...])
    mu = jnp.mean(r, -1, keepdims=True)
    var = jnp.mean(jnp.square(r - mu), -1, keepdims=True)
    xn = (r - mu) * lax.rsqrt(var + LN_EPS) * lng_ref[...] + lnb_ref[...]
    gate = jax.nn.sigmoid(_dot(xn, wg_ref[...]))
    pp = _dot(p_ref[...], wp_ref[...])
    pn = pp * lax.rsqrt(jnp.mean(pp * pp, -1, keepdims=True) + 1e-6) * pnw_ref[...]
    o_ref[...] = xn + gate * pn


def _post(x2d, y2d, p2d, w_out, w_gate, w_proj, ln_g, ln_b, pn_w, tm):
    m, d = x2d.shape
    tile = lambda arr: pl.BlockSpec((tm, arr.shape[1]), lambda i: (i, 0))
    full = lambda arr: pl.BlockSpec(arr.shape, lambda i: (0, 0))
    params = (w_out, w_gate, w_proj, ln_g, ln_b, pn_w)
    return pl.pallas_call(
        _post_kernel,
        grid=(m // tm,),
        in_specs=[tile(x2d), tile(y2d), tile(p2d)] + [full(a) for a in params],
        out_specs=pl.BlockSpec((tm, d), lambda i: (i, 0)),
        out_shape=jax.ShapeDtypeStruct((m, d), F32),
        compiler_params=pltpu.CompilerParams(dimension_semantics=("parallel",),
                                             vmem_limit_bytes=VMEM_LIMIT),
        name="post_block",
    )(x2d, y2d, p2d, *params)


def _pad_cols(w, n):
    return jnp.pad(w, ((0, 0), (0, n - w.shape[1])))


def _lane_row(vec, offset):
    return jnp.zeros((1, LANES), F32).at[0, offset:offset + vec.shape[0]].set(vec.astype(F32))


def _row_tile(m):
    for t in (512, 256, 128, 64):
        if m % t == 0:
            return t
    raise ValueError(f"token count {m} must be a multiple of {CHUNK}")


def kernel(x, p, ln_g, ln_b, ple_w_proj, ple_norm_w, ple_w_gate, dn_w_in, dn_conv_w, dn_a_log, dn_dt_bias, dn_norm_w, dn_w_out, rw_w_in, rw_mu, rw_w0, rw_w_lora_up, rw_a0, rw_a_lora_up, rw_k_k, rw_k_a, rw_r_k, rw_gn_w, rw_gn_b, rw_w_out, ml_w_in, ml_conv_w, ml_i_bias, ml_f_bias, ml_gn_w, ml_w_out):
    b, s, d = x.shape
    assert d == D_MODEL and s % CHUNK == 0
    m = b * s
    tm_proj = _row_tile(m) // 2 if _row_tile(m) > CHUNK else CHUNK
    tm_post = _row_tile(m)
    tm_rw = _row_tile(s) // 2 if _row_tile(s) > CHUNK else CHUNK
    bf = lambda w: w.astype(MXU_DTYPE)
    row = lambda v: v.reshape(1, -1).astype(F32)
    x2d = x.reshape(m, d)
    for i in range(DEPTH):
        kind, j = i % 3, i // 3
        if kind == 0:
            n_pad = 2 * DN_QK + 2 * DN_V + LANES
            h = _project(x2d, bf(_pad_cols(dn_w_in[j], n_pad)), tm_proj).reshape(b, s, n_pad)
            y = _dn_mixer(h, dn_conv_w[j], _lane_row(dn_a_log[j], DN_HEADS), _lane_row(dn_dt_bias[j], DN_HEADS),
                          row(dn_norm_w[j]))
            w_out = dn_w_out[j]
        elif kind == 1:
            r_w, wl_w, k_w, v_w, al_w, z_w = jnp.split(
                rw_w_in[j], np.cumsum([RW_W, RW_LORA, RW_W, RW_W, RW_LORA])[:].tolist(), axis=1)
            w_cat = jnp.concatenate([r_w, k_w, v_w, z_w, wl_w, al_w], axis=1)
            mu = rw_mu[j]
            mu_cat = jnp.stack([mu[0], mu[2], mu[3], mu[5], mu[1], mu[4], mu[0], mu[0]], axis=0)
            h = _rw_project(x2d.reshape(b, s, d), bf(w_cat), mu_cat, tm_rw)
            y = _rw_mixer(h, bf(rw_w_lora_up[j]), bf(rw_a_lora_up[j]), row(rw_w0[j]), row(rw_a0[j]),
                          row(rw_k_k[j]), row(rw_k_a[j]), row(rw_r_k[j]), row(rw_gn_w[j]), row(rw_gn_b[j]))
            w_out = rw_w_out[j]
        else:
            n_pad = 2 * ML_QK + 3 * ML_V + LANES
            h = _project(x2d, bf(_pad_cols(ml_w_in[j], n_pad)), tm_proj).reshape(b, s, n_pad)
            y = _ml_mixer(h, ml_conv_w[j], _lane_row(ml_i_bias[j], 0), _lane_row(ml_f_bias[j], ML_HEADS),
                          row(ml_gn_w[j]))
            w_out = ml_w_out[j]
        x2d = _post(x2d, y.reshape(m, -1), p[i].reshape(m, D_PLE), bf(w_out), bf(ple_w_gate[i]),
                    bf(ple_w_proj[i]), row(ln_g[i]), row(ln_b[i]), row(ple_norm_w[i]), tm_post)
    return x2d.reshape(b, s, d)
```

```python
import functools

import jax
import jax.numpy as jnp
import numpy as np
from jax import lax
from jax.experimental import pallas as pl
from jax.experimental.pallas import tpu as pltpu

F32 = jnp.float32
MXU_DTYPE = jnp.bfloat16

LANES = 128
SUBLANES = 8
VMEM_LIMIT = 48 * 1024 * 1024

DEPTH = 4
D_MODEL = 1024
D_PLE = 256
CONV_K = 4
CHUNK = 64
LN_EPS = 1e-5
DN_HEADS, DN_DK, DN_DV = 8, 128, 128
DN_QK = DN_HEADS * DN_DK
DN_V = DN_HEADS * DN_DV
RW_HEAD = 64
RW_HEADS = D_MODEL // RW_HEAD
RW_W = RW_HEADS * RW_HEAD
RW_LORA = 64
RW_GN_EPS = 64e-5
ML_HEADS, ML_DQK, ML_DV = 8, 64, 128
ML_QK = ML_HEADS * ML_DQK
ML_V = ML_HEADS * ML_DV
DEEPNORM_ALPHA = (2.0 * DEPTH) ** 0.25
HIST = SUBLANES
DN_CHUNKS = 4
RW_CHUNKS = 1
ML_CHUNKS = 1


def _dot(a, b):
    return lax.dot_general(a.astype(MXU_DTYPE), b.astype(MXU_DTYPE), (((1,), (0,)), ((), ())),
                           preferred_element_type=F32)


def _dot_nt(a, b):
    return lax.dot_general(a.astype(MXU_DTYPE), b.astype(MXU_DTYPE), (((1,), (1,)), ((), ())),
                           preferred_element_type=F32)


def _dot_tn(a, b):
    return lax.dot_general(a.astype(MXU_DTYPE), b.astype(MXU_DTYPE), (((0,), (0,)), ((), ())),
                           preferred_element_type=F32)


def _dot_f32(a, b):
    return lax.dot_general(a, b, (((1,), (0,)), ((), ())), precision=lax.Precision.HIGHEST,
                           preferred_element_type=F32)


def _silu(x):
    return x * jax.nn.sigmoid(x)


def _perm_time(p):
    return ((p & (SUBLANES - 1)) << 3) | (p >> 3)


def _chunk_iota(n, permuted=False):
    row = lax.broadcasted_iota(jnp.int32, (n, n), 0)
    col = lax.broadcasted_iota(jnp.int32, (n, n), 1)
    eye = (row == col).astype(F32)
    if permuted:
        row, col = _perm_time(row), _perm_time(col)
    return eye, row, col


def _doubling_masks(row, col, n):
    masks = []
    s, shift = 1, 0
    while s < n:
        same = (row >> (shift + 1)) == (col >> (shift + 1))
        masks.append(same & ((row & s) != 0) & ((col & s) == 0))
        s, shift = 2 * s, shift + 1
    return masks


def _unit_lower_inverse_many(a_list, eye, masks):
    t = [eye - jnp.where(masks[0], a, 0.0) for a in a_list]
    for m in masks[1:]:
        ta = [_dot(ti, jnp.where(m, a, 0.0)) for ti, a in zip(t, a_list)]
        t = [ti - _dot(tai, ti) for ti, tai in zip(t, ta)]
    return t


def _col_tiles(ref, chunk, first, count):
    return jnp.concatenate([ref[chunk, first + i] for i in range(count)], axis=1)


def _causal_conv_silu(prev_tail, x, w_ref):
    keep = CONV_K - 1
    tail = x[CHUNK - keep * SUBLANES:, :]
    sub = lax.broadcasted_iota(jnp.int32, tail.shape, 0) & (SUBLANES - 1)
    mixed = jnp.where(sub == SUBLANES - 1, prev_tail, tail)
    wrapped = jnp.concatenate([pltpu.roll(mixed[i * SUBLANES:(i + 1) * SUBLANES, :], 1, 0) for i in range(keep)],
                              axis=0)
    acc = w_ref[CONV_K - 1:CONV_K, :] * x
    for s in range(1, CONV_K):
        shifted = jnp.concatenate([wrapped[(keep - s) * SUBLANES:, :], x[:CHUNK - s * SUBLANES, :]], axis=0)
        acc = acc + w_ref[CONV_K - 1 - s:CONV_K - s, :] * shifted
    return _silu(acc), tail


def _store_natural_order(y_ref, chunk, tile, y):
    for b in range(SUBLANES):
        y_ref[chunk, tile, pl.ds(b, CHUNK // SUBLANES, stride=SUBLANES), :] = y[b * SUBLANES:(b + 1) * SUBLANES, :]


PROJ_COLS = 2 * LANES


def _proj_kernel(x_ref, w_ref, o_ref, *, tm):
    x = x_ref[...].astype(MXU_DTYPE)
    n = w_ref.shape[1]
    for c0 in range(0, n, PROJ_COLS):
        width = min(PROJ_COLS, n - c0)
        res = lax.dot_general(x, w_ref[:, c0:c0 + width], (((1,), (0,)), ((), ())), preferred_element_type=F32)
        for ch in range(tm // CHUNK):
            for a in range(CHUNK // SUBLANES):
                r0 = ch * CHUNK + a * SUBLANES
                for t in range(width // LANES):
                    o_ref[ch, c0 // LANES + t, pl.ds(a, SUBLANES, stride=SUBLANES), :] = (
                        res[r0:r0 + SUBLANES, t * LANES:(t + 1) * LANES])


def _project(x2d, w, tm):
    m, k = x2d.shape
    n = w.shape[1]
    return pl.pallas_call(
        functools.partial(_proj_kernel, tm=tm),
        grid=(m // tm,),
        in_specs=[pl.BlockSpec((tm, k), lambda i: (i, 0)),
                  pl.BlockSpec((k, n), lambda i: (0, 0))],
        out_specs=pl.BlockSpec((tm // CHUNK, n // LANES, CHUNK, LANES), lambda i: (i, 0, 0, 0)),
        out_shape=jax.ShapeDtypeStruct((m // CHUNK, n // LANES, CHUNK, LANES), F32),
        compiler_params=pltpu.CompilerParams(dimension_semantics=("parallel",),
                                             vmem_limit_bytes=VMEM_LIMIT),
        name="in_proj",
    )(x2d, w)


def _rw_proj_kernel(x_ref, prev_ref, w_ref, mu_ref, o_ref, buf, *, tm):
    x = x_ref[0]
    buf[HIST:HIST + tm, :] = x
    buf[0:HIST, :] = jnp.where(pl.program_id(1) == 0, 0.0, prev_ref[0])
    dx = buf[pl.ds(HIST - 1, tm), :] - x
    for g in range(4):
        lhs = x + mu_ref[g:g + 1, :] * dx
        o_ref[0, :, g * RW_W:(g + 1) * RW_W] = _dot(lhs, w_ref[:, g * RW_W:(g + 1) * RW_W])
    w_lo = w_ref[:, 4 * RW_W:4 * RW_W + LANES]
    lo_w = _dot(x + mu_ref[4:5, :] * dx, w_lo)
    lo_a = _dot(x + mu_ref[5:6, :] * dx, w_lo)
    lane = lax.broadcasted_iota(jnp.int32, lo_w.shape, 1)
    o_ref[0, :, 4 * RW_W:4 * RW_W + LANES] = jnp.where(lane < RW_LORA, lo_w, lo_a)


def _rw_project(x, w, mu, tm):
    b, s, d = x.shape
    n = w.shape[1]
    per = tm // HIST
    return pl.pallas_call(
        functools.partial(_rw_proj_kernel, tm=tm),
        grid=(b, s // tm),
        in_specs=[pl.BlockSpec((1, tm, d), lambda i, j: (i, j, 0)),
                  pl.BlockSpec((1, HIST, d), lambda i, j: (i, jnp.maximum(j * per - 1, 0), 0)),
                  pl.BlockSpec((d, n), lambda i, j: (0, 0)),
                  pl.BlockSpec(mu.shape, lambda i, j: (0, 0))],
        out_specs=pl.BlockSpec((1, tm, n), lambda i, j: (i, j, 0)),
        out_shape=jax.ShapeDtypeStruct((b, s, n), F32),
        scratch_shapes=[pltpu.VMEM((tm + HIST, d), F32)],
        compiler_params=pltpu.CompilerParams(dimension_semantics=("parallel", "parallel"),
                                             vmem_limit_bytes=VMEM_LIMIT),
        name="rw_in_proj",
    )(x, x, w, mu)


def _dn_kernel(h_ref, convw_ref, alog_ref, dtb_ref, normw_ref, y_ref, hist, state):
    c = CHUNK
    nc = h_ref.shape[0]

    @pl.when(pl.program_id(1) == 0)
    def _():
        hist[...] = jnp.zeros(hist.shape, F32)
        state[...] = jnp.zeros(state.shape, F32)

    qkv_tiles = (2 * DN_QK + DN_V) // LANES
    z_tile0 = qkv_tiles
    eye, row, col = _chunk_iota(c, permuted=True)
    causal = row >= col
    strict = row > col
    masks = _doubling_masks(row, col, c)
    causal_f = causal.astype(F32)

    heads = range(DN_HEADS)
    units = [(n, h) for n in range(nc) for h in heads]
    ids = range(len(units))
    qkv, gcum, gcum_t, beta_t = [], [], [], []
    tail = hist[...]
    for n in range(nc):
        y_n, tail = _causal_conv_silu(tail, _col_tiles(h_ref, n, 0, qkv_tiles), convw_ref)
        qkv.append(y_n)
        gates = h_ref[n, z_tile0 + DN_V // LANES]
        beta_t.append(jax.nn.sigmoid(gates))
        g_t = -(jnp.exp(alog_ref[...]) * jax.nn.softplus(gates + dtb_ref[...]))
        gcum.append(_dot_f32(causal_f, g_t))
        gcum_t.append(gcum[n].T)
    hist[...] = tail

    q = [qkv[n][:, h * DN_DK:(h + 1) * DN_DK] for n, h in units]
    k = [qkv[n][:, DN_QK + h * DN_DK:DN_QK + (h + 1) * DN_DK] for n, h in units]
    v = [qkv[n][:, 2 * DN_QK + h * DN_DV:2 * DN_QK + (h + 1) * DN_DV] for n, h in units]
    q = [x * (lax.rsqrt(jnp.sum(x * x, -1, keepdims=True) + 1e-6) * DN_DK ** -0.5) for x in q]
    k = [x * lax.rsqrt(jnp.sum(x * x, -1, keepdims=True) + 1e-6) for x in k]
    beta = [beta_t[n][:, h:h + 1] for n, h in units]
    g_col = [gcum[n][:, DN_HEADS + h:DN_HEADS + h + 1] for n, h in units]
    g_row = [gcum_t[n][DN_HEADS + h:DN_HEADS + h + 1, :] for n, h in units]
    g_last = [g[c - 1:c, :] for g in g_col]
    decay = [jnp.exp(jnp.where(causal, g_col[i] - g_row[i], -jnp.inf)) for i in ids]
    exp_g = [jnp.exp(g) for g in g_col]
    k_beta = [k[i] * beta[i] for i in ids]
    a = [jnp.where(strict, _dot_nt(k_beta[i], k[i]) * decay[i], 0.0) for i in ids]
    qk = [_dot_nt(q[i], k[i]) * decay[i] for i in ids]
    t = _unit_lower_inverse_many(a, eye, masks)
    uw = [_dot(t[i], jnp.concatenate([v[i] * beta[i], k_beta[i] * exp_g[i]], axis=1)) for i in ids]
    q_dec = [q[i] * exp_g[i] for i in ids]
    k_dec = [k[i] * jnp.exp(g_last[i] - g_col[i]) for i in ids]
    s_cur = [state[h] for h in heads]
    o = []
    for n in range(nc):
        idn = [n * DN_HEADS + h for h in heads]
        v_new = [uw[i][:, :DN_DV] - _dot(uw[i][:, DN_DV:], s_cur[h]) for h, i in zip(heads, idn)]
        o += [_dot(q_dec[i], s_cur[h]) + _dot(qk[i], v_new[h]) for h, i in zip(heads, idn)]
        s_cur = [s_cur[h] * jnp.exp(g_last[i]) + _dot_tn(k_dec[i], v_new[h]) for h, i in zip(heads, idn)]
    for h in heads:
        state[h] = s_cur[h]
    for i, (n, h) in enumerate(units):
        z = h_ref[n, z_tile0 + h]
        on = o[i] * lax.rsqrt(jnp.mean(o[i] * o[i], -1, keepdims=True) + 1e-6) * normw_ref[...]
        _store_natural_order(y_ref, n, h, on * _silu(z))


def _dn_mixer(h, b, nc, conv_w, alog_row, dtb_row, norm_w):
    nch = h.shape[0] // b // nc
    ncols = 2 * DN_QK + DN_V
    return pl.pallas_call(
        _dn_kernel,
        grid=(b, nch),
        in_specs=[pl.BlockSpec((nc,) + h.shape[1:], lambda i, j: (i * nch + j, 0, 0, 0)),
                  pl.BlockSpec(conv_w.shape, lambda i, j: (0, 0)),
                  pl.BlockSpec(alog_row.shape, lambda i, j: (0, 0)),
                  pl.BlockSpec(dtb_row.shape, lambda i, j: (0, 0)),
                  pl.BlockSpec(norm_w.shape, lambda i, j: (0, 0))],
        out_specs=pl.BlockSpec((nc, DN_V // LANES, CHUNK, LANES), lambda i, j: (i * nch + j, 0, 0, 0)),
        out_shape=jax.ShapeDtypeStruct((h.shape[0], DN_V // LANES, CHUNK, LANES), F32),
        scratch_shapes=[pltpu.VMEM(((CONV_K - 1) * SUBLANES, ncols), F32),
                        pltpu.VMEM((DN_HEADS, DN_DK, DN_DV), F32)],
        compiler_params=pltpu.CompilerParams(dimension_semantics=("parallel", "arbitrary"),
                                             vmem_limit_bytes=VMEM_LIMIT),
        name="dn_mixer",
    )(h, conv_w, alog_row, dtb_row, norm_w)


def _rw_kernel(h_ref, wup_ref, aup_ref, w0_ref, a0_ref, kk_ref, ka_ref, rk_ref, gnw_ref, gnb_ref,
               y_ref, state):
    c = CHUNK
    d = RW_HEAD
    nc = y_ref.shape[0]

    @pl.when(pl.program_id(1) == 0)
    def _():
        state[...] = jnp.zeros(state.shape, F32)

    eye, row, col = _chunk_iota(c)
    causal = row >= col
    strict = row > col
    masks = _doubling_masks(row, col, c)
    causal_f = causal.astype(F32)
    heads = range(RW_HEADS)
    units = [(n, h) for n in range(nc) for h in heads]
    ids = range(len(units))
    sl = [slice(h * d, (h + 1) * d) for h in heads]

    r_all, v_all, z_all, a_all, logd, lcum_all, kk_all, k2_all, rkr_all = ([] for _ in range(9))
    for n in range(nc):
        rows = slice(n * c, (n + 1) * c)
        r_all.append(h_ref[0, rows, 0:RW_W])
        k_n = h_ref[0, rows, RW_W:2 * RW_W]
        v_all.append(h_ref[0, rows, 2 * RW_W:3 * RW_W])
        z_all.append(h_ref[0, rows, 3 * RW_W:4 * RW_W])
        lo = h_ref[0, rows, 4 * RW_W:4 * RW_W + LANES]
        w_log = -jax.nn.softplus(-(w0_ref[...] + _dot(jnp.tanh(lo[:, :RW_LORA]), wup_ref[...]))) - 0.5
        a_all.append(jax.nn.sigmoid(a0_ref[...] + _dot(lo[:, RW_LORA:], aup_ref[...])))
        logd.append(-jnp.exp(w_log))
        lcum_all.append(_dot_f32(causal_f, logd[n]))
        kk_all.append(k_n * kk_ref[...])
        k2_all.append(k_n * (1.0 + (a_all[n] - 1.0) * ka_ref[...]))
        rkr_all.append(r_all[n] * k2_all[n] * rk_ref[...])

    v = [v_all[n][:, sl[h]] for n, h in units]
    k2 = [k2_all[n][:, sl[h]] for n, h in units]
    kk = [kk_all[n][:, sl[h]] for n, h in units]
    kk = [x * lax.rsqrt(jnp.sum(x * x, -1, keepdims=True) + 1e-6) for x in kk]
    lcum = [lcum_all[n][:, sl[h]] for n, h in units]
    l_last = [x[c - 1:c, :] for x in lcum]
    alpha = [-(kk[i] * a_all[n][:, sl[h]]) for i, (n, h) in enumerate(units)]
    p_inv = [jnp.exp(-x) for x in lcum]
    lhs = [jnp.concatenate([kk[i] * jnp.exp(lcum[i] - logd[n][:, sl[h]]), r_all[n][:, sl[h]] * jnp.exp(lcum[i])],
                           axis=0) for i, (n, h) in enumerate(units)]
    rhs = [jnp.concatenate([alpha[i] * p_inv[i], k2[i] * p_inv[i]], axis=0) for i in ids]
    m = [_dot_nt(lhs[i], rhs[i]) for i in ids]
    a_ab = [jnp.where(strict, -x[:c, :c], 0.0) for x in m]
    a_ak = [jnp.where(strict, x[:c, c:], 0.0) for x in m]
    a_r = [jnp.concatenate([jnp.where(causal, x[c:, :c], 0.0), jnp.where(causal, x[c:, c:], 0.0)], axis=1)
           for x in m]
    t = _unit_lower_inverse_many(a_ab, eye, masks)
    akv = [_dot(a_ak[i], v[i]) for i in ids]
    p_tail = [jnp.exp(l_last[i] - lcum[i]) for i in ids]
    tail = [jnp.concatenate([alpha[i] * p_tail[i], k2[i] * p_tail[i]], axis=0) for i in ids]
    p_last = [jnp.exp(x) for x in l_last]

    s_cur = [state[h] for h in heads]
    y = []
    for n in range(nc):
        idn = [n * RW_HEADS + h for h in heads]
        lhs_s = [_dot_nt(lhs[i], s_cur[h]) for h, i in zip(heads, idn)]
        u = [_dot(t[i], lhs_s[h][:c] + akv[i]) for h, i in zip(heads, idn)]
        uv = [jnp.concatenate([u[h], v[i]], axis=0) for h, i in zip(heads, idn)]
        y += [lhs_s[h][c:] + _dot(a_r[i], uv[h]) for h, i in zip(heads, idn)]
        s_cur = [s_cur[h] * p_last[i] + _dot_tn(uv[h], tail[i]) for h, i in zip(heads, idn)]
    for h in heads:
        state[h] = s_cur[h]
    per_tile = LANES // d
    for n in range(nc):
        out = []
        for h in heads:
            i = n * RW_HEADS + h
            mu = jnp.mean(y[i], -1, keepdims=True)
            var = jnp.mean(jnp.square(y[i] - mu), -1, keepdims=True)
            yn = (y[i] - mu) * lax.rsqrt(var + RW_GN_EPS) * gnw_ref[:, sl[h]] + gnb_ref[:, sl[h]]
            bonus = jnp.sum(rkr_all[n][:, sl[h]], -1, keepdims=True) * v[i]
            out.append((yn + bonus) * _silu(z_all[n][:, sl[h]]))
        for tile in range(RW_W // LANES):
            y_ref[n, tile] = jnp.concatenate(out[tile * per_tile:(tile + 1) * per_tile], axis=1)


def _rw_mixer(h, nc, wup, aup, w0, a0, k_k, k_a, r_k, gn_w, gn_b):
    b, s, n = h.shape
    nch = s // CHUNK // nc
    full = lambda arr: pl.BlockSpec(arr.shape, lambda i, j: (0,) * arr.ndim)
    params = (wup, aup, w0, a0, k_k, k_a, r_k, gn_w, gn_b)
    return pl.pallas_call(
        _rw_kernel,
        grid=(b, nch),
        in_specs=[pl.BlockSpec((1, nc * CHUNK, n), lambda i, j: (i, j, 0))] + [full(a) for a in params],
        out_specs=pl.BlockSpec((nc, RW_W // LANES, CHUNK, LANES), lambda i, j: (i * nch + j, 0, 0, 0)),
        out_shape=jax.ShapeDtypeStruct((b * nch * nc, RW_W // LANES, CHUNK, LANES), F32),
        scratch_shapes=[pltpu.VMEM((RW_HEADS, RW_HEAD, RW_HEAD), F32)],
        compiler_params=pltpu.CompilerParams(dimension_semantics=("parallel", "arbitrary"),
                                             vmem_limit_bytes=VMEM_LIMIT),
        name="rw_mixer",
    )(h, *params)


def _ml_kernel(h_ref, convw_ref, ib_ref, fb_ref, gnw_ref, y_ref, hist, cstate, nstate, mstate):
    c = CHUNK
    nc = h_ref.shape[0]

    @pl.when(pl.program_id(1) == 0)
    def _():
        hist[...] = jnp.zeros(hist.shape, F32)
        cstate[...] = jnp.zeros(cstate.shape, F32)
        nstate[...] = jnp.zeros(nstate.shape, F32)
        mstate[...] = jnp.full(mstate.shape, -jnp.inf, F32)

    qk_tiles = 2 * ML_QK // LANES
    v_tiles = ML_V // LANES
    _, row, col = _chunk_iota(c, permuted=True)
    causal = row >= col
    causal_f = causal.astype(F32)
    heads = range(ML_HEADS)
    units = [(n, h) for n in range(nc) for h in heads]
    ids = range(len(units))

    qk_all, i_t, i_tt, bcum, bcum_t = [], [], [], [], []
    tail = hist[...]
    for n in range(nc):
        y_n, tail = _causal_conv_silu(tail, _col_tiles(h_ref, n, 0, qk_tiles), convw_ref)
        qk_all.append(y_n)
        gates = h_ref[n, qk_tiles + 3 * v_tiles]
        i_t.append(gates + ib_ref[...])
        i_tt.append(i_t[n].T)
        bcum.append(_dot_f32(causal_f, jax.nn.log_sigmoid(gates + fb_ref[...])))
        bcum_t.append(bcum[n].T)
    hist[...] = tail

    q = [qk_all[n][:, h * ML_DQK:(h + 1) * ML_DQK] for n, h in units]
    k = [qk_all[n][:, ML_QK + h * ML_DQK:ML_QK + (h + 1) * ML_DQK] * ML_DQK ** -0.5 for n, h in units]
    v = [h_ref[n, qk_tiles + h] for n, h in units]
    b_col = [bcum[n][:, ML_HEADS + h:ML_HEADS + h + 1] for n, h in units]
    b_row = [bcum_t[n][ML_HEADS + h:ML_HEADS + h + 1, :] for n, h in units]
    i_col = [i_t[n][:, h:h + 1] for n, h in units]
    i_row = [i_tt[n][h:h + 1, :] for n, h in units]
    b_last = [x[c - 1:c, :] for x in b_col]
    d_log = [jnp.where(causal, b_col[i] - b_row[i] + i_row[i], -jnp.inf) for i in ids]
    m_intra = [jnp.max(x, -1, keepdims=True) for x in d_log]
    qk = [_dot_nt(q[i], k[i]) for i in ids]

    m_prev, m_t = [], []
    m_cur = [mstate[h][0:1, 0:1] for h in heads]
    for n in range(nc):
        for h in heads:
            i = n * ML_HEADS + h
            m_prev.append(m_cur[h])
            m_t.append(jnp.maximum(m_cur[h] + b_col[i], m_intra[i]))
            m_cur[h] = m_t[i][c - 1:c, :]
    m_new = [x[c - 1:c, :] for x in m_t]
    inter = [jnp.exp(m_prev[i] + b_col[i] - m_t[i]) for i in ids]
    w_qk = [jnp.exp(d_log[i] - m_t[i]) * qk[i] for i in ids]
    carry = [jnp.exp(m_prev[i] + b_last[i] - m_new[i]) for i in ids]
    k_w = [k[i] * jnp.exp(b_last[i] - b_col[i] + i_col[i] - m_new[i]) for i in ids]
    kv = [_dot_tn(k_w[i], v[i]) for i in ids]
    k_sum = [jnp.sum(k_w[i], 0, keepdims=True) for i in ids]

    c_in, n_in = [], []
    c_cur = [cstate[h] for h in heads]
    n_cur = [nstate[h] for h in heads]
    for n in range(nc):
        for h in heads:
            i = n * ML_HEADS + h
            c_in.append(c_cur[h])
            n_in.append(n_cur[h])
            c_cur[h] = c_cur[h] * carry[i] + kv[i]
            n_cur[h] = n_cur[h] * carry[i] + k_sum[i]
    for h in heads:
        cstate[h] = c_cur[h]
        nstate[h] = n_cur[h]
        mstate[h] = jnp.broadcast_to(m_cur[h], mstate.shape[1:])

    num = [inter[i] * _dot(q[i], c_in[i]) + _dot(w_qk[i], v[i]) for i in ids]
    den = [inter[i] * jnp.sum(q[i] * n_in[i], -1, keepdims=True) + jnp.sum(w_qk[i], -1, keepdims=True)
           for i in ids]
    h_tilde = [num[i] / jnp.maximum(jnp.abs(den[i]), jnp.exp(-m_t[i])) for i in ids]
    for i, (n, h) in enumerate(units):
        og = h_ref[n, qk_tiles + v_tiles + h]
        z = h_ref[n, qk_tiles + 2 * v_tiles + h]
        xg = jax.nn.sigmoid(og) * h_tilde[i]
        mu = jnp.mean(xg, -1, keepdims=True)
        var = jnp.mean(jnp.square(xg - mu), -1, keepdims=True)
        xn = (xg - mu) * lax.rsqrt(var + 1e-6) * gnw_ref[:, h * ML_DV:(h + 1) * ML_DV]
        _store_natural_order(y_ref, n, h, xn * _silu(z))


def _ml_mixer(h, b, nc, conv_w, ib_row, fb_row, gn_w):
    nch = h.shape[0] // b // nc
    return pl.pallas_call(
        _ml_kernel,
        grid=(b, nch),
        in_specs=[pl.BlockSpec((nc,) + h.shape[1:], lambda i, j: (i * nch + j, 0, 0, 0)),
                  pl.BlockSpec(conv_w.shape, lambda i, j: (0, 0)),
                  pl.BlockSpec(ib_row.shape, lambda i, j: (0, 0)),
                  pl.BlockSpec(fb_row.shape, lambda i, j: (0, 0)),
                  pl.BlockSpec(gn_w.shape, lambda i, j: (0, 0))],
        out_specs=pl.BlockSpec((nc, ML_V // LANES, CHUNK, LANES), lambda i, j: (i * nch + j, 0, 0, 0)),
        out_shape=jax.ShapeDtypeStruct((h.shape[0], ML_V // LANES, CHUNK, LANES), F32),
        scratch_shapes=[pltpu.VMEM(((CONV_K - 1) * SUBLANES, 2 * ML_QK), F32),
                        pltpu.VMEM((ML_HEADS, ML_DQK, ML_DV), F32),
                        pltpu.VMEM((ML_HEADS, 1, ML_DQK), F32),
                        pltpu.VMEM((ML_HEADS, SUBLANES, LANES), F32)],
        compiler_params=pltpu.CompilerParams(dimension_semantics=("parallel", "arbitrary"),
                                             vmem_limit_bytes=VMEM_LIMIT),
        name="ml_mixer",
    )(h, conv_w, ib_row, fb_row, gn_w)


def _post_kernel(x_ref, y_ref, p_ref, wout_ref, wg_ref, wp_ref, lng_ref, lnb_ref, pnw_ref, o_ref):
    y = jnp.concatenate([jnp.concatenate([y_ref[ch, t] for t in range(y_ref.shape[1])], axis=1)
                         for ch in range(y_ref.shape[0])], axis=0)
    r = DEEPNORM_ALPHA * x_ref[...] + _dot(y, wout_ref[...])
    mu = jnp.mean(r, -1, keepdims=True)
    var = jnp.mean(jnp.square(r - mu), -1, keepdims=True)
    xn = (r - mu) * lax.rsqrt(var + LN_EPS) * lng_ref[...] + lnb_ref[...]
    gate = jax.nn.sigmoid(_dot(xn, wg_ref[...]))
    pp = _dot(p_ref[...], wp_ref[...])
    pn = pp * lax.rsqrt(jnp.mean(pp * pp, -1, keepdims=True) + 1e-6) * pnw_ref[...]
    o_ref[...] = xn + gate * pn


def _post(x2d, y4d, p2d, w_out, w_gate, w_proj, ln_g, ln_b, pn_w, tm):
    m, d = x2d.shape
    y_spec = pl.BlockSpec((tm // CHUNK,) + y4d.shape[1:], lambda i: (i, 0, 0, 0))
    tile = lambda arr: pl.BlockSpec((tm, arr.shape[1]), lambda i: (i, 0))
    full = lambda arr: pl.BlockSpec(arr.shape, lambda i: (0, 0))
    params = (w_out, w_gate, w_proj, ln_g, ln_b, pn_w)
    return pl.pallas_call(
        _post_kernel,
        grid=(m // tm,),
        in_specs=[tile(x2d), y_spec, tile(p2d)] + [full(a) for a in params],
        out_specs=pl.BlockSpec((tm, d), lambda i: (i, 0)),
        out_shape=jax.ShapeDtypeStruct((m, d), F32),
        compiler_params=pltpu.CompilerParams(dimension_semantics=("parallel",),
                                             vmem_limit_bytes=VMEM_LIMIT),
        name="post_block",
    )(x2d, y4d, p2d, *params)


def _pad_cols(w, n):
    return jnp.pad(w, ((0, 0), (0, n - w.shape[1])))


def _lane_row(vec, offset):
    return jnp.zeros((1, LANES), F32).at[0, offset:offset + vec.shape[0]].set(vec.astype(F32))


def _row_tile(m):
    for t in (512, 256, 128, 64):
        if m % t == 0:
            return t
    raise ValueError(f"token count {m} must be a multiple of {CHUNK}")


def kernel(x, p, ln_g, ln_b, ple_w_proj, ple_norm_w, ple_w_gate, dn_w_in, dn_conv_w, dn_a_log, dn_dt_bias, dn_norm_w, dn_w_out, rw_w_in, rw_mu, rw_w0, rw_w_lora_up, rw_a0, rw_a_lora_up, rw_k_k, rw_k_a, rw_r_k, rw_gn_w, rw_gn_b, rw_w_out, ml_w_in, ml_conv_w, ml_i_bias, ml_f_bias, ml_gn_w, ml_w_out):
    b, s, d = x.shape
    assert d == D_MODEL and s % CHUNK == 0
    m = b * s
    tm_proj = _row_tile(m) // 2 if _row_tile(m) > CHUNK else CHUNK
    tm_post = _row_tile(m)
    tm_rw = _row_tile(s) // 2 if _row_tile(s) > CHUNK else CHUNK
    chunks_per_step = lambda want: next(n for n in (want, 2, 1) if (s // CHUNK) % n == 0)
    bf = lambda w: w.astype(MXU_DTYPE)
    row = lambda v: v.reshape(1, -1).astype(F32)
    x2d = x.reshape(m, d)
    for i in range(DEPTH):
        kind, j = i % 3, i // 3
        if kind == 0:
            n_pad = 2 * DN_QK + 2 * DN_V + LANES
            h = _project(x2d, bf(_pad_cols(dn_w_in[j], n_pad)), tm_proj)
            y = _dn_mixer(h, b, chunks_per_step(DN_CHUNKS), dn_conv_w[j], _lane_row(dn_a_log[j], DN_HEADS),
                          _lane_row(dn_dt_bias[j], DN_HEADS), row(dn_norm_w[j]))
            w_out = dn_w_out[j]
        elif kind == 1:
            r_w, wl_w, k_w, v_w, al_w, z_w = jnp.split(
                rw_w_in[j], np.cumsum([RW_W, RW_LORA, RW_W, RW_W, RW_LORA]).tolist(), axis=1)
            w_cat = jnp.concatenate([r_w, k_w, v_w, z_w, wl_w, al_w], axis=1)
            mu = rw_mu[j]
            mu_cat = jnp.stack([mu[0], mu[2], mu[3], mu[5], mu[1], mu[4], mu[0], mu[0]], axis=0)
            h = _rw_project(x2d.reshape(b, s, d), bf(w_cat), mu_cat, tm_rw)
            y = _rw_mixer(h, chunks_per_step(RW_CHUNKS), bf(rw_w_lora_up[j]), bf(rw_a_lora_up[j]), row(rw_w0[j]),
                          row(rw_a0[j]), row(rw_k_k[j]), row(rw_k_a[j]), row(rw_r_k[j]), row(rw_gn_w[j]),
                          row(rw_gn_b[j]))
            w_out = rw_w_out[j]
        else:
            n_pad = 2 * ML_QK + 3 * ML_V + LANES
            h = _project(x2d, bf(_pad_cols(ml_w_in[j], n_pad)), tm_proj)
            y = _ml_mixer(h, b, chunks_per_step(ML_CHUNKS), ml_conv_w[j], _lane_row(ml_i_bias[j], 0),
                          _lane_row(ml_f_bias[j], ML_HEADS), row(ml_gn_w[j]))
            w_out = ml_w_out[j]
        x2d = _post(x2d, y, p[i].reshape(m, D_PLE), bf(w_out), bf(ple_w_gate[i]),
                    bf(ple_w_proj[i]), row(ln_g[i]), row(ln_b[i]), row(ple_norm_w[i]), tm_post)
    return x2d.reshape(b, s, d)
```

```python
import functools

import jax
import jax.numpy as jnp
import numpy as np
from jax import lax
from jax.experimental import pallas as pl
from jax.experimental.pallas import tpu as pltpu

F32 = jnp.float32
MXU_DTYPE = jnp.bfloat16

LANES = 128
SUBLANES = 8
VMEM_LIMIT = 48 * 1024 * 1024

DEPTH = 4
D_MODEL = 1024
D_PLE = 256
CONV_K = 4
CHUNK = 64
LN_EPS = 1e-5
DN_HEADS, DN_DK, DN_DV = 8, 128, 128
DN_QK = DN_HEADS * DN_DK
DN_V = DN_HEADS * DN_DV
RW_HEAD = 64
RW_HEADS = D_MODEL // RW_HEAD
RW_W = RW_HEADS * RW_HEAD
RW_LORA = 64
RW_GN_EPS = 64e-5
ML_HEADS, ML_DQK, ML_DV = 8, 64, 128
ML_QK = ML_HEADS * ML_DQK
ML_V = ML_HEADS * ML_DV
DEEPNORM_ALPHA = (2.0 * DEPTH) ** 0.25
HIST = SUBLANES
DN_CHUNKS = 4
RW_CHUNKS = 4
ML_CHUNKS = 1


def _dot(a, b):
    return lax.dot_general(a.astype(MXU_DTYPE), b.astype(MXU_DTYPE), (((1,), (0,)), ((), ())),
                           preferred_element_type=F32)


def _dot_nt(a, b):
    return lax.dot_general(a.astype(MXU_DTYPE), b.astype(MXU_DTYPE), (((1,), (1,)), ((), ())),
                           preferred_element_type=F32)


def _dot_tn(a, b):
    return lax.dot_general(a.astype(MXU_DTYPE), b.astype(MXU_DTYPE), (((0,), (0,)), ((), ())),
                           preferred_element_type=F32)


def _dot_f32(a, b):
    return lax.dot_general(a, b, (((1,), (0,)), ((), ())), precision=lax.Precision.HIGHEST,
                           preferred_element_type=F32)


def _silu(x):
    return x * jax.nn.sigmoid(x)


def _perm_time(p):
    return ((p & (SUBLANES - 1)) << 3) | (p >> 3)


def _chunk_iota(n, permuted=False):
    row = lax.broadcasted_iota(jnp.int32, (n, n), 0)
    col = lax.broadcasted_iota(jnp.int32, (n, n), 1)
    eye = (row == col).astype(F32)
    if permuted:
        row, col = _perm_time(row), _perm_time(col)
    return eye, row, col


def _doubling_masks(row, col, n):
    masks = []
    s, shift = 1, 0
    while s < n:
        same = (row >> (shift + 1)) == (col >> (shift + 1))
        masks.append(same & ((row & s) != 0) & ((col & s) == 0))
        s, shift = 2 * s, shift + 1
    return masks


def _unit_lower_inverse_many(a_list, eye, masks):
    t = [eye - jnp.where(masks[0], a, 0.0) for a in a_list]
    for m in masks[1:]:
        ta = [_dot(ti, jnp.where(m, a, 0.0)) for ti, a in zip(t, a_list)]
        t = [ti - _dot(tai, ti) for ti, tai in zip(t, ta)]
    return t


def _col_tiles(ref, chunk, first, count):
    return jnp.concatenate([ref[chunk, first + i] for i in range(count)], axis=1)


def _causal_conv_silu(prev_tail, x, w_ref):
    keep = CONV_K - 1
    tail = x[CHUNK - keep * SUBLANES:, :]
    sub = lax.broadcasted_iota(jnp.int32, tail.shape, 0) & (SUBLANES - 1)
    mixed = jnp.where(sub == SUBLANES - 1, prev_tail, tail)
    wrapped = jnp.concatenate([pltpu.roll(mixed[i * SUBLANES:(i + 1) * SUBLANES, :], 1, 0) for i in range(keep)],
                              axis=0)
    acc = w_ref[CONV_K - 1:CONV_K, :] * x
    for s in range(1, CONV_K):
        shifted = jnp.concatenate([wrapped[(keep - s) * SUBLANES:, :], x[:CHUNK - s * SUBLANES, :]], axis=0)
        acc = acc + w_ref[CONV_K - 1 - s:CONV_K - s, :] * shifted
    return _silu(acc), tail


def _store_natural_order(y_ref, chunk, tile, y):
    for b in range(SUBLANES):
        y_ref[chunk, tile, pl.ds(b, CHUNK // SUBLANES, stride=SUBLANES), :] = y[b * SUBLANES:(b + 1) * SUBLANES, :]


PROJ_COLS = 2 * LANES


def _proj_kernel(x_ref, w_ref, o_ref, *, tm):
    x = x_ref[...].astype(MXU_DTYPE)
    n = w_ref.shape[1]
    for c0 in range(0, n, PROJ_COLS):
        width = min(PROJ_COLS, n - c0)
        res = lax.dot_general(x, w_ref[:, c0:c0 + width], (((1,), (0,)), ((), ())), preferred_element_type=F32)
        for ch in range(tm // CHUNK):
            for a in range(CHUNK // SUBLANES):
                r0 = ch * CHUNK + a * SUBLANES
                for t in range(width // LANES):
                    o_ref[ch, c0 // LANES + t, pl.ds(a, SUBLANES, stride=SUBLANES), :] = (
                        res[r0:r0 + SUBLANES, t * LANES:(t + 1) * LANES])


def _project(x2d, w, tm):
    m, k = x2d.shape
    n = w.shape[1]
    return pl.pallas_call(
        functools.partial(_proj_kernel, tm=tm),
        grid=(m // tm,),
        in_specs=[pl.BlockSpec((tm, k), lambda i: (i, 0)),
                  pl.BlockSpec((k, n), lambda i: (0, 0))],
        out_specs=pl.BlockSpec((tm // CHUNK, n // LANES, CHUNK, LANES), lambda i: (i, 0, 0, 0)),
        out_shape=jax.ShapeDtypeStruct((m // CHUNK, n // LANES, CHUNK, LANES), F32),
        compiler_params=pltpu.CompilerParams(dimension_semantics=("parallel",),
                                             vmem_limit_bytes=VMEM_LIMIT),
        name="in_proj",
    )(x2d, w)


def _rw_proj_kernel(x_ref, prev_ref, w_ref, mu_ref, o_ref, buf, *, tm):
    x = x_ref[0]
    buf[HIST:HIST + tm, :] = x
    buf[0:HIST, :] = jnp.where(pl.program_id(1) == 0, 0.0, prev_ref[0])
    dx = buf[pl.ds(HIST - 1, tm), :] - x
    for g in range(4):
        lhs = x + mu_ref[g:g + 1, :] * dx
        o_ref[0, :, g * RW_W:(g + 1) * RW_W] = _dot(lhs, w_ref[:, g * RW_W:(g + 1) * RW_W])
    w_lo = w_ref[:, 4 * RW_W:4 * RW_W + LANES]
    lo_w = _dot(x + mu_ref[4:5, :] * dx, w_lo)
    lo_a = _dot(x + mu_ref[5:6, :] * dx, w_lo)
    lane = lax.broadcasted_iota(jnp.int32, lo_w.shape, 1)
    o_ref[0, :, 4 * RW_W:4 * RW_W + LANES] = jnp.where(lane < RW_LORA, lo_w, lo_a)


def _rw_project(x, w, mu, tm):
    b, s, d = x.shape
    n = w.shape[1]
    per = tm // HIST
    return pl.pallas_call(
        functools.partial(_rw_proj_kernel, tm=tm),
        grid=(b, s // tm),
        in_specs=[pl.BlockSpec((1, tm, d), lambda i, j: (i, j, 0)),
                  pl.BlockSpec((1, HIST, d), lambda i, j: (i, jnp.maximum(j * per - 1, 0), 0)),
                  pl.BlockSpec((d, n), lambda i, j: (0, 0)),
                  pl.BlockSpec(mu.shape, lambda i, j: (0, 0))],
        out_specs=pl.BlockSpec((1, tm, n), lambda i, j: (i, j, 0)),
        out_shape=jax.ShapeDtypeStruct((b, s, n), F32),
        scratch_shapes=[pltpu.VMEM((tm + HIST, d), F32)],
        compiler_params=pltpu.CompilerParams(dimension_semantics=("parallel", "parallel"),
                                             vmem_limit_bytes=VMEM_LIMIT),
        name="rw_in_proj",
    )(x, x, w, mu)


def _dn_kernel(h_ref, convw_ref, alog_ref, dtb_ref, normw_ref, y_ref, hist, state):
    c = CHUNK
    nc = h_ref.shape[0]

    @pl.when(pl.program_id(1) == 0)
    def _():
        hist[...] = jnp.zeros(hist.shape, F32)
        state[...] = jnp.zeros(state.shape, F32)

    qkv_tiles = (2 * DN_QK + DN_V) // LANES
    z_tile0 = qkv_tiles
    eye, row, col = _chunk_iota(c, permuted=True)
    causal = row >= col
    strict = row > col
    masks = _doubling_masks(row, col, c)
    causal_f = causal.astype(F32)

    heads = range(DN_HEADS)
    units = [(n, h) for n in range(nc) for h in heads]
    ids = range(len(units))
    qkv, gcum, gcum_t, beta_t = [], [], [], []
    tail = hist[...]
    for n in range(nc):
        y_n, tail = _causal_conv_silu(tail, _col_tiles(h_ref, n, 0, qkv_tiles), convw_ref)
        qkv.append(y_n)
        gates = h_ref[n, z_tile0 + DN_V // LANES]
        beta_t.append(jax.nn.sigmoid(gates))
        g_t = -(jnp.exp(alog_ref[...]) * jax.nn.softplus(gates + dtb_ref[...]))
        gcum.append(_dot_f32(causal_f, g_t))
        gcum_t.append(gcum[n].T)
    hist[...] = tail

    q = [qkv[n][:, h * DN_DK:(h + 1) * DN_DK] for n, h in units]
    k = [qkv[n][:, DN_QK + h * DN_DK:DN_QK + (h + 1) * DN_DK] for n, h in units]
    v = [qkv[n][:, 2 * DN_QK + h * DN_DV:2 * DN_QK + (h + 1) * DN_DV] for n, h in units]
    q = [x * (lax.rsqrt(jnp.sum(x * x, -1, keepdims=True) + 1e-6) * DN_DK ** -0.5) for x in q]
    k = [x * lax.rsqrt(jnp.sum(x * x, -1, keepdims=True) + 1e-6) for x in k]
    beta = [beta_t[n][:, h:h + 1] for n, h in units]
    g_col = [gcum[n][:, DN_HEADS + h:DN_HEADS + h + 1] for n, h in units]
    g_row = [gcum_t[n][DN_HEADS + h:DN_HEADS + h + 1, :] for n, h in units]
    g_last = [g[c - 1:c, :] for g in g_col]
    decay = [jnp.exp(jnp.where(causal, g_col[i] - g_row[i], -jnp.inf)) for i in ids]
    exp_g = [jnp.exp(g) for g in g_col]
    k_beta = [k[i] * beta[i] for i in ids]
    a = [jnp.where(strict, _dot_nt(k_beta[i], k[i]) * decay[i], 0.0) for i in ids]
    qk = [_dot_nt(q[i], k[i]) * decay[i] for i in ids]
    t = _unit_lower_inverse_many(a, eye, masks)
    uw = [_dot(t[i], jnp.concatenate([v[i] * beta[i], k_beta[i] * exp_g[i]], axis=1)) for i in ids]
    q_dec = [q[i] * exp_g[i] for i in ids]
    k_dec = [k[i] * jnp.exp(g_last[i] - g_col[i]) for i in ids]
    s_cur = [state[h] for h in heads]
    o = []
    for n in range(nc):
        idn = [n * DN_HEADS + h for h in heads]
        v_new = [uw[i][:, :DN_DV] - _dot(uw[i][:, DN_DV:], s_cur[h]) for h, i in zip(heads, idn)]
        o += [_dot(q_dec[i], s_cur[h]) + _dot(qk[i], v_new[h]) for h, i in zip(heads, idn)]
        s_cur = [s_cur[h] * jnp.exp(g_last[i]) + _dot_tn(k_dec[i], v_new[h]) for h, i in zip(heads, idn)]
    for h in heads:
        state[h] = s_cur[h]
    for i, (n, h) in enumerate(units):
        z = h_ref[n, z_tile0 + h]
        on = o[i] * lax.rsqrt(jnp.mean(o[i] * o[i], -1, keepdims=True) + 1e-6) * normw_ref[...]
        _store_natural_order(y_ref, n, h, on * _silu(z))


def _dn_mixer(h, b, nc, conv_w, alog_row, dtb_row, norm_w):
    nch = h.shape[0] // b // nc
    ncols = 2 * DN_QK + DN_V
    return pl.pallas_call(
        _dn_kernel,
        grid=(b, nch),
        in_specs=[pl.BlockSpec((nc,) + h.shape[1:], lambda i, j: (i * nch + j, 0, 0, 0)),
                  pl.BlockSpec(conv_w.shape, lambda i, j: (0, 0)),
                  pl.BlockSpec(alog_row.shape, lambda i, j: (0, 0)),
                  pl.BlockSpec(dtb_row.shape, lambda i, j: (0, 0)),
                  pl.BlockSpec(norm_w.shape, lambda i, j: (0, 0))],
        out_specs=pl.BlockSpec((nc, DN_V // LANES, CHUNK, LANES), lambda i, j: (i * nch + j, 0, 0, 0)),
        out_shape=jax.ShapeDtypeStruct((h.shape[0], DN_V // LANES, CHUNK, LANES), F32),
        scratch_shapes=[pltpu.VMEM(((CONV_K - 1) * SUBLANES, ncols), F32),
                        pltpu.VMEM((DN_HEADS, DN_DK, DN_DV), F32)],
        compiler_params=pltpu.CompilerParams(dimension_semantics=("parallel", "arbitrary"),
                                             vmem_limit_bytes=VMEM_LIMIT),
        name="dn_mixer",
    )(h, conv_w, alog_row, dtb_row, norm_w)


def _pair_lane_masks(rows):
    lane = lax.broadcasted_iota(jnp.int32, (rows, LANES), 1)
    return lane < RW_HEAD, lane & (RW_HEAD - 1)


def _rw_kernel(h_ref, wup_ref, aup_ref, w0_ref, a0_ref, kk_ref, ka_ref, rk_ref, gnw_ref, gnb_ref,
               y_ref, state):
    c = CHUNK
    nc = y_ref.shape[0]
    pairs = RW_W // LANES

    @pl.when(pl.program_id(1) == 0)
    def _():
        state[...] = jnp.zeros(state.shape, F32)

    _, row, col = _chunk_iota(c)
    in_a, col2 = _pair_lane_masks(c)
    row2 = lax.broadcasted_iota(jnp.int32, (c, LANES), 0)
    causal2 = row2 >= col2
    strict2 = row2 > col2
    eye2 = (row2 == col2).astype(F32)
    masks2 = _doubling_masks(row2, col2, c)
    causal_f = (row >= col).astype(F32)
    row_big = lax.broadcasted_iota(jnp.int32, (LANES, LANES), 0)
    col_big = lax.broadcasted_iota(jnp.int32, (LANES, LANES), 1)
    same_head = (row_big < RW_HEAD) == (col_big < RW_HEAD)

    def row_bd(x):
        return jnp.concatenate([jnp.where(in_a, x, 0.0), jnp.where(in_a, 0.0, x)], axis=0)

    def half_sums(x):
        sum_a = jnp.sum(jnp.where(in_a, x, 0.0), -1, keepdims=True)
        sum_b = jnp.sum(jnp.where(in_a, 0.0, x), -1, keepdims=True)
        return jnp.where(in_a, sum_a, sum_b)

    units = [(n, p) for n in range(nc) for p in range(pairs)]
    ids = range(len(units))
    tiles = [slice(p * LANES, (p + 1) * LANES) for p in range(pairs)]

    r_all, v_all, z_all, a_all, k2_all, rkr_all, kku_all = ([] for _ in range(7))
    p_incl, p_excl, p_inv, p_tail, p_last = ([] for _ in range(5))
    for n in range(nc):
        rows = slice(n * c, (n + 1) * c)
        r_all.append(h_ref[0, rows, 0:RW_W])
        k_n = h_ref[0, rows, RW_W:2 * RW_W]
        v_all.append(h_ref[0, rows, 2 * RW_W:3 * RW_W])
        z_all.append(h_ref[0, rows, 3 * RW_W:4 * RW_W])
        lo = h_ref[0, rows, 4 * RW_W:4 * RW_W + LANES]
        w_log = -jax.nn.softplus(-(w0_ref[...] + _dot(jnp.tanh(lo[:, :RW_LORA]), wup_ref[...]))) - 0.5
        a_all.append(jax.nn.sigmoid(a0_ref[...] + _dot(lo[:, RW_LORA:], aup_ref[...])))
        logd = -jnp.exp(w_log)
        lcum = _dot_f32(causal_f, logd)
        l_last = lcum[c - 1:c, :]
        p_incl.append(jnp.exp(lcum))
        p_excl.append(jnp.exp(lcum - logd))
        p_inv.append(jnp.exp(-lcum))
        p_tail.append(jnp.exp(l_last - lcum))
        p_last.append(jnp.exp(l_last))
        kku_all.append(k_n * kk_ref[...])
        k2_all.append(k_n * (1.0 + (a_all[n] - 1.0) * ka_ref[...]))
        rkr_all.append(r_all[n] * k2_all[n] * rk_ref[...])

    v = [v_all[n][:, tiles[p]] for n, p in units]
    k2 = [k2_all[n][:, tiles[p]] for n, p in units]
    kku = [kku_all[n][:, tiles[p]] for n, p in units]
    kk = [kku[i] * lax.rsqrt(half_sums(kku[i] * kku[i]) + 1e-6) for i in ids]
    alpha = [-(kk[i] * a_all[n][:, tiles[p]]) for i, (n, p) in enumerate(units)]
    lhs = [jnp.concatenate([kk[i] * p_excl[n][:, tiles[p]], r_all[n][:, tiles[p]] * p_incl[n][:, tiles[p]]], axis=0)
           for i, (n, p) in enumerate(units)]
    rhs = [jnp.concatenate([row_bd(alpha[i] * p_inv[n][:, tiles[p]]), row_bd(k2[i] * p_inv[n][:, tiles[p]])], axis=0)
           for i, (n, p) in enumerate(units)]
    m = [_dot_nt(lhs[i], rhs[i]) for i in ids]
    a_ab = [jnp.where(strict2, -x[:c, :LANES], 0.0) for x in m]
    a_ak = [jnp.where(strict2, x[:c, LANES:], 0.0) for x in m]
    a_r = [jnp.concatenate([jnp.where(causal2, x[c:, :LANES], 0.0), jnp.where(causal2, x[c:, LANES:], 0.0)], axis=1)
           for x in m]
    t = [eye2 - jnp.where(masks2[0], a, 0.0) for a in a_ab]
    for mk in masks2[1:]:
        ta = [_dot(t[i], row_bd(jnp.where(mk, a_ab[i], 0.0))) for i in ids]
        t = [t[i] - _dot(ta[i], row_bd(t[i])) for i in ids]
    v_bd = [row_bd(x) for x in v]
    akv = [_dot(a_ak[i], v_bd[i]) for i in ids]
    tail = [jnp.concatenate([alpha[i] * p_tail[n][:, tiles[p]], k2[i] * p_tail[n][:, tiles[p]]], axis=0)
            for i, (n, p) in enumerate(units)]

    s_cur = [state[p] for p in range(pairs)]
    y = []
    for n in range(nc):
        idn = [n * pairs + p for p in range(pairs)]
        lhs_s = [_dot_nt(lhs[i], s_cur[p]) for p, i in enumerate(idn)]
        u = [_dot(t[i], row_bd(lhs_s[p][:c] + akv[i])) for p, i in enumerate(idn)]
        y += [lhs_s[p][c:] + _dot(a_r[i], jnp.concatenate([row_bd(u[p]), v_bd[i]], axis=0))
              for p, i in enumerate(idn)]
        s_cur = [s_cur[p] * p_last[n][:, tiles[p]]
                 + jnp.where(same_head, _dot_tn(jnp.concatenate([u[p], v[i]], axis=0), tail[i]), 0.0)
                 for p, i in enumerate(idn)]
    for p in range(pairs):
        state[p] = s_cur[p]
    inv_d = 1.0 / RW_HEAD
    for i, (n, p) in enumerate(units):
        mu = half_sums(y[i]) * inv_d
        yc = y[i] - mu
        var = half_sums(yc * yc) * inv_d
        yn = yc * lax.rsqrt(var + RW_GN_EPS) * gnw_ref[:, tiles[p]] + gnb_ref[:, tiles[p]]
        bonus = half_sums(rkr_all[n][:, tiles[p]]) * v[i]
        y_ref[n, p] = (yn + bonus) * _silu(z_all[n][:, tiles[p]])


def _rw_mixer(h, nc, wup, aup, w0, a0, k_k, k_a, r_k, gn_w, gn_b):
    b, s, n = h.shape
    nch = s // CHUNK // nc
    full = lambda arr: pl.BlockSpec(arr.shape, lambda i, j: (0,) * arr.ndim)
    params = (wup, aup, w0, a0, k_k, k_a, r_k, gn_w, gn_b)
    return pl.pallas_call(
        _rw_kernel,
        grid=(b, nch),
        in_specs=[pl.BlockSpec((1, nc * CHUNK, n), lambda i, j: (i, j, 0))] + [full(a) for a in params],
        out_specs=pl.BlockSpec((nc, RW_W // LANES, CHUNK, LANES), lambda i, j: (i * nch + j, 0, 0, 0)),
        out_shape=jax.ShapeDtypeStruct((b * nch * nc, RW_W // LANES, CHUNK, LANES), F32),
        scratch_shapes=[pltpu.VMEM((RW_W // LANES, LANES, LANES), F32)],
        compiler_params=pltpu.CompilerParams(dimension_semantics=("parallel", "arbitrary"),
                                             vmem_limit_bytes=VMEM_LIMIT),
        name="rw_mixer",
    )(h, *params)


def _ml_kernel(h_ref, convw_ref, ib_ref, fb_ref, gnw_ref, y_ref, hist, cstate, nstate, mstate):
    c = CHUNK
    nc = h_ref.shape[0]

    @pl.when(pl.program_id(1) == 0)
    def _():
        hist[...] = jnp.zeros(hist.shape, F32)
        cstate[...] = jnp.zeros(cstate.shape, F32)
        nstate[...] = jnp.zeros(nstate.shape, F32)
        mstate[...] = jnp.full(mstate.shape, -jnp.inf, F32)

    qk_tiles = 2 * ML_QK // LANES
    v_tiles = ML_V // LANES
    _, row, col = _chunk_iota(c, permuted=True)
    causal = row >= col
    causal_f = causal.astype(F32)
    heads = range(ML_HEADS)
    units = [(n, h) for n in range(nc) for h in heads]
    ids = range(len(units))

    qk_all, i_t, i_tt, bcum, bcum_t = [], [], [], [], []
    tail = hist[...]
    for n in range(nc):
        y_n, tail = _causal_conv_silu(tail, _col_tiles(h_ref, n, 0, qk_tiles), convw_ref)
        qk_all.append(y_n)
        gates = h_ref[n, qk_tiles + 3 * v_tiles]
        i_t.append(gates + ib_ref[...])
        i_tt.append(i_t[n].T)
        bcum.append(_dot_f32(causal_f, jax.nn.log_sigmoid(gates + fb_ref[...])))
        bcum_t.append(bcum[n].T)
    hist[...] = tail

    q = [qk_all[n][:, h * ML_DQK:(h + 1) * ML_DQK] for n, h in units]
    k = [qk_all[n][:, ML_QK + h * ML_DQK:ML_QK + (h + 1) * ML_DQK] * ML_DQK ** -0.5 for n, h in units]
    v = [h_ref[n, qk_tiles + h] for n, h in units]
    b_col = [bcum[n][:, ML_HEADS + h:ML_HEADS + h + 1] for n, h in units]
    b_row = [bcum_t[n][ML_HEADS + h:ML_HEADS + h + 1, :] for n, h in units]
    i_col = [i_t[n][:, h:h + 1] for n, h in units]
    i_row = [i_tt[n][h:h + 1, :] for n, h in units]
    b_last = [x[c - 1:c, :] for x in b_col]
    d_log = [jnp.where(causal, b_col[i] - b_row[i] + i_row[i], -jnp.inf) for i in ids]
    m_intra = [jnp.max(x, -1, keepdims=True) for x in d_log]
    qk = [_dot_nt(q[i], k[i]) for i in ids]

    m_prev, m_t = [], []
    m_cur = [mstate[h][0:1, 0:1] for h in heads]
    for n in range(nc):
        for h in heads:
            i = n * ML_HEADS + h
            m_prev.append(m_cur[h])
            m_t.append(jnp.maximum(m_cur[h] + b_col[i], m_intra[i]))
            m_cur[h] = m_t[i][c - 1:c, :]
    m_new = [x[c - 1:c, :] for x in m_t]
    inter = [jnp.exp(m_prev[i] + b_col[i] - m_t[i]) for i in ids]
    w_qk = [jnp.exp(d_log[i] - m_t[i]) * qk[i] for i in ids]
    carry = [jnp.exp(m_prev[i] + b_last[i] - m_new[i]) for i in ids]
    k_w = [k[i] * jnp.exp(b_last[i] - b_col[i] + i_col[i] - m_new[i]) for i in ids]
    kv = [_dot_tn(k_w[i], v[i]) for i in ids]
    k_sum = [jnp.sum(k_w[i], 0, keepdims=True) for i in ids]

    c_in, n_in = [], []
    c_cur = [cstate[h] for h in heads]
    n_cur = [nstate[h] for h in heads]
    for n in range(nc):
        for h in heads:
            i = n * ML_HEADS + h
            c_in.append(c_cur[h])
            n_in.append(n_cur[h])
            c_cur[h] = c_cur[h] * carry[i] + kv[i]
            n_cur[h] = n_cur[h] * carry[i] + k_sum[i]
    for h in heads:
        cstate[h] = c_cur[h]
        nstate[h] = n_cur[h]
        mstate[h] = jnp.broadcast_to(m_cur[h], mstate.shape[1:])

    num = [inter[i] * _dot(q[i], c_in[i]) + _dot(w_qk[i], v[i]) for i in ids]
    den = [inter[i] * jnp.sum(q[i] * n_in[i], -1, keepdims=True) + jnp.sum(w_qk[i], -1, keepdims=True)
           for i in ids]
    h_tilde = [num[i] / jnp.maximum(jnp.abs(den[i]), jnp.exp(-m_t[i])) for i in ids]
    for i, (n, h) in enumerate(units):
        og = h_ref[n, qk_tiles + v_tiles + h]
        z = h_ref[n, qk_tiles + 2 * v_tiles + h]
        xg = jax.nn.sigmoid(og) * h_tilde[i]
        mu = jnp.mean(xg, -1, keepdims=True)
        var = jnp.mean(jnp.square(xg - mu), -1, keepdims=True)
        xn = (xg - mu) * lax.rsqrt(var + 1e-6) * gnw_ref[:, h * ML_DV:(h + 1) * ML_DV]
        _store_natural_order(y_ref, n, h, xn * _silu(z))


def _ml_mixer(h, b, nc, conv_w, ib_row, fb_row, gn_w):
    nch = h.shape[0] // b // nc
    return pl.pallas_call(
        _ml_kernel,
        grid=(b, nch),
        in_specs=[pl.BlockSpec((nc,) + h.shape[1:], lambda i, j: (i * nch + j, 0, 0, 0)),
                  pl.BlockSpec(conv_w.shape, lambda i, j: (0, 0)),
                  pl.BlockSpec(ib_row.shape, lambda i, j: (0, 0)),
                  pl.BlockSpec(fb_row.shape, lambda i, j: (0, 0)),
                  pl.BlockSpec(gn_w.shape, lambda i, j: (0, 0))],
        out_specs=pl.BlockSpec((nc, ML_V // LANES, CHUNK, LANES), lambda i, j: (i * nch + j, 0, 0, 0)),
        out_shape=jax.ShapeDtypeStruct((h.shape[0], ML_V // LANES, CHUNK, LANES), F32),
        scratch_shapes=[pltpu.VMEM(((CONV_K - 1) * SUBLANES, 2 * ML_QK), F32),
                        pltpu.VMEM((ML_HEADS, ML_DQK, ML_DV), F32),
                        pltpu.VMEM((ML_HEADS, 1, ML_DQK), F32),
                        pltpu.VMEM((ML_HEADS, SUBLANES, LANES), F32)],
        compiler_params=pltpu.CompilerParams(dimension_semantics=("parallel", "arbitrary"),
                                             vmem_limit_bytes=VMEM_LIMIT),
        name="ml_mixer",
    )(h, conv_w, ib_row, fb_row, gn_w)


def _post_kernel(x_ref, y_ref, p_ref, wout_ref, wg_ref, wp_ref, lng_ref, lnb_ref, pnw_ref, o_ref):
    y = jnp.concatenate([jnp.concatenate([y_ref[ch, t] for t in range(y_ref.shape[1])], axis=1)
                         for ch in range(y_ref.shape[0])], axis=0)
    r = DEEPNORM_ALPHA * x_ref[...] + _dot(y, wout_ref[...])
    mu = jnp.mean(r, -1, keepdims=True)
    var = jnp.mean(jnp.square(r - mu), -1, keepdims=True)
    xn = (r - mu) * lax.rsqrt(var + LN_EPS) * lng_ref[...] + lnb_ref[...]
    gate = jax.nn.sigmoid(_dot(xn, wg_ref[...]))
    pp = _dot(p_ref[...], wp_ref[...])
    pn = pp * lax.rsqrt(jnp.mean(pp * pp, -1, keepdims=True) + 1e-6) * pnw_ref[...]
    o_ref[...] = xn + gate * pn


def _post(x2d, y4d, p2d, w_out, w_gate, w_proj, ln_g, ln_b, pn_w, tm):
    m, d = x2d.shape
    y_spec = pl.BlockSpec((tm // CHUNK,) + y4d.shape[1:], lambda i: (i, 0, 0, 0))
    tile = lambda arr: pl.BlockSpec((tm, arr.shape[1]), lambda i: (i, 0))
    full = lambda arr: pl.BlockSpec(arr.shape, lambda i: (0, 0))
    params = (w_out, w_gate, w_proj, ln_g, ln_b, pn_w)
    return pl.pallas_call(
        _post_kernel,
        grid=(m // tm,),
        in_specs=[tile(x2d), y_spec, tile(p2d)] + [full(a) for a in params],
        out_specs=pl.BlockSpec((tm, d), lambda i: (i, 0)),
        out_shape=jax.ShapeDtypeStruct((m, d), F32),
        compiler_params=pltpu.CompilerParams(dimension_semantics=("parallel",),
                                             vmem_limit_bytes=VMEM_LIMIT),
        name="post_block",
    )(x2d, y4d, p2d, *params)


def _pad_cols(w, n):
    return jnp.pad(w, ((0, 0), (0, n - w.shape[1])))


def _lane_row(vec, offset):
    return jnp.zeros((1, LANES), F32).at[0, offset:offset + vec.shape[0]].set(vec.astype(F32))


def _row_tile(m):
    for t in (512, 256, 128, 64):
        if m % t == 0:
            return t
    raise ValueError(f"token count {m} must be a multiple of {CHUNK}")


def kernel(x, p, ln_g, ln_b, ple_w_proj, ple_norm_w, ple_w_gate, dn_w_in, dn_conv_w, dn_a_log, dn_dt_bias, dn_norm_w, dn_w_out, rw_w_in, rw_mu, rw_w0, rw_w_lora_up, rw_a0, rw_a_lora_up, rw_k_k, rw_k_a, rw_r_k, rw_gn_w, rw_gn_b, rw_w_out, ml_w_in, ml_conv_w, ml_i_bias, ml_f_bias, ml_gn_w, ml_w_out):
    b, s, d = x.shape
    assert d == D_MODEL and s % CHUNK == 0
    m = b * s
    tm_proj = _row_tile(m) // 2 if _row_tile(m) > CHUNK else CHUNK
    tm_post = _row_tile(m)
    tm_rw = _row_tile(s) // 2 if _row_tile(s) > CHUNK else CHUNK
    chunks_per_step = lambda want: next(n for n in (want, 2, 1) if (s // CHUNK) % n == 0)
    bf = lambda w: w.astype(MXU_DTYPE)
    row = lambda v: v.reshape(1, -1).astype(F32)
    x2d = x.reshape(m, d)
    for i in range(DEPTH):
        kind, j = i % 3, i // 3
        if kind == 0:
            n_pad = 2 * DN_QK + 2 * DN_V + LANES
            h = _project(x2d, bf(_pad_cols(dn_w_in[j], n_pad)), tm_proj)
            y = _dn_mixer(h, b, chunks_per_step(DN_CHUNKS), dn_conv_w[j], _lane_row(dn_a_log[j], DN_HEADS),
                          _lane_row(dn_dt_bias[j], DN_HEADS), row(dn_norm_w[j]))
            w_out = dn_w_out[j]
        elif kind == 1:
            r_w, wl_w, k_w, v_w, al_w, z_w = jnp.split(
                rw_w_in[j], np.cumsum([RW_W, RW_LORA, RW_W, RW_W, RW_LORA]).tolist(), axis=1)
            w_cat = jnp.concatenate([r_w, k_w, v_w, z_w, wl_w, al_w], axis=1)
            mu = rw_mu[j]
            mu_cat = jnp.stack([mu[0], mu[2], mu[3], mu[5], mu[1], mu[4], mu[0], mu[0]], axis=0)
            h = _rw_project(x2d.reshape(b, s, d), bf(w_cat), mu_cat, tm_rw)
            y = _rw_mixer(h, chunks_per_step(RW_CHUNKS), bf(rw_w_lora_up[j]), bf(rw_a_lora_up[j]), row(rw_w0[j]),
                          row(rw_a0[j]), row(rw_k_k[j]), row(rw_k_a[j]), row(rw_r_k[j]), row(rw_gn_w[j]),
                          row(rw_gn_b[j]))
            w_out = rw_w_out[j]
        else:
            n_pad = 2 * ML_QK + 3 * ML_V + LANES
            h = _project(x2d, bf(_pad_cols(ml_w_in[j], n_pad)), tm_proj)
            y = _ml_mixer(h, b, chunks_per_step(ML_CHUNKS), ml_conv_w[j], _lane_row(ml_i_bias[j], 0),
                          _lane_row(ml_f_bias[j], ML_HEADS), row(ml_gn_w[j]))
            w_out = ml_w_out[j]
        x2d = _post(x2d, y, p[i].reshape(m, D_PLE), bf(w_out), bf(ple_w_gate[i]),
                    bf(ple_w_proj[i]), row(ln_g[i]), row(ln_b[i]), row(ple_norm_w[i]), tm_post)
    return x2d.reshape(b, s, d)
```

```python
import functools

import jax
import jax.numpy as jnp
import numpy as np
from jax import lax
from jax.experimental import pallas as pl
from jax.experimental.pallas import tpu as pltpu

F32 = jnp.float32
MXU_DTYPE = jnp.bfloat16

LANES = 128
SUBLANES = 8
VMEM_LIMIT = 48 * 1024 * 1024

DEPTH = 4
D_MODEL = 1024
D_PLE = 256
CONV_K = 4
CHUNK = 64
LN_EPS = 1e-5
DN_HEADS, DN_DK, DN_DV = 8, 128, 128
DN_QK = DN_HEADS * DN_DK
DN_V = DN_HEADS * DN_DV
RW_HEAD = 64
RW_HEADS = D_MODEL // RW_HEAD
RW_W = RW_HEADS * RW_HEAD
RW_LORA = 64
RW_GN_EPS = 64e-5
ML_HEADS, ML_DQK, ML_DV = 8, 64, 128
ML_QK = ML_HEADS * ML_DQK
ML_V = ML_HEADS * ML_DV
DEEPNORM_ALPHA = (2.0 * DEPTH) ** 0.25
HIST = SUBLANES
DN_CHUNKS = 4
RW_CHUNKS = 4
ML_CHUNKS = 1


def _dot(a, b):
    return lax.dot_general(a.astype(MXU_DTYPE), b.astype(MXU_DTYPE), (((1,), (0,)), ((), ())),
                           preferred_element_type=F32)


def _dot_nt(a, b):
    return lax.dot_general(a.astype(MXU_DTYPE), b.astype(MXU_DTYPE), (((1,), (1,)), ((), ())),
                           preferred_element_type=F32)


def _dot_tn(a, b):
    return lax.dot_general(a.astype(MXU_DTYPE), b.astype(MXU_DTYPE), (((0,), (0,)), ((), ())),
                           preferred_element_type=F32)


def _dot_f32(a, b):
    return lax.dot_general(a, b, (((1,), (0,)), ((), ())), precision=lax.Precision.HIGHEST,
                           preferred_element_type=F32)


def _silu(x):
    return x * jax.nn.sigmoid(x)


def _perm_time(p):
    return ((p & (SUBLANES - 1)) << 3) | (p >> 3)


def _chunk_iota(n, permuted=False):
    row = lax.broadcasted_iota(jnp.int32, (n, n), 0)
    col = lax.broadcasted_iota(jnp.int32, (n, n), 1)
    eye = (row == col).astype(F32)
    if permuted:
        row, col = _perm_time(row), _perm_time(col)
    return eye, row, col


def _doubling_masks(row, col, n):
    masks = []
    s, shift = 1, 0
    while s < n:
        same = (row >> (shift + 1)) == (col >> (shift + 1))
        masks.append(same & ((row & s) != 0) & ((col & s) == 0))
        s, shift = 2 * s, shift + 1
    return masks


def _unit_lower_inverse_many(a_list, eye, masks):
    t = [eye - jnp.where(masks[0], a, 0.0) for a in a_list]
    for m in masks[1:]:
        ta = [_dot(ti, jnp.where(m, a, 0.0)) for ti, a in zip(t, a_list)]
        t = [ti - _dot(tai, ti) for ti, tai in zip(t, ta)]
    return t


HALF = LANES // 2


def _pair_lane_masks(rows):
    lane = lax.broadcasted_iota(jnp.int32, (rows, LANES), 1)
    return lane < HALF, lane & (HALF - 1)


def _row_block_diag(x, in_a):
    return jnp.concatenate([jnp.where(in_a, x, 0.0), jnp.where(in_a, 0.0, x)], axis=0)


def _unit_lower_inverse_pairs(a_list, eye2, masks2, in_a):
    t = [eye2 - jnp.where(masks2[0], a, 0.0) for a in a_list]
    for m in masks2[1:]:
        ta = [_dot(ti, _row_block_diag(jnp.where(m, a, 0.0), in_a)) for ti, a in zip(t, a_list)]
        t = [ti - _dot(tai, _row_block_diag(ti, in_a)) for ti, tai in zip(t, ta)]
    return t


def _col_tiles(ref, chunk, first, count):
    return jnp.concatenate([ref[chunk, first + i] for i in range(count)], axis=1)


def _causal_conv_silu(prev_tail, x, w_ref):
    keep = CONV_K - 1
    tail = x[CHUNK - keep * SUBLANES:, :]
    sub = lax.broadcasted_iota(jnp.int32, tail.shape, 0) & (SUBLANES - 1)
    mixed = jnp.where(sub == SUBLANES - 1, prev_tail, tail)
    wrapped = jnp.concatenate([pltpu.roll(mixed[i * SUBLANES:(i + 1) * SUBLANES, :], 1, 0) for i in range(keep)],
                              axis=0)
    acc = w_ref[CONV_K - 1:CONV_K, :] * x
    for s in range(1, CONV_K):
        shifted = jnp.concatenate([wrapped[(keep - s) * SUBLANES:, :], x[:CHUNK - s * SUBLANES, :]], axis=0)
        acc = acc + w_ref[CONV_K - 1 - s:CONV_K - s, :] * shifted
    return _silu(acc), tail


def _store_natural_order(y_ref, chunk, tile, y):
    for b in range(SUBLANES):
        y_ref[chunk, tile, pl.ds(b, CHUNK // SUBLANES, stride=SUBLANES), :] = y[b * SUBLANES:(b + 1) * SUBLANES, :]


PROJ_COLS = 2 * LANES
PROJ_ROWS = 512
POST_ROWS = 1024


def _proj_kernel(x_ref, w_ref, o_ref, *, tm):
    x = x_ref[...].astype(MXU_DTYPE)
    n = w_ref.shape[1]
    for c0 in range(0, n, PROJ_COLS):
        width = min(PROJ_COLS, n - c0)
        res = lax.dot_general(x, w_ref[:, c0:c0 + width], (((1,), (0,)), ((), ())), preferred_element_type=F32)
        for ch in range(tm // CHUNK):
            for a in range(CHUNK // SUBLANES):
                r0 = ch * CHUNK + a * SUBLANES
                for t in range(width // LANES):
                    o_ref[ch, c0 // LANES + t, pl.ds(a, SUBLANES, stride=SUBLANES), :] = (
                        res[r0:r0 + SUBLANES, t * LANES:(t + 1) * LANES])


def _project(x2d, w, tm):
    m, k = x2d.shape
    n = w.shape[1]
    return pl.pallas_call(
        functools.partial(_proj_kernel, tm=tm),
        grid=(m // tm,),
        in_specs=[pl.BlockSpec((tm, k), lambda i: (i, 0)),
                  pl.BlockSpec((k, n), lambda i: (0, 0))],
        out_specs=pl.BlockSpec((tm // CHUNK, n // LANES, CHUNK, LANES), lambda i: (i, 0, 0, 0)),
        out_shape=jax.ShapeDtypeStruct((m // CHUNK, n // LANES, CHUNK, LANES), F32),
        compiler_params=pltpu.CompilerParams(dimension_semantics=("parallel",),
                                             vmem_limit_bytes=VMEM_LIMIT),
        name="in_proj",
    )(x2d, w)


def _rw_proj_kernel(x_ref, prev_ref, w_ref, mu_ref, o_ref, buf, *, tm):
    x = x_ref[0]
    buf[HIST:HIST + tm, :] = x
    buf[0:HIST, :] = jnp.where(pl.program_id(1) == 0, 0.0, prev_ref[0])
    dx = buf[pl.ds(HIST - 1, tm), :] - x
    for g in range(4):
        lhs = x + mu_ref[g:g + 1, :] * dx
        o_ref[0, :, g * RW_W:(g + 1) * RW_W] = _dot(lhs, w_ref[:, g * RW_W:(g + 1) * RW_W])
    w_lo = w_ref[:, 4 * RW_W:4 * RW_W + LANES]
    lo_w = _dot(x + mu_ref[4:5, :] * dx, w_lo)
    lo_a = _dot(x + mu_ref[5:6, :] * dx, w_lo)
    lane = lax.broadcasted_iota(jnp.int32, lo_w.shape, 1)
    o_ref[0, :, 4 * RW_W:4 * RW_W + LANES] = jnp.where(lane < RW_LORA, lo_w, lo_a)


def _rw_project(x, w, mu, tm):
    b, s, d = x.shape
    n = w.shape[1]
    per = tm // HIST
    return pl.pallas_call(
        functools.partial(_rw_proj_kernel, tm=tm),
        grid=(b, s // tm),
        in_specs=[pl.BlockSpec((1, tm, d), lambda i, j: (i, j, 0)),
                  pl.BlockSpec((1, HIST, d), lambda i, j: (i, jnp.maximum(j * per - 1, 0), 0)),
                  pl.BlockSpec((d, n), lambda i, j: (0, 0)),
                  pl.BlockSpec(mu.shape, lambda i, j: (0, 0))],
        out_specs=pl.BlockSpec((1, tm, n), lambda i, j: (i, j, 0)),
        out_shape=jax.ShapeDtypeStruct((b, s, n), F32),
        scratch_shapes=[pltpu.VMEM((tm + HIST, d), F32)],
        compiler_params=pltpu.CompilerParams(dimension_semantics=("parallel", "parallel"),
                                             vmem_limit_bytes=VMEM_LIMIT),
        name="rw_in_proj",
    )(x, x, w, mu)


def _dn_kernel(h_ref, convw_ref, alog_ref, dtb_ref, normw_ref, y_ref, hist, state):
    c = CHUNK
    nc = h_ref.shape[0]

    @pl.when(pl.program_id(1) == 0)
    def _():
        hist[...] = jnp.zeros(hist.shape, F32)
        state[...] = jnp.zeros(state.shape, F32)

    qkv_tiles = (2 * DN_QK + DN_V) // LANES
    z_tile0 = qkv_tiles
    _, row, col = _chunk_iota(c, permuted=True)
    causal_f = (row >= col).astype(F32)
    in_a, col2 = _pair_lane_masks(c)
    row2 = lax.broadcasted_iota(jnp.int32, (c, LANES), 0)
    eye2 = (row2 == col2).astype(F32)
    row2, col2 = _perm_time(row2), _perm_time(col2)
    causal2 = row2 >= col2
    strict2 = row2 > col2
    masks2 = _doubling_masks(row2, col2, c)

    heads = range(DN_HEADS)
    units = [(n, h) for n in range(nc) for h in heads]
    ids = range(len(units))
    qkv, gcum, gcum_t, beta_t = [], [], [], []
    tail = hist[...]
    for n in range(nc):
        y_n, tail = _causal_conv_silu(tail, _col_tiles(h_ref, n, 0, qkv_tiles), convw_ref)
        qkv.append(y_n)
        gates = h_ref[n, z_tile0 + DN_V // LANES]
        beta_t.append(jax.nn.sigmoid(gates))
        g_t = -(jnp.exp(alog_ref[...]) * jax.nn.softplus(gates + dtb_ref[...]))
        gcum.append(_dot_f32(causal_f, g_t))
        gcum_t.append(gcum[n].T)
    hist[...] = tail

    q = [qkv[n][:, h * DN_DK:(h + 1) * DN_DK] for n, h in units]
    k = [qkv[n][:, DN_QK + h * DN_DK:DN_QK + (h + 1) * DN_DK] for n, h in units]
    v = [qkv[n][:, 2 * DN_QK + h * DN_DV:2 * DN_QK + (h + 1) * DN_DV] for n, h in units]
    q = [x * (lax.rsqrt(jnp.sum(x * x, -1, keepdims=True) + 1e-6) * DN_DK ** -0.5) for x in q]
    k = [x * lax.rsqrt(jnp.sum(x * x, -1, keepdims=True) + 1e-6) for x in k]
    beta = [beta_t[n][:, h:h + 1] for n, h in units]
    g_col = [gcum[n][:, DN_HEADS + h:DN_HEADS + h + 1] for n, h in units]
    g_row = [gcum_t[n][DN_HEADS + h:DN_HEADS + h + 1, :] for n, h in units]
    g_last = [g[c - 1:c, :] for g in g_col]
    exp_g = [jnp.exp(g) for g in g_col]
    k_beta = [k[i] * beta[i] for i in ids]
    zeros = jnp.zeros((c, DN_DK), F32)
    m2, decay2 = [], []
    for i0 in range(0, len(units), 2):
        i1 = i0 + 1
        lhs = jnp.concatenate([jnp.concatenate([k_beta[i0], k_beta[i1]], axis=1),
                               jnp.concatenate([q[i0], q[i1]], axis=1)], axis=0)
        rhs = jnp.concatenate([jnp.concatenate([k[i0], zeros], axis=1),
                               jnp.concatenate([zeros, k[i1]], axis=1)], axis=0)
        m2.append(_dot_nt(lhs, rhs))
        g_col2 = jnp.where(in_a, g_col[i0], g_col[i1])
        g_row2 = jnp.concatenate([g_row[i0], g_row[i1]], axis=1)
        decay2.append(jnp.exp(jnp.where(causal2, g_col2 - g_row2, -jnp.inf)))
    a2 = [jnp.where(strict2, m[:c] * d, 0.0) for m, d in zip(m2, decay2)]
    qk2 = [m[c:] * d for m, d in zip(m2, decay2)]
    t2 = _unit_lower_inverse_pairs(a2, eye2, masks2, in_a)
    t = [t2[i // 2][:, (i % 2) * c:(i % 2 + 1) * c] for i in ids]
    qk = [qk2[i // 2][:, (i % 2) * c:(i % 2 + 1) * c] for i in ids]
    uw = [_dot(t[i], jnp.concatenate([v[i] * beta[i], k_beta[i] * exp_g[i]], axis=1)) for i in ids]
    q_dec = [q[i] * exp_g[i] for i in ids]
    k_dec = [k[i] * jnp.exp(g_last[i] - g_col[i]) for i in ids]
    s_cur = [state[h] for h in heads]
    o = []
    for n in range(nc):
        idn = [n * DN_HEADS + h for h in heads]
        ws = [_dot(jnp.concatenate([uw[i][:, DN_DV:], q_dec[i]], axis=0), s_cur[h]) for h, i in zip(heads, idn)]
        v_new = [uw[i][:, :DN_DV] - ws[h][:c] for h, i in zip(heads, idn)]
        o += [ws[h][c:] + _dot(qk[i], v_new[h]) for h, i in zip(heads, idn)]
        s_cur = [s_cur[h] * jnp.exp(g_last[i]) + _dot_tn(k_dec[i], v_new[h]) for h, i in zip(heads, idn)]
    for h in heads:
        state[h] = s_cur[h]
    for i, (n, h) in enumerate(units):
        z = h_ref[n, z_tile0 + h]
        on = o[i] * lax.rsqrt(jnp.mean(o[i] * o[i], -1, keepdims=True) + 1e-6) * normw_ref[...]
        _store_natural_order(y_ref, n, h, on * _silu(z))


def _dn_mixer(h, b, nc, conv_w, alog_row, dtb_row, norm_w):
    nch = h.shape[0] // b // nc
    ncols = 2 * DN_QK + DN_V
    return pl.pallas_call(
        _dn_kernel,
        grid=(b, nch),
        in_specs=[pl.BlockSpec((nc,) + h.shape[1:], lambda i, j: (i * nch + j, 0, 0, 0)),
                  pl.BlockSpec(conv_w.shape, lambda i, j: (0, 0)),
                  pl.BlockSpec(alog_row.shape, lambda i, j: (0, 0)),
                  pl.BlockSpec(dtb_row.shape, lambda i, j: (0, 0)),
                  pl.BlockSpec(norm_w.shape, lambda i, j: (0, 0))],
        out_specs=pl.BlockSpec((nc, DN_V // LANES, CHUNK, LANES), lambda i, j: (i * nch + j, 0, 0, 0)),
        out_shape=jax.ShapeDtypeStruct((h.shape[0], DN_V // LANES, CHUNK, LANES), F32),
        scratch_shapes=[pltpu.VMEM(((CONV_K - 1) * SUBLANES, ncols), F32),
                        pltpu.VMEM((DN_HEADS, DN_DK, DN_DV), F32)],
        compiler_params=pltpu.CompilerParams(dimension_semantics=("parallel", "arbitrary"),
                                             vmem_limit_bytes=VMEM_LIMIT),
        name="dn_mixer",
    )(h, conv_w, alog_row, dtb_row, norm_w)


def _rw_kernel(h_ref, wup_ref, aup_ref, w0_ref, a0_ref, kk_ref, ka_ref, rk_ref, gnw_ref, gnb_ref,
               y_ref, state):
    c = CHUNK
    nc = y_ref.shape[0]
    pairs = RW_W // LANES

    @pl.when(pl.program_id(1) == 0)
    def _():
        state[...] = jnp.zeros(state.shape, F32)

    _, row, col = _chunk_iota(c)
    in_a, col2 = _pair_lane_masks(c)
    row2 = lax.broadcasted_iota(jnp.int32, (c, LANES), 0)
    causal2 = row2 >= col2
    strict2 = row2 > col2
    eye2 = (row2 == col2).astype(F32)
    masks2 = _doubling_masks(row2, col2, c)
    causal_f = (row >= col).astype(F32)
    row_big = lax.broadcasted_iota(jnp.int32, (LANES, LANES), 0)
    col_big = lax.broadcasted_iota(jnp.int32, (LANES, LANES), 1)
    same_head = (row_big < RW_HEAD) == (col_big < RW_HEAD)

    row_bd = functools.partial(_row_block_diag, in_a=in_a)

    def half_sums(x):
        sum_a = jnp.sum(jnp.where(in_a, x, 0.0), -1, keepdims=True)
        sum_b = jnp.sum(jnp.where(in_a, 0.0, x), -1, keepdims=True)
        return jnp.where(in_a, sum_a, sum_b)

    units = [(n, p) for n in range(nc) for p in range(pairs)]
    ids = range(len(units))
    tiles = [slice(p * LANES, (p + 1) * LANES) for p in range(pairs)]

    r_all, v_all, z_all, a_all, k2_all, rkr_all, kku_all = ([] for _ in range(7))
    p_incl, p_excl, p_inv, p_tail, p_last = ([] for _ in range(5))
    for n in range(nc):
        rows = slice(n * c, (n + 1) * c)
        r_all.append(h_ref[0, rows, 0:RW_W])
        k_n = h_ref[0, rows, RW_W:2 * RW_W]
        v_all.append(h_ref[0, rows, 2 * RW_W:3 * RW_W])
        z_all.append(h_ref[0, rows, 3 * RW_W:4 * RW_W])
        lo = h_ref[0, rows, 4 * RW_W:4 * RW_W + LANES]
        w_log = -jax.nn.softplus(-(w0_ref[...] + _dot(jnp.tanh(lo[:, :RW_LORA]), wup_ref[...]))) - 0.5
        a_all.append(jax.nn.sigmoid(a0_ref[...] + _dot(lo[:, RW_LORA:], aup_ref[...])))
        logd = -jnp.exp(w_log)
        lcum = _dot_f32(causal_f, logd)
        l_last = lcum[c - 1:c, :]
        p_incl.append(jnp.exp(lcum))
        p_excl.append(jnp.exp(lcum - logd))
        p_inv.append(jnp.exp(-lcum))
        p_tail.append(jnp.exp(l_last - lcum))
        p_last.append(jnp.exp(l_last))
        kku_all.append(k_n * kk_ref[...])
        k2_all.append(k_n * (1.0 + (a_all[n] - 1.0) * ka_ref[...]))
        rkr_all.append(r_all[n] * k2_all[n] * rk_ref[...])

    v = [v_all[n][:, tiles[p]] for n, p in units]
    k2 = [k2_all[n][:, tiles[p]] for n, p in units]
    kku = [kku_all[n][:, tiles[p]] for n, p in units]
    kk = [kku[i] * lax.rsqrt(half_sums(kku[i] * kku[i]) + 1e-6) for i in ids]
    alpha = [-(kk[i] * a_all[n][:, tiles[p]]) for i, (n, p) in enumerate(units)]
    lhs = [jnp.concatenate([kk[i] * p_excl[n][:, tiles[p]], r_all[n][:, tiles[p]] * p_incl[n][:, tiles[p]]], axis=0)
           for i, (n, p) in enumerate(units)]
    rhs = [jnp.concatenate([row_bd(alpha[i] * p_inv[n][:, tiles[p]]), row_bd(k2[i] * p_inv[n][:, tiles[p]])], axis=0)
           for i, (n, p) in enumerate(units)]
    m = [_dot_nt(lhs[i], rhs[i]) for i in ids]
    a_ab = [jnp.where(strict2, -x[:c, :LANES], 0.0) for x in m]
    a_ak = [jnp.where(strict2, x[:c, LANES:], 0.0) for x in m]
    a_r = [jnp.concatenate([jnp.where(causal2, x[c:, :LANES], 0.0), jnp.where(causal2, x[c:, LANES:], 0.0)], axis=1)
           for x in m]
    t = _unit_lower_inverse_pairs(a_ab, eye2, masks2, in_a)
    v_bd = [row_bd(x) for x in v]
    akv = [_dot(a_ak[i], v_bd[i]) for i in ids]
    tail = [jnp.concatenate([alpha[i] * p_tail[n][:, tiles[p]], k2[i] * p_tail[n][:, tiles[p]]], axis=0)
            for i, (n, p) in enumerate(units)]

    s_cur = [state[p] for p in range(pairs)]
    y = []
    for n in range(nc):
        idn = [n * pairs + p for p in range(pairs)]
        lhs_s = [_dot_nt(lhs[i], s_cur[p]) for p, i in enumerate(idn)]
        u = [_dot(t[i], row_bd(lhs_s[p][:c] + akv[i])) for p, i in enumerate(idn)]
        y += [lhs_s[p][c:] + _dot(a_r[i], jnp.concatenate([row_bd(u[p]), v_bd[i]], axis=0))
              for p, i in enumerate(idn)]
        s_cur = [s_cur[p] * p_last[n][:, tiles[p]]
                 + jnp.where(same_head, _dot_tn(jnp.concatenate([u[p], v[i]], axis=0), tail[i]), 0.0)
                 for p, i in enumerate(idn)]
    for p in range(pairs):
        state[p] = s_cur[p]
    inv_d = 1.0 / RW_HEAD
    for i, (n, p) in enumerate(units):
        mu = half_sums(y[i]) * inv_d
        yc = y[i] - mu
        var = half_sums(yc * yc) * inv_d
        yn = yc * lax.rsqrt(var + RW_GN_EPS) * gnw_ref[:, tiles[p]] + gnb_ref[:, tiles[p]]
        bonus = half_sums(rkr_all[n][:, tiles[p]]) * v[i]
        y_ref[n, p] = (yn + bonus) * _silu(z_all[n][:, tiles[p]])


def _rw_mixer(h, nc, wup, aup, w0, a0, k_k, k_a, r_k, gn_w, gn_b):
    b, s, n = h.shape
    nch = s // CHUNK // nc
    full = lambda arr: pl.BlockSpec(arr.shape, lambda i, j: (0,) * arr.ndim)
    params = (wup, aup, w0, a0, k_k, k_a, r_k, gn_w, gn_b)
    return pl.pallas_call(
        _rw_kernel,
        grid=(b, nch),
        in_specs=[pl.BlockSpec((1, nc * CHUNK, n), lambda i, j: (i, j, 0))] + [full(a) for a in params],
        out_specs=pl.BlockSpec((nc, RW_W // LANES, CHUNK, LANES), lambda i, j: (i * nch + j, 0, 0, 0)),
        out_shape=jax.ShapeDtypeStruct((b * nch * nc, RW_W // LANES, CHUNK, LANES), F32),
        scratch_shapes=[pltpu.VMEM((RW_W // LANES, LANES, LANES), F32)],
        compiler_params=pltpu.CompilerParams(dimension_semantics=("parallel", "arbitrary"),
                                             vmem_limit_bytes=VMEM_LIMIT),
        name="rw_mixer",
    )(h, *params)


def _ml_kernel(h_ref, convw_ref, ib_ref, fb_ref, gnw_ref, y_ref, hist, cstate, nstate, mstate):
    c = CHUNK
    nc = h_ref.shape[0]

    @pl.when(pl.program_id(1) == 0)
    def _():
        hist[...] = jnp.zeros(hist.shape, F32)
        cstate[...] = jnp.zeros(cstate.shape, F32)
        nstate[...] = jnp.zeros(nstate.shape, F32)
        mstate[...] = jnp.full(mstate.shape, -jnp.inf, F32)

    qk_tiles = 2 * ML_QK // LANES
    v_tiles = ML_V // LANES
    _, row, col = _chunk_iota(c, permuted=True)
    causal = row >= col
    causal_f = causal.astype(F32)
    heads = range(ML_HEADS)
    units = [(n, h) for n in range(nc) for h in heads]
    ids = range(len(units))

    qk_all, i_t, i_tt, bcum, bcum_t = [], [], [], [], []
    tail = hist[...]
    for n in range(nc):
        y_n, tail = _causal_conv_silu(tail, _col_tiles(h_ref, n, 0, qk_tiles), convw_ref)
        qk_all.append(y_n)
        gates = h_ref[n, qk_tiles + 3 * v_tiles]
        i_t.append(gates + ib_ref[...])
        i_tt.append(i_t[n].T)
        bcum.append(_dot_f32(causal_f, jax.nn.log_sigmoid(gates + fb_ref[...])))
        bcum_t.append(bcum[n].T)
    hist[...] = tail

    q = [qk_all[n][:, h * ML_DQK:(h + 1) * ML_DQK] for n, h in units]
    k = [qk_all[n][:, ML_QK + h * ML_DQK:ML_QK + (h + 1) * ML_DQK] * ML_DQK ** -0.5 for n, h in units]
    v = [h_ref[n, qk_tiles + h] for n, h in units]
    b_col = [bcum[n][:, ML_HEADS + h:ML_HEADS + h + 1] for n, h in units]
    b_row = [bcum_t[n][ML_HEADS + h:ML_HEADS + h + 1, :] for n, h in units]
    i_col = [i_t[n][:, h:h + 1] for n, h in units]
    i_row = [i_tt[n][h:h + 1, :] for n, h in units]
    b_last = [x[c - 1:c, :] for x in b_col]
    d_log = [jnp.where(causal, b_col[i] - b_row[i] + i_row[i], -jnp.inf) for i in ids]
    m_intra = [jnp.max(x, -1, keepdims=True) for x in d_log]
    qk = [_dot_nt(q[i], k[i]) for i in ids]

    m_prev, m_t = [], []
    m_cur = [mstate[h][0:1, 0:1] for h in heads]
    for n in range(nc):
        for h in heads:
            i = n * ML_HEADS + h
            m_prev.append(m_cur[h])
            m_t.append(jnp.maximum(m_cur[h] + b_col[i], m_intra[i]))
            m_cur[h] = m_t[i][c - 1:c, :]
    m_new = [x[c - 1:c, :] for x in m_t]
    inter = [jnp.exp(m_prev[i] + b_col[i] - m_t[i]) for i in ids]
    w_qk = [jnp.exp(d_log[i] - m_t[i]) * qk[i] for i in ids]
    carry = [jnp.exp(m_prev[i] + b_last[i] - m_new[i]) for i in ids]
    k_w = [k[i] * jnp.exp(b_last[i] - b_col[i] + i_col[i] - m_new[i]) for i in ids]
    kv = [_dot_tn(k_w[i], v[i]) for i in ids]
    k_sum = [jnp.sum(k_w[i], 0, keepdims=True) for i in ids]

    c_in, n_in = [], []
    c_cur = [cstate[h] for h in heads]
    n_cur = [nstate[h] for h in heads]
    for n in range(nc):
        for h in heads:
            i = n * ML_HEADS + h
            c_in.append(c_cur[h])
            n_in.append(n_cur[h])
            c_cur[h] = c_cur[h] * carry[i] + kv[i]
            n_cur[h] = n_cur[h] * carry[i] + k_sum[i]
    for h in heads:
        cstate[h] = c_cur[h]
        nstate[h] = n_cur[h]
        mstate[h] = jnp.broadcast_to(m_cur[h], mstate.shape[1:])

    num = [inter[i] * _dot(q[i], c_in[i]) + _dot(w_qk[i], v[i]) for i in ids]
    den = [inter[i] * jnp.sum(q[i] * n_in[i], -1, keepdims=True) + jnp.sum(w_qk[i], -1, keepdims=True)
           for i in ids]
    h_tilde = [num[i] / jnp.maximum(jnp.abs(den[i]), jnp.exp(-m_t[i])) for i in ids]
    for i, (n, h) in enumerate(units):
        og = h_ref[n, qk_tiles + v_tiles + h]
        z = h_ref[n, qk_tiles + 2 * v_tiles + h]
        xg = jax.nn.sigmoid(og) * h_tilde[i]
        mu = jnp.mean(xg, -1, keepdims=True)
        var = jnp.mean(jnp.square(xg - mu), -1, keepdims=True)
        xn = (xg - mu) * lax.rsqrt(var + 1e-6) * gnw_ref[:, h * ML_DV:(h + 1) * ML_DV]
        _store_natural_order(y_ref, n, h, xn * _silu(z))


def _ml_mixer(h, b, nc, conv_w, ib_row, fb_row, gn_w):
    nch = h.shape[0] // b // nc
    return pl.pallas_call(
        _ml_kernel,
        grid=(b, nch),
        in_specs=[pl.BlockSpec((nc,) + h.shape[1:], lambda i, j: (i * nch + j, 0, 0, 0)),
                  pl.BlockSpec(conv_w.shape, lambda i, j: (0, 0)),
                  pl.BlockSpec(ib_row.shape, lambda i, j: (0, 0)),
                  pl.BlockSpec(fb_row.shape, lambda i, j: (0, 0)),
                  pl.BlockSpec(gn_w.shape, lambda i, j: (0, 0))],
        out_specs=pl.BlockSpec((nc, ML_V // LANES, CHUNK, LANES), lambda i, j: (i * nch + j, 0, 0, 0)),
        out_shape=jax.ShapeDtypeStruct((h.shape[0], ML_V // LANES, CHUNK, LANES), F32),
        scratch_shapes=[pltpu.VMEM(((CONV_K - 1) * SUBLANES, 2 * ML_QK), F32),
                        pltpu.VMEM((ML_HEADS, ML_DQK, ML_DV), F32),
                        pltpu.VMEM((ML_HEADS, 1, ML_DQK), F32),
                        pltpu.VMEM((ML_HEADS, SUBLANES, LANES), F32)],
        compiler_params=pltpu.CompilerParams(dimension_semantics=("parallel", "arbitrary"),
                                             vmem_limit_bytes=VMEM_LIMIT),
        name="ml_mixer",
    )(h, conv_w, ib_row, fb_row, gn_w)


def _post_kernel(x_ref, y_ref, p_ref, wout_ref, wg_ref, wp_ref, lng_ref, lnb_ref, pnw_ref, o_ref):
    y = jnp.concatenate([jnp.concatenate([y_ref[ch, t] for t in range(y_ref.shape[1])], axis=1)
                         for ch in range(y_ref.shape[0])], axis=0)
    r = DEEPNORM_ALPHA * x_ref[...] + _dot(y, wout_ref[...])
    mu = jnp.mean(r, -1, keepdims=True)
    var = jnp.mean(jnp.square(r - mu), -1, keepdims=True)
    xn = (r - mu) * lax.rsqrt(var + LN_EPS) * lng_ref[...] + lnb_ref[...]
    gate = jax.nn.sigmoid(_dot(xn, wg_ref[...]))
    pp = _dot(p_ref[...], wp_ref[...])
    pn = pp * lax.rsqrt(jnp.mean(pp * pp, -1, keepdims=True) + 1e-6) * pnw_ref[...]
    o_ref[...] = xn + gate * pn


def _post(x2d, y4d, p2d, w_out, w_gate, w_proj, ln_g, ln_b, pn_w, tm):
    m, d = x2d.shape
    y_spec = pl.BlockSpec((tm // CHUNK,) + y4d.shape[1:], lambda i: (i, 0, 0, 0))
    tile = lambda arr: pl.BlockSpec((tm, arr.shape[1]), lambda i: (i, 0))
    full = lambda arr: pl.BlockSpec(arr.shape, lambda i: (0, 0))
    params = (w_out, w_gate, w_proj, ln_g, ln_b, pn_w)
    return pl.pallas_call(
        _post_kernel,
        grid=(m // tm,),
        in_specs=[tile(x2d), y_spec, tile(p2d)] + [full(a) for a in params],
        out_specs=pl.BlockSpec((tm, d), lambda i: (i, 0)),
        out_shape=jax.ShapeDtypeStruct((m, d), F32),
        compiler_params=pltpu.CompilerParams(dimension_semantics=("parallel",),
                                             vmem_limit_bytes=VMEM_LIMIT),
        name="post_block",
    )(x2d, y4d, p2d, *params)


def _pad_cols(w, n):
    return jnp.pad(w, ((0, 0), (0, n - w.shape[1])))


def _lane_row(vec, offset):
    return jnp.zeros((1, LANES), F32).at[0, offset:offset + vec.shape[0]].set(vec.astype(F32))


def _row_tile(m, cap):
    t = cap
    while t >= CHUNK:
        if m % t == 0:
            return t
        t //= 2
    raise ValueError(f"token count {m} must be a multiple of {CHUNK}")


def kernel(x, p, ln_g, ln_b, ple_w_proj, ple_norm_w, ple_w_gate, dn_w_in, dn_conv_w, dn_a_log, dn_dt_bias, dn_norm_w, dn_w_out, rw_w_in, rw_mu, rw_w0, rw_w_lora_up, rw_a0, rw_a_lora_up, rw_k_k, rw_k_a, rw_r_k, rw_gn_w, rw_gn_b, rw_w_out, ml_w_in, ml_conv_w, ml_i_bias, ml_f_bias, ml_gn_w, ml_w_out):
    b, s, d = x.shape
    assert d == D_MODEL and s % CHUNK == 0
    m = b * s
    tm_proj = _row_tile(m, PROJ_ROWS)
    tm_post = _row_tile(m, POST_ROWS)
    tm_rw = _row_tile(s, PROJ_ROWS)
    chunks_per_step = lambda want: next(n for n in (want, 2, 1) if (s // CHUNK) % n == 0)
    bf = lambda w: w.astype(MXU_DTYPE)
    row = lambda v: v.reshape(1, -1).astype(F32)
    x2d = x.reshape(m, d)
    for i in range(DEPTH):
        kind, j = i % 3, i // 3
        if kind == 0:
            n_pad = 2 * DN_QK + 2 * DN_V + LANES
            h = _project(x2d, bf(_pad_cols(dn_w_in[j], n_pad)), tm_proj)
            y = _dn_mixer(h, b, chunks_per_step(DN_CHUNKS), dn_conv_w[j], _lane_row(dn_a_log[j], DN_HEADS),
                          _lane_row(dn_dt_bias[j], DN_HEADS), row(dn_norm_w[j]))
            w_out = dn_w_out[j]
        elif kind == 1:
            r_w, wl_w, k_w, v_w, al_w, z_w = jnp.split(
                rw_w_in[j], np.cumsum([RW_W, RW_LORA, RW_W, RW_W, RW_LORA]).tolist(), axis=1)
            w_cat = jnp.concatenate([r_w, k_w, v_w, z_w, wl_w, al_w], axis=1)
            mu = rw_mu[j]
            mu_cat = jnp.stack([mu[0], mu[2], mu[3], mu[5], mu[1], mu[4], mu[0], mu[0]], axis=0)
            h = _rw_project(x2d.reshape(b, s, d), bf(w_cat), mu_cat, tm_rw)
            y = _rw_mixer(h, chunks_per_step(RW_CHUNKS), bf(rw_w_lora_up[j]), bf(rw_a_lora_up[j]), row(rw_w0[j]),
                          row(rw_a0[j]), row(rw_k_k[j]), row(rw_k_a[j]), row(rw_r_k[j]), row(rw_gn_w[j]),
                          row(rw_gn_b[j]))
            w_out = rw_w_out[j]
        else:
            n_pad = 2 * ML_QK + 3 * ML_V + LANES
            h = _project(x2d, bf(_pad_cols(ml_w_in[j], n_pad)), tm_proj)
            y = _ml_mixer(h, b, chunks_per_step(ML_CHUNKS), ml_conv_w[j], _lane_row(ml_i_bias[j], 0),
                          _lane_row(ml_f_bias[j], ML_HEADS), row(ml_gn_w[j]))
            w_out = ml_w_out[j]
        x2d = _post(x2d, y, p[i].reshape(m, D_PLE), bf(w_out), bf(ple_w_gate[i]),
                    bf(ple_w_proj[i]), row(ln_g[i]), row(ln_b[i]), row(ple_norm_w[i]), tm_post)
    return x2d.reshape(b, s, d)
```

```python
import functools

import jax
import jax.numpy as jnp
import numpy as np
from jax import lax
from jax.experimental import pallas as pl
from jax.experimental.pallas import tpu as pltpu

F32 = jnp.float32
MXU_DTYPE = jnp.bfloat16

LANES = 128
SUBLANES = 8
VMEM_LIMIT = 48 * 1024 * 1024

DEPTH = 4
D_MODEL = 1024
D_PLE = 256
CONV_K = 4
CHUNK = 64
LN_EPS = 1e-5
DN_HEADS, DN_DK, DN_DV = 8, 128, 128
DN_QK = DN_HEADS * DN_DK
DN_V = DN_HEADS * DN_DV
RW_HEAD = 64
RW_HEADS = D_MODEL // RW_HEAD
RW_W = RW_HEADS * RW_HEAD
RW_LORA = 64
RW_GN_EPS = 64e-5
ML_HEADS, ML_DQK, ML_DV = 8, 64, 128
ML_QK = ML_HEADS * ML_DQK
ML_V = ML_HEADS * ML_DV
DEEPNORM_ALPHA = (2.0 * DEPTH) ** 0.25
HIST = SUBLANES
DN_CHUNKS = 4
RW_CHUNKS = 4
ML_CHUNKS = 1


def _dot(a, b):
    return lax.dot_general(a.astype(MXU_DTYPE), b.astype(MXU_DTYPE), (((1,), (0,)), ((), ())),
                           preferred_element_type=F32)


def _dot_nt(a, b):
    return lax.dot_general(a.astype(MXU_DTYPE), b.astype(MXU_DTYPE), (((1,), (1,)), ((), ())),
                           preferred_element_type=F32)


def _dot_tn(a, b):
    return lax.dot_general(a.astype(MXU_DTYPE), b.astype(MXU_DTYPE), (((0,), (0,)), ((), ())),
                           preferred_element_type=F32)


def _dot_f32(a, b):
    return lax.dot_general(a, b, (((1,), (0,)), ((), ())), precision=lax.Precision.HIGHEST,
                           preferred_element_type=F32)


def _silu(x):
    return x * jax.nn.sigmoid(x)


def _perm_time(p):
    return ((p & (SUBLANES - 1)) << 3) | (p >> 3)


def _chunk_iota(n, permuted=False):
    row = lax.broadcasted_iota(jnp.int32, (n, n), 0)
    col = lax.broadcasted_iota(jnp.int32, (n, n), 1)
    eye = (row == col).astype(F32)
    if permuted:
        row, col = _perm_time(row), _perm_time(col)
    return eye, row, col


def _doubling_masks(row, col, n):
    masks = []
    s, shift = 1, 0
    while s < n:
        same = (row >> (shift + 1)) == (col >> (shift + 1))
        masks.append(same & ((row & s) != 0) & ((col & s) == 0))
        s, shift = 2 * s, shift + 1
    return masks


def _unit_lower_inverse_many(a_list, eye, masks):
    t = [eye - jnp.where(masks[0], a, 0.0) for a in a_list]
    for m in masks[1:]:
        ta = [_dot(ti, jnp.where(m, a, 0.0)) for ti, a in zip(t, a_list)]
        t = [ti - _dot(tai, ti) for ti, tai in zip(t, ta)]
    return t


HALF = LANES // 2


def _pair_lane_masks(rows):
    lane = lax.broadcasted_iota(jnp.int32, (rows, LANES), 1)
    return lane < HALF, lane & (HALF - 1)


def _row_block_diag(x, in_a):
    return jnp.concatenate([jnp.where(in_a, x, 0.0), jnp.where(in_a, 0.0, x)], axis=0)


def _unit_lower_inverse_pairs(a_list, eye2, masks2, in_a):
    t = [eye2 - jnp.where(masks2[0], a, 0.0) for a in a_list]
    for m in masks2[1:]:
        ta = [_dot(ti, _row_block_diag(jnp.where(m, a, 0.0), in_a)) for ti, a in zip(t, a_list)]
        t = [ti - _dot(tai, _row_block_diag(ti, in_a)) for ti, tai in zip(t, ta)]
    return t


def _col_tiles(ref, chunk, first, count):
    return jnp.concatenate([ref[chunk, first + i] for i in range(count)], axis=1)


def _causal_conv_silu(prev_tail, x, w_ref):
    keep = CONV_K - 1
    tail = x[CHUNK - keep * SUBLANES:, :]
    sub = lax.broadcasted_iota(jnp.int32, tail.shape, 0) & (SUBLANES - 1)
    mixed = jnp.where(sub == SUBLANES - 1, prev_tail, tail)
    wrapped = jnp.concatenate([pltpu.roll(mixed[i * SUBLANES:(i + 1) * SUBLANES, :], 1, 0) for i in range(keep)],
                              axis=0)
    acc = w_ref[CONV_K - 1:CONV_K, :] * x
    for s in range(1, CONV_K):
        shifted = jnp.concatenate([wrapped[(keep - s) * SUBLANES:, :], x[:CHUNK - s * SUBLANES, :]], axis=0)
        acc = acc + w_ref[CONV_K - 1 - s:CONV_K - s, :] * shifted
    return _silu(acc), tail


def _store_natural_order(y_ref, chunk, tile, y):
    for b in range(SUBLANES):
        y_ref[chunk, tile, pl.ds(b, CHUNK // SUBLANES, stride=SUBLANES), :] = y[b * SUBLANES:(b + 1) * SUBLANES, :]


PROJ_COLS = 2 * LANES
PROJ_ROWS = 512
POST_ROWS = 1024


def _proj_kernel(x_ref, w_ref, o_ref, *, tm):
    x = x_ref[...].astype(MXU_DTYPE)
    n = w_ref.shape[1]
    for c0 in range(0, n, PROJ_COLS):
        width = min(PROJ_COLS, n - c0)
        res = lax.dot_general(x, w_ref[:, c0:c0 + width], (((1,), (0,)), ((), ())), preferred_element_type=F32)
        for ch in range(tm // CHUNK):
            for a in range(CHUNK // SUBLANES):
                r0 = ch * CHUNK + a * SUBLANES
                for t in range(width // LANES):
                    o_ref[ch, c0 // LANES + t, pl.ds(a, SUBLANES, stride=SUBLANES), :] = (
                        res[r0:r0 + SUBLANES, t * LANES:(t + 1) * LANES])


def _project(x2d, w, tm):
    m, k = x2d.shape
    n = w.shape[1]
    return pl.pallas_call(
        functools.partial(_proj_kernel, tm=tm),
        grid=(m // tm,),
        in_specs=[pl.BlockSpec((tm, k), lambda i: (i, 0)),
                  pl.BlockSpec((k, n), lambda i: (0, 0))],
        out_specs=pl.BlockSpec((tm // CHUNK, n // LANES, CHUNK, LANES), lambda i: (i, 0, 0, 0)),
        out_shape=jax.ShapeDtypeStruct((m // CHUNK, n // LANES, CHUNK, LANES), F32),
        compiler_params=pltpu.CompilerParams(dimension_semantics=("parallel",),
                                             vmem_limit_bytes=VMEM_LIMIT),
        name="in_proj",
    )(x2d, w)


def _rw_proj_kernel(x_ref, prev_ref, w_ref, mu_ref, o_ref, buf, *, tm):
    x = x_ref[0]
    buf[HIST:HIST + tm, :] = x
    buf[0:HIST, :] = jnp.where(pl.program_id(1) == 0, 0.0, prev_ref[0])
    dx = buf[pl.ds(HIST - 1, tm), :] - x
    for g in range(4):
        lhs = x + mu_ref[g:g + 1, :] * dx
        o_ref[0, :, g * RW_W:(g + 1) * RW_W] = _dot(lhs, w_ref[:, g * RW_W:(g + 1) * RW_W])
    w_lo = w_ref[:, 4 * RW_W:4 * RW_W + LANES]
    lo_w = _dot(x + mu_ref[4:5, :] * dx, w_lo)
    lo_a = _dot(x + mu_ref[5:6, :] * dx, w_lo)
    lane = lax.broadcasted_iota(jnp.int32, lo_w.shape, 1)
    o_ref[0, :, 4 * RW_W:4 * RW_W + LANES] = jnp.where(lane < RW_LORA, lo_w, lo_a)


def _rw_project(x, w, mu, tm):
    b, s, d = x.shape
    n = w.shape[1]
    per = tm // HIST
    return pl.pallas_call(
        functools.partial(_rw_proj_kernel, tm=tm),
        grid=(b, s // tm),
        in_specs=[pl.BlockSpec((1, tm, d), lambda i, j: (i, j, 0)),
                  pl.BlockSpec((1, HIST, d), lambda i, j: (i, jnp.maximum(j * per - 1, 0), 0)),
                  pl.BlockSpec((d, n), lambda i, j: (0, 0)),
                  pl.BlockSpec(mu.shape, lambda i, j: (0, 0))],
        out_specs=pl.BlockSpec((1, tm, n), lambda i, j: (i, j, 0)),
        out_shape=jax.ShapeDtypeStruct((b, s, n), F32),
        scratch_shapes=[pltpu.VMEM((tm + HIST, d), F32)],
        compiler_params=pltpu.CompilerParams(dimension_semantics=("parallel", "parallel"),
                                             vmem_limit_bytes=VMEM_LIMIT),
        name="rw_in_proj",
    )(x, x, w, mu)


def _dn_kernel(h_ref, convw_ref, alog_ref, dtb_ref, normw_ref, y_ref, hist, state):
    c = CHUNK
    nc = h_ref.shape[0]

    @pl.when(pl.program_id(1) == 0)
    def _():
        hist[...] = jnp.zeros(hist.shape, F32)
        state[...] = jnp.zeros(state.shape, F32)

    qkv_tiles = (2 * DN_QK + DN_V) // LANES
    z_tile0 = qkv_tiles
    _, row, col = _chunk_iota(c, permuted=True)
    causal_f = (row >= col).astype(F32)
    in_a, col2 = _pair_lane_masks(c)
    row2 = lax.broadcasted_iota(jnp.int32, (c, LANES), 0)
    eye2 = (row2 == col2).astype(F32)
    row2, col2 = _perm_time(row2), _perm_time(col2)
    causal2 = row2 >= col2
    strict2 = row2 > col2
    masks2 = _doubling_masks(row2, col2, c)

    heads = range(DN_HEADS)
    units = [(n, h) for n in range(nc) for h in heads]
    ids = range(len(units))
    qkv, gcum, gcum_t, beta_t = [], [], [], []
    tail = hist[...]
    for n in range(nc):
        y_n, tail = _causal_conv_silu(tail, _col_tiles(h_ref, n, 0, qkv_tiles), convw_ref)
        qkv.append(y_n)
        gates = h_ref[n, z_tile0 + DN_V // LANES]
        beta_t.append(jax.nn.sigmoid(gates))
        g_t = -(jnp.exp(alog_ref[...]) * jax.nn.softplus(gates + dtb_ref[...]))
        gcum.append(_dot_f32(causal_f, g_t))
        gcum_t.append(gcum[n].T)
    hist[...] = tail

    q = [qkv[n][:, h * DN_DK:(h + 1) * DN_DK] for n, h in units]
    k = [qkv[n][:, DN_QK + h * DN_DK:DN_QK + (h + 1) * DN_DK] for n, h in units]
    v = [qkv[n][:, 2 * DN_QK + h * DN_DV:2 * DN_QK + (h + 1) * DN_DV] for n, h in units]
    q = [x * (lax.rsqrt(jnp.sum(x * x, -1, keepdims=True) + 1e-6) * DN_DK ** -0.5) for x in q]
    k = [x * lax.rsqrt(jnp.sum(x * x, -1, keepdims=True) + 1e-6) for x in k]
    beta = [beta_t[n][:, h:h + 1] for n, h in units]
    g_col = [gcum[n][:, DN_HEADS + h:DN_HEADS + h + 1] for n, h in units]
    g_row = [gcum_t[n][DN_HEADS + h:DN_HEADS + h + 1, :] for n, h in units]
    g_last = [g[c - 1:c, :] for g in g_col]
    exp_g = [jnp.exp(g) for g in g_col]
    k_beta = [k[i] * beta[i] for i in ids]
    zeros = jnp.zeros((c, DN_DK), F32)
    m2, decay2 = [], []
    for i0 in range(0, len(units), 2):
        i1 = i0 + 1
        lhs = jnp.concatenate([jnp.concatenate([k_beta[i0], k_beta[i1]], axis=1),
                               jnp.concatenate([q[i0], q[i1]], axis=1)], axis=0)
        rhs = jnp.concatenate([jnp.concatenate([k[i0], zeros], axis=1),
                               jnp.concatenate([zeros, k[i1]], axis=1)], axis=0)
        m2.append(_dot_nt(lhs, rhs))
        g_col2 = jnp.where(in_a, g_col[i0], g_col[i1])
        g_row2 = jnp.concatenate([g_row[i0], g_row[i1]], axis=1)
        decay2.append(jnp.exp(jnp.where(causal2, g_col2 - g_row2, -jnp.inf)))
    a2 = [jnp.where(strict2, m[:c] * d, 0.0) for m, d in zip(m2, decay2)]
    qk2 = [m[c:] * d for m, d in zip(m2, decay2)]
    t2 = _unit_lower_inverse_pairs(a2, eye2, masks2, in_a)
    t = [t2[i // 2][:, (i % 2) * c:(i % 2 + 1) * c] for i in ids]
    qk = [qk2[i // 2][:, (i % 2) * c:(i % 2 + 1) * c] for i in ids]
    uw = [_dot(t[i], jnp.concatenate([v[i] * beta[i], k_beta[i] * exp_g[i]], axis=1)) for i in ids]
    q_dec = [q[i] * exp_g[i] for i in ids]
    k_dec = [k[i] * jnp.exp(g_last[i] - g_col[i]) for i in ids]
    s_cur = [state[h] for h in heads]
    o = []
    for n in range(nc):
        idn = [n * DN_HEADS + h for h in heads]
        ws = [_dot(jnp.concatenate([uw[i][:, DN_DV:], q_dec[i]], axis=0), s_cur[h]) for h, i in zip(heads, idn)]
        v_new = [uw[i][:, :DN_DV] - ws[h][:c] for h, i in zip(heads, idn)]
        o += [ws[h][c:] + _dot(qk[i], v_new[h]) for h, i in zip(heads, idn)]
        s_cur = [s_cur[h] * jnp.exp(g_last[i]) + _dot_tn(k_dec[i], v_new[h]) for h, i in zip(heads, idn)]
    for h in heads:
        state[h] = s_cur[h]
    for i, (n, h) in enumerate(units):
        z = h_ref[n, z_tile0 + h]
        on = o[i] * lax.rsqrt(jnp.mean(o[i] * o[i], -1, keepdims=True) + 1e-6) * normw_ref[...]
        _store_natural_order(y_ref, n, h, on * _silu(z))


def _dn_mixer(h, b, nc, conv_w, alog_row, dtb_row, norm_w):
    nch = h.shape[0] // b // nc
    ncols = 2 * DN_QK + DN_V
    return pl.pallas_call(
        _dn_kernel,
        grid=(b, nch),
        in_specs=[pl.BlockSpec((nc,) + h.shape[1:], lambda i, j: (i * nch + j, 0, 0, 0)),
                  pl.BlockSpec(conv_w.shape, lambda i, j: (0, 0)),
                  pl.BlockSpec(alog_row.shape, lambda i, j: (0, 0)),
                  pl.BlockSpec(dtb_row.shape, lambda i, j: (0, 0)),
                  pl.BlockSpec(norm_w.shape, lambda i, j: (0, 0))],
        out_specs=pl.BlockSpec((nc, DN_V // LANES, CHUNK, LANES), lambda i, j: (i * nch + j, 0, 0, 0)),
        out_shape=jax.ShapeDtypeStruct((h.shape[0], DN_V // LANES, CHUNK, LANES), F32),
        scratch_shapes=[pltpu.VMEM(((CONV_K - 1) * SUBLANES, ncols), F32),
                        pltpu.VMEM((DN_HEADS, DN_DK, DN_DV), F32)],
        compiler_params=pltpu.CompilerParams(dimension_semantics=("parallel", "arbitrary"),
                                             vmem_limit_bytes=VMEM_LIMIT),
        name="dn_mixer",
    )(h, conv_w, alog_row, dtb_row, norm_w)


def _rw_kernel(h_ref, wup_ref, aup_ref, w0_ref, a0_ref, kk_ref, ka_ref, rk_ref, gnw_ref, gnb_ref,
               y_ref, state):
    c = CHUNK
    nc = y_ref.shape[0]
    pairs = RW_W // LANES

    @pl.when(pl.program_id(1) == 0)
    def _():
        state[...] = jnp.zeros(state.shape, F32)

    _, row, col = _chunk_iota(c)
    in_a, col2 = _pair_lane_masks(c)
    row2 = lax.broadcasted_iota(jnp.int32, (c, LANES), 0)
    causal2 = row2 >= col2
    strict2 = row2 > col2
    eye2 = (row2 == col2).astype(F32)
    masks2 = _doubling_masks(row2, col2, c)
    causal_f = (row >= col).astype(F32)
    row_big = lax.broadcasted_iota(jnp.int32, (LANES, LANES), 0)
    col_big = lax.broadcasted_iota(jnp.int32, (LANES, LANES), 1)
    same_head = (row_big < RW_HEAD) == (col_big < RW_HEAD)

    row_bd = functools.partial(_row_block_diag, in_a=in_a)

    def half_sums(x):
        sum_a = jnp.sum(jnp.where(in_a, x, 0.0), -1, keepdims=True)
        sum_b = jnp.sum(jnp.where(in_a, 0.0, x), -1, keepdims=True)
        return jnp.where(in_a, sum_a, sum_b)

    units = [(n, p) for n in range(nc) for p in range(pairs)]
    ids = range(len(units))
    tiles = [slice(p * LANES, (p + 1) * LANES) for p in range(pairs)]

    r_all, v_all, z_all, a_all, k2_all, rkr_all, kku_all = ([] for _ in range(7))
    p_incl, p_excl, p_inv, p_tail, p_last = ([] for _ in range(5))
    for n in range(nc):
        rows = slice(n * c, (n + 1) * c)
        r_all.append(h_ref[0, rows, 0:RW_W])
        k_n = h_ref[0, rows, RW_W:2 * RW_W]
        v_all.append(h_ref[0, rows, 2 * RW_W:3 * RW_W])
        z_all.append(h_ref[0, rows, 3 * RW_W:4 * RW_W])
        lo = h_ref[0, rows, 4 * RW_W:4 * RW_W + LANES]
        w_log = -jax.nn.softplus(-(w0_ref[...] + _dot(jnp.tanh(lo[:, :RW_LORA]), wup_ref[...]))) - 0.5
        a_all.append(jax.nn.sigmoid(a0_ref[...] + _dot(lo[:, RW_LORA:], aup_ref[...])))
        logd = -jnp.exp(w_log)
        lcum = _dot_f32(causal_f, logd)
        l_last = lcum[c - 1:c, :]
        p_incl.append(jnp.exp(lcum))
        p_excl.append(jnp.exp(lcum - logd))
        p_inv.append(jnp.exp(-lcum))
        p_tail.append(jnp.exp(l_last - lcum))
        p_last.append(jnp.exp(l_last))
        kku_all.append(k_n * kk_ref[...])
        k2_all.append(k_n * (1.0 + (a_all[n] - 1.0) * ka_ref[...]))
        rkr_all.append(r_all[n] * k2_all[n] * rk_ref[...])

    v = [v_all[n][:, tiles[p]] for n, p in units]
    k2 = [k2_all[n][:, tiles[p]] for n, p in units]
    kku = [kku_all[n][:, tiles[p]] for n, p in units]
    kk = [kku[i] * lax.rsqrt(half_sums(kku[i] * kku[i]) + 1e-6) for i in ids]
    alpha = [-(kk[i] * a_all[n][:, tiles[p]]) for i, (n, p) in enumerate(units)]
    lhs = [jnp.concatenate([kk[i] * p_excl[n][:, tiles[p]], r_all[n][:, tiles[p]] * p_incl[n][:, tiles[p]]], axis=0)
           for i, (n, p) in enumerate(units)]
    rhs = [jnp.concatenate([row_bd(alpha[i] * p_inv[n][:, tiles[p]]), row_bd(k2[i] * p_inv[n][:, tiles[p]])], axis=0)
           for i, (n, p) in enumerate(units)]
    m = [_dot_nt(lhs[i], rhs[i]) for i in ids]
    a_ab = [jnp.where(strict2, -x[:c, :LANES], 0.0) for x in m]
    a_ak = [jnp.where(strict2, x[:c, LANES:], 0.0) for x in m]
    a_r = [jnp.concatenate([jnp.where(causal2, x[c:, :LANES], 0.0), jnp.where(causal2, x[c:, LANES:], 0.0)], axis=1)
           for x in m]
    t = _unit_lower_inverse_pairs(a_ab, eye2, masks2, in_a)
    v_bd = [row_bd(x) for x in v]
    akv = [_dot(a_ak[i], v_bd[i]) for i in ids]
    tail = [jnp.concatenate([alpha[i] * p_tail[n][:, tiles[p]], k2[i] * p_tail[n][:, tiles[p]]], axis=0)
            for i, (n, p) in enumerate(units)]

    s_cur = [state[p] for p in range(pairs)]
    y = []
    for n in range(nc):
        idn = [n * pairs + p for p in range(pairs)]
        lhs_s = [_dot_nt(lhs[i], s_cur[p]) for p, i in enumerate(idn)]
        u = [_dot(t[i], row_bd(lhs_s[p][:c] + akv[i])) for p, i in enumerate(idn)]
        y += [lhs_s[p][c:] + _dot(a_r[i], jnp.concatenate([row_bd(u[p]), v_bd[i]], axis=0))
              for p, i in enumerate(idn)]
        s_cur = [s_cur[p] * p_last[n][:, tiles[p]]
                 + jnp.where(same_head, _dot_tn(jnp.concatenate([u[p], v[i]], axis=0), tail[i]), 0.0)
                 for p, i in enumerate(idn)]
    for p in range(pairs):
        state[p] = s_cur[p]
    inv_d = 1.0 / RW_HEAD
    for i, (n, p) in enumerate(units):
        mu = half_sums(y[i]) * inv_d
        yc = y[i] - mu
        var = half_sums(yc * yc) * inv_d
        yn = yc * lax.rsqrt(var + RW_GN_EPS) * gnw_ref[:, tiles[p]] + gnb_ref[:, tiles[p]]
        bonus = half_sums(rkr_all[n][:, tiles[p]]) * v[i]
        y_ref[n, p] = (yn + bonus) * _silu(z_all[n][:, tiles[p]])


def _rw_mixer(h, nc, wup, aup, w0, a0, k_k, k_a, r_k, gn_w, gn_b):
    b, s, n = h.shape
    nch = s // CHUNK // nc
    full = lambda arr: pl.BlockSpec(arr.shape, lambda i, j: (0,) * arr.ndim)
    params = (wup, aup, w0, a0, k_k, k_a, r_k, gn_w, gn_b)
    return pl.pallas_call(
        _rw_kernel,
        grid=(b, nch),
        in_specs=[pl.BlockSpec((1, nc * CHUNK, n), lambda i, j: (i, j, 0))] + [full(a) for a in params],
        out_specs=pl.BlockSpec((nc, RW_W // LANES, CHUNK, LANES), lambda i, j: (i * nch + j, 0, 0, 0)),
        out_shape=jax.ShapeDtypeStruct((b * nch * nc, RW_W // LANES, CHUNK, LANES), F32),
        scratch_shapes=[pltpu.VMEM((RW_W // LANES, LANES, LANES), F32)],
        compiler_params=pltpu.CompilerParams(dimension_semantics=("parallel", "arbitrary"),
                                             vmem_limit_bytes=VMEM_LIMIT),
        name="rw_mixer",
    )(h, *params)


def _ml_kernel(h_ref, convw_ref, ib_ref, fb_ref, gnw_ref, y_ref, hist, cstate, mstate):
    c = CHUNK
    nc = h_ref.shape[0]

    @pl.when(pl.program_id(1) == 0)
    def _():
        hist[...] = jnp.zeros(hist.shape, F32)
        cstate[...] = jnp.zeros(cstate.shape, F32)
        mstate[...] = jnp.full(mstate.shape, -jnp.inf, F32)

    qk_tiles = 2 * ML_QK // LANES
    v_tiles = ML_V // LANES
    _, row, col = _chunk_iota(c, permuted=True)
    causal = row >= col
    causal_f = causal.astype(F32)
    heads = range(ML_HEADS)
    units = [(n, h) for n in range(nc) for h in heads]
    ids = range(len(units))

    qk_all, i_t, i_tt, bcum, bcum_t = [], [], [], [], []
    tail = hist[...]
    for n in range(nc):
        y_n, tail = _causal_conv_silu(tail, _col_tiles(h_ref, n, 0, qk_tiles), convw_ref)
        qk_all.append(y_n)
        gates = h_ref[n, qk_tiles + 3 * v_tiles]
        i_t.append(gates + ib_ref[...])
        i_tt.append(i_t[n].T)
        bcum.append(_dot_f32(causal_f, jax.nn.log_sigmoid(gates + fb_ref[...])))
        bcum_t.append(bcum[n].T)
    hist[...] = tail

    q = [qk_all[n][:, h * ML_DQK:(h + 1) * ML_DQK] for n, h in units]
    k = [qk_all[n][:, ML_QK + h * ML_DQK:ML_QK + (h + 1) * ML_DQK] * ML_DQK ** -0.5 for n, h in units]
    v = [h_ref[n, qk_tiles + h] for n, h in units]
    b_col = [bcum[n][:, ML_HEADS + h:ML_HEADS + h + 1] for n, h in units]
    b_row = [bcum_t[n][ML_HEADS + h:ML_HEADS + h + 1, :] for n, h in units]
    i_col = [i_t[n][:, h:h + 1] for n, h in units]
    i_row = [i_tt[n][h:h + 1, :] for n, h in units]
    b_last = [x[c - 1:c, :] for x in b_col]
    d_log = [jnp.where(causal, b_col[i] - b_row[i] + i_row[i], -jnp.inf) for i in ids]
    m_intra = [jnp.max(x, -1, keepdims=True) for x in d_log]
    qk = [_dot_nt(q[i], k[i]) for i in ids]

    m_prev, m_t = [], []
    m_cur = [mstate[h][0:1, 0:1] for h in heads]
    for n in range(nc):
        for h in heads:
            i = n * ML_HEADS + h
            m_prev.append(m_cur[h])
            m_t.append(jnp.maximum(m_cur[h] + b_col[i], m_intra[i]))
            m_cur[h] = m_t[i][c - 1:c, :]
    m_new = [x[c - 1:c, :] for x in m_t]
    inter = [jnp.exp(m_prev[i] + b_col[i] - m_t[i]) for i in ids]
    w_qk = [jnp.exp(d_log[i] - m_t[i]) * qk[i] for i in ids]
    carry = [jnp.exp(m_prev[i] + b_last[i] - m_new[i]) for i in ids]
    k_w = [k[i] * jnp.exp(b_last[i] - b_col[i] + i_col[i] - m_new[i]) for i in ids]
    ones = jnp.ones((c, LANES), F32)
    v_ext = [jnp.concatenate([x, ones], axis=1) for x in v]
    kv = [_dot_tn(k_w[i], v_ext[i]) for i in ids]

    c_in = []
    c_cur = [cstate[h] for h in heads]
    for n in range(nc):
        for h in heads:
            i = n * ML_HEADS + h
            c_in.append(c_cur[h])
            c_cur[h] = c_cur[h] * carry[i] + kv[i]
    for h in heads:
        cstate[h] = c_cur[h]
        mstate[h] = jnp.broadcast_to(m_cur[h], mstate.shape[1:])

    nd = [inter[i] * _dot(q[i], c_in[i]) + _dot(w_qk[i], v_ext[i]) for i in ids]
    h_tilde = [nd[i][:, :ML_DV] / jnp.maximum(jnp.abs(nd[i][:, ML_DV:]), jnp.exp(-m_t[i])) for i in ids]
    for i, (n, h) in enumerate(units):
        og = h_ref[n, qk_tiles + v_tiles + h]
        z = h_ref[n, qk_tiles + 2 * v_tiles + h]
        xg = jax.nn.sigmoid(og) * h_tilde[i]
        mu = jnp.mean(xg, -1, keepdims=True)
        var = jnp.mean(jnp.square(xg - mu), -1, keepdims=True)
        xn = (xg - mu) * lax.rsqrt(var + 1e-6) * gnw_ref[:, h * ML_DV:(h + 1) * ML_DV]
        _store_natural_order(y_ref, n, h, xn * _silu(z))


def _ml_mixer(h, b, nc, conv_w, ib_row, fb_row, gn_w):
    nch = h.shape[0] // b // nc
    return pl.pallas_call(
        _ml_kernel,
        grid=(b, nch),
        in_specs=[pl.BlockSpec((nc,) + h.shape[1:], lambda i, j: (i * nch + j, 0, 0, 0)),
                  pl.BlockSpec(conv_w.shape, lambda i, j: (0, 0)),
                  pl.BlockSpec(ib_row.shape, lambda i, j: (0, 0)),
                  pl.BlockSpec(fb_row.shape, lambda i, j: (0, 0)),
                  pl.BlockSpec(gn_w.shape, lambda i, j: (0, 0))],
        out_specs=pl.BlockSpec((nc, ML_V // LANES, CHUNK, LANES), lambda i, j: (i * nch + j, 0, 0, 0)),
        out_shape=jax.ShapeDtypeStruct((h.shape[0], ML_V // LANES, CHUNK, LANES), F32),
        scratch_shapes=[pltpu.VMEM(((CONV_K - 1) * SUBLANES, 2 * ML_QK), F32),
                        pltpu.VMEM((ML_HEADS, ML_DQK, ML_DV + LANES), F32),
                        pltpu.VMEM((ML_HEADS, SUBLANES, LANES), F32)],
        compiler_params=pltpu.CompilerParams(dimension_semantics=("parallel", "arbitrary"),
                                             vmem_limit_bytes=VMEM_LIMIT),
        name="ml_mixer",
    )(h, conv_w, ib_row, fb_row, gn_w)


def _post_kernel(x_ref, y_ref, p_ref, wout_ref, wg_ref, wp_ref, lng_ref, lnb_ref, pnw_ref, o_ref):
    y = jnp.concatenate([jnp.concatenate([y_ref[ch, t] for t in range(y_ref.shape[1])], axis=1)
                         for ch in range(y_ref.shape[0])], axis=0)
    r = DEEPNORM_ALPHA * x_ref[...] + _dot(y, wout_ref[...])
    mu = jnp.mean(r, -1, keepdims=True)
    var = jnp.mean(jnp.square(r - mu), -1, keepdims=True)
    xn = (r - mu) * lax.rsqrt(var + LN_EPS) * lng_ref[...] + lnb_ref[...]
    gate = jax.nn.sigmoid(_dot(xn, wg_ref[...]))
    pp = _dot(p_ref[...], wp_ref[...])
    pn = pp * lax.rsqrt(jnp.mean(pp * pp, -1, keepdims=True) + 1e-6) * pnw_ref[...]
    o_ref[...] = xn + gate * pn


def _post(x2d, y4d, p3d, layer, w_out, w_gate, w_proj, ln_g, ln_b, pn_w, tm):
    m, d = x2d.shape
    p_spec = pl.BlockSpec((None, tm, p3d.shape[2]), lambda i: (layer, i, 0))
    y_spec = pl.BlockSpec((tm // CHUNK,) + y4d.shape[1:], lambda i: (i, 0, 0, 0))
    tile = lambda arr: pl.BlockSpec((tm, arr.shape[1]), lambda i: (i, 0))
    full = lambda arr: pl.BlockSpec(arr.shape, lambda i: (0, 0))
    params = (w_out, w_gate, w_proj, ln_g, ln_b, pn_w)
    return pl.pallas_call(
        _post_kernel,
        grid=(m // tm,),
        in_specs=[tile(x2d), y_spec, p_spec] + [full(a) for a in params],
        out_specs=pl.BlockSpec((tm, d), lambda i: (i, 0)),
        out_shape=jax.ShapeDtypeStruct((m, d), F32),
        compiler_params=pltpu.CompilerParams(dimension_semantics=("parallel",),
                                             vmem_limit_bytes=VMEM_LIMIT),
        name="post_block",
    )(x2d, y4d, p3d, *params)


def _pad_cols(w, n):
    return jnp.pad(w, ((0, 0), (0, n - w.shape[1])))


def _lane_row(vec, offset):
    return jnp.zeros((1, LANES), F32).at[0, offset:offset + vec.shape[0]].set(vec.astype(F32))


def _row_tile(m, cap):
    t = cap
    while t >= CHUNK:
        if m % t == 0:
            return t
        t //= 2
    raise ValueError(f"token count {m} must be a multiple of {CHUNK}")


def kernel(x, p, ln_g, ln_b, ple_w_proj, ple_norm_w, ple_w_gate, dn_w_in, dn_conv_w, dn_a_log, dn_dt_bias, dn_norm_w, dn_w_out, rw_w_in, rw_mu, rw_w0, rw_w_lora_up, rw_a0, rw_a_lora_up, rw_k_k, rw_k_a, rw_r_k, rw_gn_w, rw_gn_b, rw_w_out, ml_w_in, ml_conv_w, ml_i_bias, ml_f_bias, ml_gn_w, ml_w_out):
    b, s, d = x.shape
    assert d == D_MODEL and s % CHUNK == 0
    m = b * s
    tm_proj = _row_tile(m, PROJ_ROWS)
    tm_post = _row_tile(m, POST_ROWS)
    tm_rw = _row_tile(s, PROJ_ROWS)
    chunks_per_step = lambda want: next(n for n in (want, 2, 1) if (s // CHUNK) % n == 0)
    bf = lambda w: w.astype(MXU_DTYPE)
    row = lambda v: v.reshape(1, -1).astype(F32)
    x2d = x.reshape(m, d)
    p3d = p.reshape(DEPTH, m, D_PLE)
    for i in range(DEPTH):
        kind, j = i % 3, i // 3
        if kind == 0:
            n_pad = 2 * DN_QK + 2 * DN_V + LANES
            h = _project(x2d, bf(_pad_cols(dn_w_in[j], n_pad)), tm_proj)
            y = _dn_mixer(h, b, chunks_per_step(DN_CHUNKS), dn_conv_w[j], _lane_row(dn_a_log[j], DN_HEADS),
                          _lane_row(dn_dt_bias[j], DN_HEADS), row(dn_norm_w[j]))
            w_out = dn_w_out[j]
        elif kind == 1:
            r_w, wl_w, k_w, v_w, al_w, z_w = jnp.split(
                rw_w_in[j], np.cumsum([RW_W, RW_LORA, RW_W, RW_W, RW_LORA]).tolist(), axis=1)
            w_cat = jnp.concatenate([r_w, k_w, v_w, z_w, wl_w, al_w], axis=1)
            mu = rw_mu[j]
            mu_cat = jnp.stack([mu[0], mu[2], mu[3], mu[5], mu[1], mu[4], mu[0], mu[0]], axis=0)
            h = _rw_project(x2d.reshape(b, s, d), bf(w_cat), mu_cat, tm_rw)
            y = _rw_mixer(h, chunks_per_step(RW_CHUNKS), bf(rw_w_lora_up[j]), bf(rw_a_lora_up[j]), row(rw_w0[j]),
                          row(rw_a0[j]), row(rw_k_k[j]), row(rw_k_a[j]), row(rw_r_k[j]), row(rw_gn_w[j]),
                          row(rw_gn_b[j]))
            w_out = rw_w_out[j]
        else:
            n_pad = 2 * ML_QK + 3 * ML_V + LANES
            h = _project(x2d, bf(_pad_cols(ml_w_in[j], n_pad)), tm_proj)
            y = _ml_mixer(h, b, chunks_per_step(ML_CHUNKS), ml_conv_w[j], _lane_row(ml_i_bias[j], 0),
                          _lane_row(ml_f_bias[j], ML_HEADS), row(ml_gn_w[j]))
            w_out = ml_w_out[j]
        x2d = _post(x2d, y, p3d, i, bf(w_out), bf(ple_w_gate[i]),
                    bf(ple_w_proj[i]), row(ln_g[i]), row(ln_b[i]), row(ple_norm_w[i]), tm_post)
    return x2d.reshape(b, s, d)
```

```python
import functools

import jax
import jax.numpy as jnp
import numpy as np
from jax import lax
from jax.experimental import pallas as pl
from jax.experimental.pallas import tpu as pltpu

F32 = jnp.float32
MXU_DTYPE = jnp.bfloat16

LANES = 128
SUBLANES = 8
VMEM_LIMIT = 48 * 1024 * 1024

DEPTH = 4
D_MODEL = 1024
D_PLE = 256
CONV_K = 4
CHUNK = 64
LN_EPS = 1e-5
DN_HEADS, DN_DK, DN_DV = 8, 128, 128
DN_QK = DN_HEADS * DN_DK
DN_V = DN_HEADS * DN_DV
RW_HEAD = 64
RW_HEADS = D_MODEL // RW_HEAD
RW_W = RW_HEADS * RW_HEAD
RW_LORA = 64
RW_GN_EPS = 64e-5
ML_HEADS, ML_DQK, ML_DV = 8, 64, 128
ML_QK = ML_HEADS * ML_DQK
ML_V = ML_HEADS * ML_DV
DEEPNORM_ALPHA = (2.0 * DEPTH) ** 0.25
HIST = SUBLANES
DN_CHUNKS = 4
RW_CHUNKS = 4
ML_CHUNKS = 1


def _dot(a, b):
    return lax.dot_general(a.astype(MXU_DTYPE), b.astype(MXU_DTYPE), (((1,), (0,)), ((), ())),
                           preferred_element_type=F32)


def _dot_nt(a, b):
    return lax.dot_general(a.astype(MXU_DTYPE), b.astype(MXU_DTYPE), (((1,), (1,)), ((), ())),
                           preferred_element_type=F32)


def _dot_tn(a, b):
    return lax.dot_general(a.astype(MXU_DTYPE), b.astype(MXU_DTYPE), (((0,), (0,)), ((), ())),
                           preferred_element_type=F32)


def _silu(x):
    return x * jax.nn.sigmoid(x)


def _perm_time(p):
    return ((p & (SUBLANES - 1)) << 3) | (p >> 3)


def _chunk_iota(n, permuted=False):
    row = lax.broadcasted_iota(jnp.int32, (n, n), 0)
    col = lax.broadcasted_iota(jnp.int32, (n, n), 1)
    eye = (row == col).astype(F32)
    if permuted:
        row, col = _perm_time(row), _perm_time(col)
    return eye, row, col


def _doubling_masks(row, col, n):
    masks = []
    s, shift = 1, 0
    while s < n:
        same = (row >> (shift + 1)) == (col >> (shift + 1))
        masks.append(same & ((row & s) != 0) & ((col & s) == 0))
        s, shift = 2 * s, shift + 1
    return masks


HALF = LANES // 2


def _pair_lane_masks(rows):
    lane = lax.broadcasted_iota(jnp.int32, (rows, LANES), 1)
    return lane < HALF, lane & (HALF - 1)


def _row_block_diag(x, in_a):
    return jnp.concatenate([jnp.where(in_a, x, 0.0), jnp.where(in_a, 0.0, x)], axis=0)


def _unit_lower_inverse_pairs(a_list, eye2, masks2, in_a):
    t = [eye2 - jnp.where(masks2[0], a, 0.0) for a in a_list]
    for m in masks2[1:]:
        ta = [_dot(ti, _row_block_diag(jnp.where(m, a, 0.0), in_a)) for ti, a in zip(t, a_list)]
        t = [ti - _dot(tai, _row_block_diag(ti, in_a)) for ti, tai in zip(t, ta)]
    return t


def _col_tiles(ref, chunk, first, count):
    return jnp.concatenate([ref[chunk, first + i] for i in range(count)], axis=1)


def _causal_conv_silu(prev_tail, x, w_ref):
    keep = CONV_K - 1
    tail = x[CHUNK - keep * SUBLANES:, :]
    sub = lax.broadcasted_iota(jnp.int32, tail.shape, 0) & (SUBLANES - 1)
    mixed = jnp.where(sub == SUBLANES - 1, prev_tail, tail)
    wrapped = jnp.concatenate([pltpu.roll(mixed[i * SUBLANES:(i + 1) * SUBLANES, :], 1, 0) for i in range(keep)],
                              axis=0)
    acc = w_ref[CONV_K - 1:CONV_K, :] * x
    for s in range(1, CONV_K):
        shifted = jnp.concatenate([wrapped[(keep - s) * SUBLANES:, :], x[:CHUNK - s * SUBLANES, :]], axis=0)
        acc = acc + w_ref[CONV_K - 1 - s:CONV_K - s, :] * shifted
    return _silu(acc), tail


def _store_natural_order(y_ref, chunk, tile, y):
    for b in range(SUBLANES):
        y_ref[chunk, tile, pl.ds(b, CHUNK // SUBLANES, stride=SUBLANES), :] = y[b * SUBLANES:(b + 1) * SUBLANES, :]


PROJ_COLS = 2 * LANES
PROJ_ROWS = 512
POST_ROWS = 1024


def _proj_kernel(x_ref, w_ref, o_ref, *, tm):
    x = x_ref[...].astype(MXU_DTYPE)
    n = w_ref.shape[1]
    for c0 in range(0, n, PROJ_COLS):
        width = min(PROJ_COLS, n - c0)
        res = lax.dot_general(x, w_ref[:, c0:c0 + width], (((1,), (0,)), ((), ())), preferred_element_type=F32)
        for ch in range(tm // CHUNK):
            for a in range(CHUNK // SUBLANES):
                r0 = ch * CHUNK + a * SUBLANES
                for t in range(width // LANES):
                    o_ref[ch, c0 // LANES + t, pl.ds(a, SUBLANES, stride=SUBLANES), :] = (
                        res[r0:r0 + SUBLANES, t * LANES:(t + 1) * LANES])


def _project(x2d, w, tm):
    m, k = x2d.shape
    n = w.shape[1]
    return pl.pallas_call(
        functools.partial(_proj_kernel, tm=tm),
        grid=(m // tm,),
        in_specs=[pl.BlockSpec((tm, k), lambda i: (i, 0)),
                  pl.BlockSpec((k, n), lambda i: (0, 0))],
        out_specs=pl.BlockSpec((tm // CHUNK, n // LANES, CHUNK, LANES), lambda i: (i, 0, 0, 0)),
        out_shape=jax.ShapeDtypeStruct((m // CHUNK, n // LANES, CHUNK, LANES), F32),
        compiler_params=pltpu.CompilerParams(dimension_semantics=("parallel",),
                                             vmem_limit_bytes=VMEM_LIMIT),
        name="in_proj",
    )(x2d, w)


def _rw_proj_kernel(x_ref, prev_ref, w_ref, mu_ref, o_ref, buf, *, tm):
    x = x_ref[0]
    buf[HIST:HIST + tm, :] = x
    buf[0:HIST, :] = jnp.where(pl.program_id(1) == 0, 0.0, prev_ref[0])
    dx = buf[pl.ds(HIST - 1, tm), :] - x
    for g in range(4):
        lhs = x + mu_ref[g:g + 1, :] * dx
        o_ref[0, :, g * RW_W:(g + 1) * RW_W] = _dot(lhs, w_ref[:, g * RW_W:(g + 1) * RW_W])
    w_lo = w_ref[:, 4 * RW_W:4 * RW_W + LANES]
    lo_w = _dot(x + mu_ref[4:5, :] * dx, w_lo)
    lo_a = _dot(x + mu_ref[5:6, :] * dx, w_lo)
    lane = lax.broadcasted_iota(jnp.int32, lo_w.shape, 1)
    o_ref[0, :, 4 * RW_W:4 * RW_W + LANES] = jnp.where(lane < RW_LORA, lo_w, lo_a)


def _rw_project(x, w, mu, tm):
    b, s, d = x.shape
    n = w.shape[1]
    per = tm // HIST
    return pl.pallas_call(
        functools.partial(_rw_proj_kernel, tm=tm),
        grid=(b, s // tm),
        in_specs=[pl.BlockSpec((1, tm, d), lambda i, j: (i, j, 0)),
                  pl.BlockSpec((1, HIST, d), lambda i, j: (i, jnp.maximum(j * per - 1, 0), 0)),
                  pl.BlockSpec((d, n), lambda i, j: (0, 0)),
                  pl.BlockSpec(mu.shape, lambda i, j: (0, 0))],
        out_specs=pl.BlockSpec((1, tm, n), lambda i, j: (i, j, 0)),
        out_shape=jax.ShapeDtypeStruct((b, s, n), F32),
        scratch_shapes=[pltpu.VMEM((tm + HIST, d), F32)],
        compiler_params=pltpu.CompilerParams(dimension_semantics=("parallel", "parallel"),
                                             vmem_limit_bytes=VMEM_LIMIT),
        name="rw_in_proj",
    )(x, x, w, mu)


def _dn_kernel(h_ref, convw_ref, alog_ref, dtb_ref, normw_ref, y_ref, hist, state):
    c = CHUNK
    nc = h_ref.shape[0]

    @pl.when(pl.program_id(1) == 0)
    def _():
        hist[...] = jnp.zeros(hist.shape, F32)
        state[...] = jnp.zeros(state.shape, F32)

    qkv_tiles = (2 * DN_QK + DN_V) // LANES
    z_tile0 = qkv_tiles
    in_a, col2 = _pair_lane_masks(c)
    row2 = lax.broadcasted_iota(jnp.int32, (c, LANES), 0)
    eye2 = (row2 == col2).astype(F32)
    row2, col2 = _perm_time(row2), _perm_time(col2)
    causal2 = row2 >= col2
    strict2 = row2 > col2
    masks2 = _doubling_masks(row2, col2, c)

    heads = range(DN_HEADS)
    units = [(n, h) for n in range(nc) for h in heads]
    ids = range(len(units))
    qkv, gcum, gcum_t, beta_t, expg_t, kdec_t, glast_r = ([] for _ in range(7))
    tail = hist[...]
    for n in range(nc):
        y_n, tail = _causal_conv_silu(tail, _col_tiles(h_ref, n, 0, qkv_tiles), convw_ref)
        qkv.append(y_n)
        gates = h_ref[n, z_tile0 + DN_V // LANES]
        beta_t.append(jax.nn.sigmoid(gates))
        g_t = -(jnp.exp(alog_ref[...]) * jax.nn.softplus(gates + dtb_ref[...]))
        gcum.append(_time_scan(g_t, jnp.add, 0.0))
        gcum_t.append(gcum[n].T)
        g_end = gcum[n][c - 1:c, :]
        expg_t.append(jnp.exp(gcum[n]))
        kdec_t.append(jnp.exp(g_end - gcum[n]))
        glast_r.append(jnp.exp(g_end))
    hist[...] = tail

    q = [qkv[n][:, h * DN_DK:(h + 1) * DN_DK] for n, h in units]
    k = [qkv[n][:, DN_QK + h * DN_DK:DN_QK + (h + 1) * DN_DK] for n, h in units]
    v = [qkv[n][:, 2 * DN_QK + h * DN_DV:2 * DN_QK + (h + 1) * DN_DV] for n, h in units]
    q = [x * (lax.rsqrt(jnp.sum(x * x, -1, keepdims=True) + 1e-6) * DN_DK ** -0.5) for x in q]
    k = [x * lax.rsqrt(jnp.sum(x * x, -1, keepdims=True) + 1e-6) for x in k]
    beta = [beta_t[n][:, h:h + 1] for n, h in units]
    g_col = [gcum[n][:, DN_HEADS + h:DN_HEADS + h + 1] for n, h in units]
    g_row = [gcum_t[n][DN_HEADS + h:DN_HEADS + h + 1, :] for n, h in units]
    exp_g = [expg_t[n][:, DN_HEADS + h:DN_HEADS + h + 1] for n, h in units]
    k_beta = [k[i] * beta[i] for i in ids]
    zeros = jnp.zeros((c, DN_DK), F32)
    m2, decay2 = [], []
    for i0 in range(0, len(units), 2):
        i1 = i0 + 1
        lhs = jnp.concatenate([jnp.concatenate([k_beta[i0], k_beta[i1]], axis=1),
                               jnp.concatenate([q[i0], q[i1]], axis=1)], axis=0)
        rhs = jnp.concatenate([jnp.concatenate([k[i0], zeros], axis=1),
                               jnp.concatenate([zeros, k[i1]], axis=1)], axis=0)
        m2.append(_dot_nt(lhs, rhs))
        g_col2 = jnp.where(in_a, g_col[i0], g_col[i1])
        g_row2 = jnp.concatenate([g_row[i0], g_row[i1]], axis=1)
        decay2.append(jnp.exp(jnp.where(causal2, g_col2 - g_row2, -jnp.inf)))
    a2 = [jnp.where(strict2, m[:c] * d, 0.0) for m, d in zip(m2, decay2)]
    qk2 = [m[c:] * d for m, d in zip(m2, decay2)]
    t2 = _unit_lower_inverse_pairs(a2, eye2, masks2, in_a)
    t = [t2[i // 2][:, (i % 2) * c:(i % 2 + 1) * c] for i in ids]
    qk = [qk2[i // 2][:, (i % 2) * c:(i % 2 + 1) * c] for i in ids]
    uw = [_dot(t[i], jnp.concatenate([v[i] * beta[i], k_beta[i] * exp_g[i]], axis=1)) for i in ids]
    q_dec = [q[i] * exp_g[i] for i in ids]
    k_dec = [k[i] * kdec_t[n][:, DN_HEADS + h:DN_HEADS + h + 1] for i, (n, h) in enumerate(units)]
    s_decay = [glast_r[n][:, DN_HEADS + h:DN_HEADS + h + 1] for n, h in units]
    s_cur = [state[h] for h in heads]
    o = []
    for n in range(nc):
        idn = [n * DN_HEADS + h for h in heads]
        ws = [_dot(jnp.concatenate([uw[i][:, DN_DV:], q_dec[i]], axis=0), s_cur[h]) for h, i in zip(heads, idn)]
        v_new = [uw[i][:, :DN_DV] - ws[h][:c] for h, i in zip(heads, idn)]
        o += [ws[h][c:] + _dot(qk[i], v_new[h]) for h, i in zip(heads, idn)]
        s_cur = [s_cur[h] * s_decay[i] + _dot_tn(k_dec[i], v_new[h]) for h, i in zip(heads, idn)]
    for h in heads:
        state[h] = s_cur[h]
    for i, (n, h) in enumerate(units):
        z = h_ref[n, z_tile0 + h]
        on = o[i] * lax.rsqrt(jnp.mean(o[i] * o[i], -1, keepdims=True) + 1e-6) * normw_ref[...]
        _store_natural_order(y_ref, n, h, on * _silu(z))


def _dn_mixer(h, b, nc, conv_w, alog_row, dtb_row, norm_w):
    nch = h.shape[0] // b // nc
    ncols = 2 * DN_QK + DN_V
    return pl.pallas_call(
        _dn_kernel,
        grid=(b, nch),
        in_specs=[pl.BlockSpec((nc,) + h.shape[1:], lambda i, j: (i * nch + j, 0, 0, 0)),
                  pl.BlockSpec(conv_w.shape, lambda i, j: (0, 0)),
                  pl.BlockSpec(alog_row.shape, lambda i, j: (0, 0)),
                  pl.BlockSpec(dtb_row.shape, lambda i, j: (0, 0)),
                  pl.BlockSpec(norm_w.shape, lambda i, j: (0, 0))],
        out_specs=pl.BlockSpec((nc, DN_V // LANES, CHUNK, LANES), lambda i, j: (i * nch + j, 0, 0, 0)),
        out_shape=jax.ShapeDtypeStruct((h.shape[0], DN_V // LANES, CHUNK, LANES), F32),
        scratch_shapes=[pltpu.VMEM(((CONV_K - 1) * SUBLANES, ncols), F32),
                        pltpu.VMEM((DN_HEADS, DN_DK, DN_DV), F32)],
        compiler_params=pltpu.CompilerParams(dimension_semantics=("parallel", "arbitrary"),
                                             vmem_limit_bytes=VMEM_LIMIT),
        name="dn_mixer",
    )(h, conv_w, alog_row, dtb_row, norm_w)


def _cumsum_natural_time(x):
    row = lax.broadcasted_iota(jnp.int32, x.shape, 0)
    d = 1
    while d < CHUNK:
        if d < SUBLANES:
            shifted = jnp.where(row < d, 0.0, pltpu.roll(x, d, 0))
        else:
            shifted = jnp.concatenate([jnp.zeros((d, x.shape[1]), F32), x[:CHUNK - d, :]], axis=0)
        x = x + shifted
        d *= 2
    return x


def _rw_kernel(h_ref, wup_ref, aup_ref, w0_ref, a0_ref, kk_ref, ka_ref, rk_ref, gnw_ref, gnb_ref,
               y_ref, state):
    c = CHUNK
    nc = y_ref.shape[0]
    pairs = RW_W // LANES

    @pl.when(pl.program_id(1) == 0)
    def _():
        state[...] = jnp.zeros(state.shape, F32)

    in_a, col2 = _pair_lane_masks(c)
    row2 = lax.broadcasted_iota(jnp.int32, (c, LANES), 0)
    causal2 = row2 >= col2
    strict2 = row2 > col2
    eye2 = (row2 == col2).astype(F32)
    masks2 = _doubling_masks(row2, col2, c)
    row_big = lax.broadcasted_iota(jnp.int32, (LANES, LANES), 0)
    col_big = lax.broadcasted_iota(jnp.int32, (LANES, LANES), 1)
    same_head = (row_big < RW_HEAD) == (col_big < RW_HEAD)

    row_bd = functools.partial(_row_block_diag, in_a=in_a)

    def half_sums(x):
        sum_a = jnp.sum(jnp.where(in_a, x, 0.0), -1, keepdims=True)
        sum_b = jnp.sum(jnp.where(in_a, 0.0, x), -1, keepdims=True)
        return jnp.where(in_a, sum_a, sum_b)

    units = [(n, p) for n in range(nc) for p in range(pairs)]
    ids = range(len(units))
    tiles = [slice(p * LANES, (p + 1) * LANES) for p in range(pairs)]

    r_all, v_all, z_all, a_all, k2_all, rkr_all, kku_all = ([] for _ in range(7))
    p_incl, p_excl, p_inv, p_tail, p_last = ([] for _ in range(5))
    for n in range(nc):
        rows = slice(n * c, (n + 1) * c)
        r_all.append(h_ref[0, rows, 0:RW_W])
        k_n = h_ref[0, rows, RW_W:2 * RW_W]
        v_all.append(h_ref[0, rows, 2 * RW_W:3 * RW_W])
        z_all.append(h_ref[0, rows, 3 * RW_W:4 * RW_W])
        lo = h_ref[0, rows, 4 * RW_W:4 * RW_W + LANES]
        w_log = -jax.nn.softplus(-(w0_ref[...] + _dot(jnp.tanh(lo[:, :RW_LORA]), wup_ref[...]))) - 0.5
        a_all.append(jax.nn.sigmoid(a0_ref[...] + _dot(lo[:, RW_LORA:], aup_ref[...])))
        logd = -jnp.exp(w_log)
        lcum = _cumsum_natural_time(logd)
        l_last = lcum[c - 1:c, :]
        p_incl.append(jnp.exp(lcum))
        p_excl.append(jnp.exp(lcum - logd))
        p_inv.append(jnp.exp(-lcum))
        p_tail.append(jnp.exp(l_last - lcum))
        p_last.append(jnp.exp(l_last))
        kku_all.append(k_n * kk_ref[...])
        k2_all.append(k_n * (1.0 + (a_all[n] - 1.0) * ka_ref[...]))
        rkr_all.append(r_all[n] * k2_all[n] * rk_ref[...])

    v = [v_all[n][:, tiles[p]] for n, p in units]
    k2 = [k2_all[n][:, tiles[p]] for n, p in units]
    kku = [kku_all[n][:, tiles[p]] for n, p in units]
    kk = [kku[i] * lax.rsqrt(half_sums(kku[i] * kku[i]) + 1e-6) for i in ids]
    alpha = [-(kk[i] * a_all[n][:, tiles[p]]) for i, (n, p) in enumerate(units)]
    lhs = [jnp.concatenate([kk[i] * p_excl[n][:, tiles[p]], r_all[n][:, tiles[p]] * p_incl[n][:, tiles[p]]], axis=0)
           for i, (n, p) in enumerate(units)]
    rhs = [jnp.concatenate([row_bd(alpha[i] * p_inv[n][:, tiles[p]]), row_bd(k2[i] * p_inv[n][:, tiles[p]])], axis=0)
           for i, (n, p) in enumerate(units)]
    m = [_dot_nt(lhs[i], rhs[i]) for i in ids]
    a_ab = [jnp.where(strict2, -x[:c, :LANES], 0.0) for x in m]
    a_ak = [jnp.where(strict2, x[:c, LANES:], 0.0) for x in m]
    a_r = [jnp.concatenate([jnp.where(causal2, x[c:, :LANES], 0.0), jnp.where(causal2, x[c:, LANES:], 0.0)], axis=1)
           for x in m]
    t = _unit_lower_inverse_pairs(a_ab, eye2, masks2, in_a)
    v_bd = [row_bd(x) for x in v]
    akv = [_dot(a_ak[i], v_bd[i]) for i in ids]
    tail = [jnp.concatenate([alpha[i] * p_tail[n][:, tiles[p]], k2[i] * p_tail[n][:, tiles[p]]], axis=0)
            for i, (n, p) in enumerate(units)]

    s_cur = [state[p] for p in range(pairs)]
    y = []
    for n in range(nc):
        idn = [n * pairs + p for p in range(pairs)]
        lhs_s = [_dot_nt(lhs[i], s_cur[p]) for p, i in enumerate(idn)]
        u = [_dot(t[i], row_bd(lhs_s[p][:c] + akv[i])) for p, i in enumerate(idn)]
        y += [lhs_s[p][c:] + _dot(a_r[i], jnp.concatenate([row_bd(u[p]), v_bd[i]], axis=0))
              for p, i in enumerate(idn)]
        s_cur = [s_cur[p] * p_last[n][:, tiles[p]]
                 + jnp.where(same_head, _dot_tn(jnp.concatenate([u[p], v[i]], axis=0), tail[i]), 0.0)
                 for p, i in enumerate(idn)]
    for p in range(pairs):
        state[p] = s_cur[p]
    inv_d = 1.0 / RW_HEAD
    for i, (n, p) in enumerate(units):
        mu = half_sums(y[i]) * inv_d
        yc = y[i] - mu
        var = half_sums(yc * yc) * inv_d
        yn = yc * lax.rsqrt(var + RW_GN_EPS) * gnw_ref[:, tiles[p]] + gnb_ref[:, tiles[p]]
        bonus = half_sums(rkr_all[n][:, tiles[p]]) * v[i]
        y_ref[n, p] = (yn + bonus) * _silu(z_all[n][:, tiles[p]])


def _rw_mixer(h, nc, wup, aup, w0, a0, k_k, k_a, r_k, gn_w, gn_b):
    b, s, n = h.shape
    nch = s // CHUNK // nc
    full = lambda arr: pl.BlockSpec(arr.shape, lambda i, j: (0,) * arr.ndim)
    params = (wup, aup, w0, a0, k_k, k_a, r_k, gn_w, gn_b)
    return pl.pallas_call(
        _rw_kernel,
        grid=(b, nch),
        in_specs=[pl.BlockSpec((1, nc * CHUNK, n), lambda i, j: (i, j, 0))] + [full(a) for a in params],
        out_specs=pl.BlockSpec((nc, RW_W // LANES, CHUNK, LANES), lambda i, j: (i * nch + j, 0, 0, 0)),
        out_shape=jax.ShapeDtypeStruct((b * nch * nc, RW_W // LANES, CHUNK, LANES), F32),
        scratch_shapes=[pltpu.VMEM((RW_W // LANES, LANES, LANES), F32)],
        compiler_params=pltpu.CompilerParams(dimension_semantics=("parallel", "arbitrary"),
                                             vmem_limit_bytes=VMEM_LIMIT),
        name="rw_mixer",
    )(h, *params)


def _time_shift(x, d, fill):
    if d < SUBLANES:
        last = x[CHUNK - d * SUBLANES:, :]
        sub = lax.broadcasted_iota(jnp.int32, last.shape, 0) & (SUBLANES - 1)
        rolled = jnp.concatenate([pltpu.roll(last[g * SUBLANES:(g + 1) * SUBLANES, :], 1, 0) for g in range(d)], axis=0)
        return jnp.concatenate([jnp.where(sub == 0, fill, rolled), x[:CHUNK - d * SUBLANES, :]], axis=0)
    k = d // SUBLANES
    sub = lax.broadcasted_iota(jnp.int32, x.shape, 0) & (SUBLANES - 1)
    rolled = jnp.concatenate([pltpu.roll(x[g * SUBLANES:(g + 1) * SUBLANES, :], k, 0)
                              for g in range(CHUNK // SUBLANES)], axis=0)
    return jnp.where(sub < k, fill, rolled)


def _time_scan(x, op, identity):
    d = 1
    while d < CHUNK:
        x = op(x, _time_shift(x, d, identity))
        d *= 2
    return x


def _ml_kernel(h_ref, convw_ref, ib_ref, fb_ref, gnw_ref, y_ref, hist, cstate, mstate):
    c = CHUNK
    nc = h_ref.shape[0]

    @pl.when(pl.program_id(1) == 0)
    def _():
        hist[...] = jnp.zeros(hist.shape, F32)
        cstate[...] = jnp.zeros(cstate.shape, F32)
        mstate[...] = jnp.full(mstate.shape, -jnp.inf, F32)

    qk_tiles = 2 * ML_QK // LANES
    v_tiles = ML_V // LANES
    _, row, col = _chunk_iota(c, permuted=True)
    causal = row >= col
    heads = range(ML_HEADS)
    units = [(n, h) for n in range(nc) for h in heads]
    ids = range(len(units))

    qk_all, col_part, key_t, inter_t, eneg_t, kw_t, carry_r = ([] for _ in range(7))
    tail = hist[...]
    m_row = mstate[0:1, :]
    for n in range(nc):
        y_n, tail = _causal_conv_silu(tail, _col_tiles(h_ref, n, 0, qk_tiles), convw_ref)
        qk_all.append(y_n)
        gates = h_ref[n, qk_tiles + 3 * v_tiles]
        i_t = gates + ib_ref[...]
        f_t = pltpu.roll(jax.nn.log_sigmoid(gates + fb_ref[...]), LANES - ML_HEADS, 1)
        b = _time_scan(f_t, jnp.add, 0.0)
        key = i_t - b
        m_intra = b + _time_scan(key, jnp.maximum, -jnp.inf)
        m_state = m_row + b
        m_t = jnp.maximum(m_state, m_intra)
        m_new = m_t[c - 1:c, :]
        b_last = b[c - 1:c, :]
        inter_t.append(jnp.exp(m_state - m_t))
        eneg_t.append(jnp.exp(-m_t))
        carry_r.append(jnp.exp(m_row + b_last - m_new))
        kw_t.append(jnp.exp(key + b_last - m_new))
        col_part.append(b - m_t)
        key_t.append(key.T)
        m_row = m_new
    hist[...] = tail
    mstate[...] = jnp.broadcast_to(m_row, mstate.shape)

    q = [qk_all[n][:, h * ML_DQK:(h + 1) * ML_DQK] for n, h in units]
    k = [qk_all[n][:, ML_QK + h * ML_DQK:ML_QK + (h + 1) * ML_DQK] * ML_DQK ** -0.5 for n, h in units]
    v = [h_ref[n, qk_tiles + h] for n, h in units]
    qk = [_dot_nt(q[i], k[i]) for i in ids]
    w_qk = [jnp.exp(jnp.where(causal, col_part[n][:, h:h + 1] + key_t[n][h:h + 1, :], -jnp.inf)) * qk[i]
            for i, (n, h) in enumerate(units)]
    k_w = [k[i] * kw_t[n][:, h:h + 1] for i, (n, h) in enumerate(units)]
    ones = jnp.ones((c, LANES), F32)
    v_ext = [jnp.concatenate([x, ones], axis=1) for x in v]
    kv = [_dot_tn(k_w[i], v_ext[i]) for i in ids]

    c_in = []
    c_cur = [cstate[h] for h in heads]
    for i, (n, h) in enumerate(units):
        c_in.append(c_cur[h])
        c_cur[h] = c_cur[h] * carry_r[n][:, h:h + 1] + kv[i]
    for h in heads:
        cstate[h] = c_cur[h]

    nd = [inter_t[n][:, h:h + 1] * _dot(q[i], c_in[i]) + _dot(w_qk[i], v_ext[i])
          for i, (n, h) in enumerate(units)]
    h_tilde = [nd[i][:, :ML_DV] / jnp.maximum(jnp.abs(nd[i][:, ML_DV:]), eneg_t[n][:, h:h + 1])
               for i, (n, h) in enumerate(units)]
    for i, (n, h) in enumerate(units):
        og = h_ref[n, qk_tiles + v_tiles + h]
        z = h_ref[n, qk_tiles + 2 * v_tiles + h]
        xg = jax.nn.sigmoid(og) * h_tilde[i]
        mu = jnp.mean(xg, -1, keepdims=True)
        var = jnp.mean(jnp.square(xg - mu), -1, keepdims=True)
        xn = (xg - mu) * lax.rsqrt(var + 1e-6) * gnw_ref[:, h * ML_DV:(h + 1) * ML_DV]
        _store_natural_order(y_ref, n, h, xn * _silu(z))


def _ml_mixer(h, b, nc, conv_w, ib_row, fb_row, gn_w):
    nch = h.shape[0] // b // nc
    return pl.pallas_call(
        _ml_kernel,
        grid=(b, nch),
        in_specs=[pl.BlockSpec((nc,) + h.shape[1:], lambda i, j: (i * nch + j, 0, 0, 0)),
                  pl.BlockSpec(conv_w.shape, lambda i, j: (0, 0)),
                  pl.BlockSpec(ib_row.shape, lambda i, j: (0, 0)),
                  pl.BlockSpec(fb_row.shape, lambda i, j: (0, 0)),
                  pl.BlockSpec(gn_w.shape, lambda i, j: (0, 0))],
        out_specs=pl.BlockSpec((nc, ML_V // LANES, CHUNK, LANES), lambda i, j: (i * nch + j, 0, 0, 0)),
        out_shape=jax.ShapeDtypeStruct((h.shape[0], ML_V // LANES, CHUNK, LANES), F32),
        scratch_shapes=[pltpu.VMEM(((CONV_K - 1) * SUBLANES, 2 * ML_QK), F32),
                        pltpu.VMEM((ML_HEADS, ML_DQK, ML_DV + LANES), F32),
                        pltpu.VMEM((SUBLANES, LANES), F32)],
        compiler_params=pltpu.CompilerParams(dimension_semantics=("parallel", "arbitrary"),
                                             vmem_limit_bytes=VMEM_LIMIT),
        name="ml_mixer",
    )(h, conv_w, ib_row, fb_row, gn_w)


def _post_kernel(x_ref, y_ref, p_ref, wout_ref, wg_ref, wp_ref, lng_ref, lnb_ref, pnw_ref, o_ref):
    y = jnp.concatenate([jnp.concatenate([y_ref[ch, t] for t in range(y_ref.shape[1])], axis=1)
                         for ch in range(y_ref.shape[0])], axis=0)
    r = DEEPNORM_ALPHA * x_ref[...] + _dot(y, wout_ref[...])
    mu = jnp.mean(r, -1, keepdims=True)
    var = jnp.mean(jnp.square(r - mu), -1, keepdims=True)
    xn = (r - mu) * lax.rsqrt(var + LN_EPS) * lng_ref[...] + lnb_ref[...]
    gate = jax.nn.sigmoid(_dot(xn, wg_ref[...]))
    pp = _dot(p_ref[...], wp_ref[...])
    pn = pp * lax.rsqrt(jnp.mean(pp * pp, -1, keepdims=True) + 1e-6) * pnw_ref[...]
    o_ref[...] = xn + gate * pn


def _post(x2d, y4d, p3d, layer, w_out, w_gate, w_proj, ln_g, ln_b, pn_w, tm):
    m, d = x2d.shape
    p_spec = pl.BlockSpec((None, tm, p3d.shape[2]), lambda i: (layer, i, 0))
    y_spec = pl.BlockSpec((tm // CHUNK,) + y4d.shape[1:], lambda i: (i, 0, 0, 0))
    tile = lambda arr: pl.BlockSpec((tm, arr.shape[1]), lambda i: (i, 0))
    full = lambda arr: pl.BlockSpec(arr.shape, lambda i: (0, 0))
    params = (w_out, w_gate, w_proj, ln_g, ln_b, pn_w)
    return pl.pallas_call(
        _post_kernel,
        grid=(m // tm,),
        in_specs=[tile(x2d), y_spec, p_spec] + [full(a) for a in params],
        out_specs=pl.BlockSpec((tm, d), lambda i: (i, 0)),
        out_shape=jax.ShapeDtypeStruct((m, d), F32),
        compiler_params=pltpu.CompilerParams(dimension_semantics=("parallel",),
                                             vmem_limit_bytes=VMEM_LIMIT),
        name="post_block",
    )(x2d, y4d, p3d, *params)


def _pad_cols(w, n):
    return jnp.pad(w, ((0, 0), (0, n - w.shape[1])))


def _lane_row(vec, offset):
    return jnp.zeros((1, LANES), F32).at[0, offset:offset + vec.shape[0]].set(vec.astype(F32))


def _row_tile(m, cap):
    t = cap
    while t >= CHUNK:
        if m % t == 0:
            return t
        t //= 2
    raise ValueError(f"token count {m} must be a multiple of {CHUNK}")


def kernel(x, p, ln_g, ln_b, ple_w_proj, ple_norm_w, ple_w_gate, dn_w_in, dn_conv_w, dn_a_log, dn_dt_bias, dn_norm_w, dn_w_out, rw_w_in, rw_mu, rw_w0, rw_w_lora_up, rw_a0, rw_a_lora_up, rw_k_k, rw_k_a, rw_r_k, rw_gn_w, rw_gn_b, rw_w_out, ml_w_in, ml_conv_w, ml_i_bias, ml_f_bias, ml_gn_w, ml_w_out):
    b, s, d = x.shape
    assert d == D_MODEL and s % CHUNK == 0
    m = b * s
    tm_proj = _row_tile(m, PROJ_ROWS)
    tm_post = _row_tile(m, POST_ROWS)
    tm_rw = _row_tile(s, PROJ_ROWS)
    chunks_per_step = lambda want: next(n for n in (want, 2, 1) if (s // CHUNK) % n == 0)
    bf = lambda w: w.astype(MXU_DTYPE)
    row = lambda v: v.reshape(1, -1).astype(F32)
    x2d = x.reshape(m, d)
    p3d = p.reshape(DEPTH, m, D_PLE)
    for i in range(DEPTH):
        kind, j = i % 3, i // 3
        if kind == 0:
            n_pad = 2 * DN_QK + 2 * DN_V + LANES
            h = _project(x2d, bf(_pad_cols(dn_w_in[j], n_pad)), tm_proj)
            y = _dn_mixer(h, b, chunks_per_step(DN_CHUNKS), dn_conv_w[j], _lane_row(dn_a_log[j], DN_HEADS),
                          _lane_row(dn_dt_bias[j], DN_HEADS), row(dn_norm_w[j]))
            w_out = dn_w_out[j]
        elif kind == 1:
            r_w, wl_w, k_w, v_w, al_w, z_w = jnp.split(
                rw_w_in[j], np.cumsum([RW_W, RW_LORA, RW_W, RW_W, RW_LORA]).tolist(), axis=1)
            w_cat = jnp.concatenate([r_w, k_w, v_w, z_w, wl_w, al_w], axis=1)
            mu = rw_mu[j]
            mu_cat = jnp.stack([mu[0], mu[2], mu[3], mu[5], mu[1], mu[4], mu[0], mu[0]], axis=0)
            h = _rw_project(x2d.reshape(b, s, d), bf(w_cat), mu_cat, tm_rw)
            y = _rw_mixer(h, chunks_per_step(RW_CHUNKS), bf(rw_w_lora_up[j]), bf(rw_a_lora_up[j]), row(rw_w0[j]),
                          row(rw_a0[j]), row(rw_k_k[j]), row(rw_k_a[j]), row(rw_r_k[j]), row(rw_gn_w[j]),
                          row(rw_gn_b[j]))
            w_out = rw_w_out[j]
        else:
            n_pad = 2 * ML_QK + 3 * ML_V + LANES
            h = _project(x2d, bf(_pad_cols(ml_w_in[j], n_pad)), tm_proj)
            y = _ml_mixer(h, b, chunks_per_step(ML_CHUNKS), ml_conv_w[j], _lane_row(ml_i_bias[j], 0),
                          _lane_row(ml_f_bias[j], ML_HEADS), row(ml_gn_w[j]))
            w_out = ml_w_out[j]
        x2d = _post(x2d, y, p3d, i, bf(w_out), bf(ple_w_gate[i]),
                    bf(ple_w_proj[i]), row(ln_g[i]), row(ln_b[i]), row(ple_norm_w[i]), tm_post)
    return x2d.reshape(b, s, d)
```

```python
import functools

import jax
import jax.numpy as jnp
import numpy as np
from jax import lax
from jax.experimental import pallas as pl
from jax.experimental.pallas import tpu as pltpu

F32 = jnp.float32
MXU_DTYPE = jnp.bfloat16

LANES = 128
SUBLANES = 8
VMEM_LIMIT = 48 * 1024 * 1024

DEPTH = 4
D_MODEL = 1024
D_PLE = 256
CONV_K = 4
CHUNK = 64
LN_EPS = 1e-5
DN_HEADS, DN_DK, DN_DV = 8, 128, 128
DN_QK = DN_HEADS * DN_DK
DN_V = DN_HEADS * DN_DV
RW_HEAD = 64
RW_HEADS = D_MODEL // RW_HEAD
RW_W = RW_HEADS * RW_HEAD
RW_LORA = 64
RW_GN_EPS = 64e-5
ML_HEADS, ML_DQK, ML_DV = 8, 64, 128
ML_QK = ML_HEADS * ML_DQK
ML_V = ML_HEADS * ML_DV
DEEPNORM_ALPHA = (2.0 * DEPTH) ** 0.25
HIST = SUBLANES
DN_CHUNKS = 4
RW_CHUNKS = 4
ML_CHUNKS = 1


def _dot(a, b):
    return lax.dot_general(a.astype(MXU_DTYPE), b.astype(MXU_DTYPE), (((1,), (0,)), ((), ())),
                           preferred_element_type=F32)


def _dot_nt(a, b):
    return lax.dot_general(a.astype(MXU_DTYPE), b.astype(MXU_DTYPE), (((1,), (1,)), ((), ())),
                           preferred_element_type=F32)


def _dot_tn(a, b):
    return lax.dot_general(a.astype(MXU_DTYPE), b.astype(MXU_DTYPE), (((0,), (0,)), ((), ())),
                           preferred_element_type=F32)


def _silu(x):
    return x * jax.nn.sigmoid(x)


def _perm_time(p):
    return ((p & (SUBLANES - 1)) << 3) | (p >> 3)


def _chunk_iota(n, permuted=False):
    row = lax.broadcasted_iota(jnp.int32, (n, n), 0)
    col = lax.broadcasted_iota(jnp.int32, (n, n), 1)
    eye = (row == col).astype(F32)
    if permuted:
        row, col = _perm_time(row), _perm_time(col)
    return eye, row, col


def _doubling_masks(row, col, n):
    masks = []
    s, shift = 1, 0
    while s < n:
        same = (row >> (shift + 1)) == (col >> (shift + 1))
        masks.append(same & ((row & s) != 0) & ((col & s) == 0))
        s, shift = 2 * s, shift + 1
    return masks


HALF = LANES // 2


def _pair_lane_masks(rows):
    lane = lax.broadcasted_iota(jnp.int32, (rows, LANES), 1)
    return lane < HALF, lane & (HALF - 1)


def _row_block_diag(x, in_a):
    return jnp.concatenate([jnp.where(in_a, x, 0.0), jnp.where(in_a, 0.0, x)], axis=0)


def _unit_lower_inverse_pairs(a_list, eye2, masks2, in_a):
    t = [eye2 - jnp.where(masks2[0], a, 0.0) for a in a_list]
    for m in masks2[1:]:
        ta = [_dot(ti, _row_block_diag(jnp.where(m, a, 0.0), in_a)) for ti, a in zip(t, a_list)]
        t = [ti - _dot(tai, _row_block_diag(ti, in_a)) for ti, tai in zip(t, ta)]
    return t


def _load_permuted(ref, chunk, tile):
    return jnp.concatenate([ref[chunk, tile, pl.ds(b, CHUNK // SUBLANES, stride=SUBLANES), :]
                            for b in range(SUBLANES)], axis=0)


def _col_tiles(ref, chunk, first, count):
    return jnp.concatenate([_load_permuted(ref, chunk, first + i) for i in range(count)], axis=1)


def _causal_conv_silu(prev_tail, x, w_ref):
    keep = CONV_K - 1
    tail = x[CHUNK - keep * SUBLANES:, :]
    sub = lax.broadcasted_iota(jnp.int32, tail.shape, 0) & (SUBLANES - 1)
    mixed = jnp.where(sub == SUBLANES - 1, prev_tail, tail)
    wrapped = jnp.concatenate([pltpu.roll(mixed[i * SUBLANES:(i + 1) * SUBLANES, :], 1, 0) for i in range(keep)],
                              axis=0)
    acc = w_ref[CONV_K - 1:CONV_K, :] * x
    for s in range(1, CONV_K):
        shifted = jnp.concatenate([wrapped[(keep - s) * SUBLANES:, :], x[:CHUNK - s * SUBLANES, :]], axis=0)
        acc = acc + w_ref[CONV_K - 1 - s:CONV_K - s, :] * shifted
    return _silu(acc), tail


def _store_natural_order(y_ref, chunk, tile, y):
    for b in range(SUBLANES):
        y_ref[chunk, tile, pl.ds(b, CHUNK // SUBLANES, stride=SUBLANES), :] = y[b * SUBLANES:(b + 1) * SUBLANES, :]


PROJ_COLS = 2 * LANES
PROJ_ROWS = 512
POST_ROWS = 1024
POST_SUB_ROWS = 256


def _proj_kernel(x_ref, w_ref, o_ref, *, tm):
    x = x_ref[...].astype(MXU_DTYPE)
    n = w_ref.shape[1]
    for c0 in range(0, n, PROJ_COLS):
        width = min(PROJ_COLS, n - c0)
        res = lax.dot_general(x, w_ref[:, c0:c0 + width], (((1,), (0,)), ((), ())), preferred_element_type=F32)
        for ch in range(tm // CHUNK):
            for t in range(width // LANES):
                o_ref[ch, c0 // LANES + t] = res[ch * CHUNK:(ch + 1) * CHUNK, t * LANES:(t + 1) * LANES]


def _project(x2d, w, tm):
    m, k = x2d.shape
    n = w.shape[1]
    return pl.pallas_call(
        functools.partial(_proj_kernel, tm=tm),
        grid=(m // tm,),
        in_specs=[pl.BlockSpec((tm, k), lambda i: (i, 0)),
                  pl.BlockSpec((k, n), lambda i: (0, 0))],
        out_specs=pl.BlockSpec((tm // CHUNK, n // LANES, CHUNK, LANES), lambda i: (i, 0, 0, 0)),
        out_shape=jax.ShapeDtypeStruct((m // CHUNK, n // LANES, CHUNK, LANES), F32),
        compiler_params=pltpu.CompilerParams(dimension_semantics=("parallel",),
                                             vmem_limit_bytes=VMEM_LIMIT),
        name="in_proj",
    )(x2d, w)


def _rw_proj_kernel(x_ref, prev_ref, w_ref, mu_ref, o_ref, buf, *, tm):
    x = x_ref[0]
    buf[HIST:HIST + tm, :] = x
    buf[0:HIST, :] = jnp.where(pl.program_id(1) == 0, 0.0, prev_ref[0])
    dx = buf[pl.ds(HIST - 1, tm), :] - x
    for g in range(4):
        lhs = x + mu_ref[g:g + 1, :] * dx
        o_ref[0, :, g * RW_W:(g + 1) * RW_W] = _dot(lhs, w_ref[:, g * RW_W:(g + 1) * RW_W])
    w_lo = w_ref[:, 4 * RW_W:4 * RW_W + LANES]
    lo_w = _dot(x + mu_ref[4:5, :] * dx, w_lo)
    lo_a = _dot(x + mu_ref[5:6, :] * dx, w_lo)
    lane = lax.broadcasted_iota(jnp.int32, lo_w.shape, 1)
    o_ref[0, :, 4 * RW_W:4 * RW_W + LANES] = jnp.where(lane < RW_LORA, lo_w, lo_a)


def _rw_project(x, w, mu, tm):
    b, s, d = x.shape
    n = w.shape[1]
    per = tm // HIST
    return pl.pallas_call(
        functools.partial(_rw_proj_kernel, tm=tm),
        grid=(b, s // tm),
        in_specs=[pl.BlockSpec((1, tm, d), lambda i, j: (i, j, 0)),
                  pl.BlockSpec((1, HIST, d), lambda i, j: (i, jnp.maximum(j * per - 1, 0), 0)),
                  pl.BlockSpec((d, n), lambda i, j: (0, 0)),
                  pl.BlockSpec(mu.shape, lambda i, j: (0, 0))],
        out_specs=pl.BlockSpec((1, tm, n), lambda i, j: (i, j, 0)),
        out_shape=jax.ShapeDtypeStruct((b, s, n), F32),
        scratch_shapes=[pltpu.VMEM((tm + HIST, d), F32)],
        compiler_params=pltpu.CompilerParams(dimension_semantics=("parallel", "parallel"),
                                             vmem_limit_bytes=VMEM_LIMIT),
        name="rw_in_proj",
    )(x, x, w, mu)


def _dn_kernel(h_ref, convw_ref, alog_ref, dtb_ref, normw_ref, y_ref, hist, state):
    c = CHUNK
    nc = h_ref.shape[0]

    @pl.when(pl.program_id(1) == 0)
    def _():
        hist[...] = jnp.zeros(hist.shape, F32)
        state[...] = jnp.zeros(state.shape, F32)

    qkv_tiles = (2 * DN_QK + DN_V) // LANES
    z_tile0 = qkv_tiles
    in_a, col2 = _pair_lane_masks(c)
    row2 = lax.broadcasted_iota(jnp.int32, (c, LANES), 0)
    eye2 = (row2 == col2).astype(F32)
    row2, col2 = _perm_time(row2), _perm_time(col2)
    causal2 = row2 >= col2
    strict2 = row2 > col2
    masks2 = _doubling_masks(row2, col2, c)

    heads = range(DN_HEADS)
    units = [(n, h) for n in range(nc) for h in heads]
    ids = range(len(units))
    qkv, gcum, gcum_t, beta_t, expg_t, kdec_t, glast_r = ([] for _ in range(7))
    tail = hist[...]
    for n in range(nc):
        y_n, tail = _causal_conv_silu(tail, _col_tiles(h_ref, n, 0, qkv_tiles), convw_ref)
        qkv.append(y_n)
        gates = _load_permuted(h_ref, n, z_tile0 + DN_V // LANES)
        beta_t.append(jax.nn.sigmoid(gates))
        g_t = -(jnp.exp(alog_ref[...]) * jax.nn.softplus(gates + dtb_ref[...]))
        gcum.append(_time_scan(g_t, jnp.add, 0.0))
        gcum_t.append(gcum[n].T)
        g_end = gcum[n][c - 1:c, :]
        expg_t.append(jnp.exp(gcum[n]))
        kdec_t.append(jnp.exp(g_end - gcum[n]))
        glast_r.append(jnp.exp(g_end))
    hist[...] = tail

    q = [qkv[n][:, h * DN_DK:(h + 1) * DN_DK] for n, h in units]
    k = [qkv[n][:, DN_QK + h * DN_DK:DN_QK + (h + 1) * DN_DK] for n, h in units]
    v = [qkv[n][:, 2 * DN_QK + h * DN_DV:2 * DN_QK + (h + 1) * DN_DV] for n, h in units]
    q = [x * (lax.rsqrt(jnp.sum(x * x, -1, keepdims=True) + 1e-6) * DN_DK ** -0.5) for x in q]
    k = [x * lax.rsqrt(jnp.sum(x * x, -1, keepdims=True) + 1e-6) for x in k]
    beta = [beta_t[n][:, h:h + 1] for n, h in units]
    g_col = [gcum[n][:, DN_HEADS + h:DN_HEADS + h + 1] for n, h in units]
    g_row = [gcum_t[n][DN_HEADS + h:DN_HEADS + h + 1, :] for n, h in units]
    exp_g = [expg_t[n][:, DN_HEADS + h:DN_HEADS + h + 1] for n, h in units]
    k_beta = [k[i] * beta[i] for i in ids]
    zeros = jnp.zeros((c, DN_DK), F32)
    m2, decay2 = [], []
    for i0 in range(0, len(units), 2):
        i1 = i0 + 1
        lhs = jnp.concatenate([jnp.concatenate([k_beta[i0], k_beta[i1]], axis=1),
                               jnp.concatenate([q[i0], q[i1]], axis=1)], axis=0)
        rhs = jnp.concatenate([jnp.concatenate([k[i0], zeros], axis=1),
                               jnp.concatenate([zeros, k[i1]], axis=1)], axis=0)
        m2.append(_dot_nt(lhs, rhs))
        g_col2 = jnp.where(in_a, g_col[i0], g_col[i1])
        g_row2 = jnp.concatenate([g_row[i0], g_row[i1]], axis=1)
        decay2.append(jnp.exp(jnp.where(causal2, g_col2 - g_row2, -jnp.inf)))
    a2 = [jnp.where(strict2, m[:c] * d, 0.0) for m, d in zip(m2, decay2)]
    qk2 = [m[c:] * d for m, d in zip(m2, decay2)]
    t2 = _unit_lower_inverse_pairs(a2, eye2, masks2, in_a)
    t = [t2[i // 2][:, (i % 2) * c:(i % 2 + 1) * c] for i in ids]
    qk = [qk2[i // 2][:, (i % 2) * c:(i % 2 + 1) * c] for i in ids]
    uw = [_dot(t[i], jnp.concatenate([v[i] * beta[i], k_beta[i] * exp_g[i]], axis=1)) for i in ids]
    q_dec = [q[i] * exp_g[i] for i in ids]
    k_dec = [k[i] * kdec_t[n][:, DN_HEADS + h:DN_HEADS + h + 1] for i, (n, h) in enumerate(units)]
    s_decay = [glast_r[n][:, DN_HEADS + h:DN_HEADS + h + 1] for n, h in units]
    s_cur = [state[h] for h in heads]
    o = []
    for n in range(nc):
        idn = [n * DN_HEADS + h for h in heads]
        ws = [_dot(jnp.concatenate([uw[i][:, DN_DV:], q_dec[i]], axis=0), s_cur[h]) for h, i in zip(heads, idn)]
        v_new = [uw[i][:, :DN_DV] - ws[h][:c] for h, i in zip(heads, idn)]
        o += [ws[h][c:] + _dot(qk[i], v_new[h]) for h, i in zip(heads, idn)]
        s_cur = [s_cur[h] * s_decay[i] + _dot_tn(k_dec[i], v_new[h]) for h, i in zip(heads, idn)]
    for h in heads:
        state[h] = s_cur[h]
    for i, (n, h) in enumerate(units):
        z = _load_permuted(h_ref, n, z_tile0 + h)
        on = o[i] * lax.rsqrt(jnp.mean(o[i] * o[i], -1, keepdims=True) + 1e-6) * normw_ref[...]
        _store_natural_order(y_ref, n, h, on * _silu(z))


def _dn_mixer(h, b, nc, conv_w, alog_row, dtb_row, norm_w):
    nch = h.shape[0] // b // nc
    ncols = 2 * DN_QK + DN_V
    return pl.pallas_call(
        _dn_kernel,
        grid=(b, nch),
        in_specs=[pl.BlockSpec((nc,) + h.shape[1:], lambda i, j: (i * nch + j, 0, 0, 0)),
                  pl.BlockSpec(conv_w.shape, lambda i, j: (0, 0)),
                  pl.BlockSpec(alog_row.shape, lambda i, j: (0, 0)),
                  pl.BlockSpec(dtb_row.shape, lambda i, j: (0, 0)),
                  pl.BlockSpec(norm_w.shape, lambda i, j: (0, 0))],
        out_specs=pl.BlockSpec((nc, DN_V // LANES, CHUNK, LANES), lambda i, j: (i * nch + j, 0, 0, 0)),
        out_shape=jax.ShapeDtypeStruct((h.shape[0], DN_V // LANES, CHUNK, LANES), F32),
        scratch_shapes=[pltpu.VMEM(((CONV_K - 1) * SUBLANES, ncols), F32),
                        pltpu.VMEM((DN_HEADS, DN_DK, DN_DV), F32)],
        compiler_params=pltpu.CompilerParams(dimension_semantics=("parallel", "arbitrary"),
                                             vmem_limit_bytes=VMEM_LIMIT),
        name="dn_mixer",
    )(h, conv_w, alog_row, dtb_row, norm_w)


def _cumsum_natural_time(x):
    row = lax.broadcasted_iota(jnp.int32, x.shape, 0)
    d = 1
    while d < CHUNK:
        if d < SUBLANES:
            shifted = jnp.where(row < d, 0.0, pltpu.roll(x, d, 0))
        else:
            shifted = jnp.concatenate([jnp.zeros((d, x.shape[1]), F32), x[:CHUNK - d, :]], axis=0)
        x = x + shifted
        d *= 2
    return x


def _rw_kernel(h_ref, wup_ref, aup_ref, w0_ref, a0_ref, kk_ref, ka_ref, rk_ref, gnw_ref, gnb_ref,
               y_ref, state):
    c = CHUNK
    nc = y_ref.shape[0]
    pairs = RW_W // LANES

    @pl.when(pl.program_id(1) == 0)
    def _():
        state[...] = jnp.zeros(state.shape, F32)

    in_a, col2 = _pair_lane_masks(c)
    row2 = lax.broadcasted_iota(jnp.int32, (c, LANES), 0)
    causal2 = row2 >= col2
    strict2 = row2 > col2
    eye2 = (row2 == col2).astype(F32)
    masks2 = _doubling_masks(row2, col2, c)
    row_big = lax.broadcasted_iota(jnp.int32, (LANES, LANES), 0)
    col_big = lax.broadcasted_iota(jnp.int32, (LANES, LANES), 1)
    same_head = (row_big < RW_HEAD) == (col_big < RW_HEAD)

    row_bd = functools.partial(_row_block_diag, in_a=in_a)

    def half_sums(x):
        sum_a = jnp.sum(jnp.where(in_a, x, 0.0), -1, keepdims=True)
        sum_b = jnp.sum(jnp.where(in_a, 0.0, x), -1, keepdims=True)
        return jnp.where(in_a, sum_a, sum_b)

    units = [(n, p) for n in range(nc) for p in range(pairs)]
    ids = range(len(units))
    tiles = [slice(p * LANES, (p + 1) * LANES) for p in range(pairs)]

    r_all, v_all, z_all, a_all, k2_all, rkr_all, kku_all = ([] for _ in range(7))
    p_incl, p_excl, p_inv, p_tail, p_last = ([] for _ in range(5))
    for n in range(nc):
        rows = slice(n * c, (n + 1) * c)
        r_all.append(h_ref[0, rows, 0:RW_W])
        k_n = h_ref[0, rows, RW_W:2 * RW_W]
        v_all.append(h_ref[0, rows, 2 * RW_W:3 * RW_W])
        z_all.append(h_ref[0, rows, 3 * RW_W:4 * RW_W])
        lo = h_ref[0, rows, 4 * RW_W:4 * RW_W + LANES]
        w_log = -jax.nn.softplus(-(w0_ref[...] + _dot(jnp.tanh(lo[:, :RW_LORA]), wup_ref[...]))) - 0.5
        a_all.append(jax.nn.sigmoid(a0_ref[...] + _dot(lo[:, RW_LORA:], aup_ref[...])))
        logd = -jnp.exp(w_log)
        lcum = _cumsum_natural_time(logd)
        l_last = lcum[c - 1:c, :]
        p_incl.append(jnp.exp(lcum))
        p_excl.append(jnp.exp(lcum - logd))
        p_inv.append(jnp.exp(-lcum))
        p_tail.append(jnp.exp(l_last - lcum))
        p_last.append(jnp.exp(l_last))
        kku_all.append(k_n * kk_ref[...])
        k2_all.append(k_n * (1.0 + (a_all[n] - 1.0) * ka_ref[...]))
        rkr_all.append(r_all[n] * k2_all[n] * rk_ref[...])

    v = [v_all[n][:, tiles[p]] for n, p in units]
    k2 = [k2_all[n][:, tiles[p]] for n, p in units]
    kku = [kku_all[n][:, tiles[p]] for n, p in units]
    kk = [kku[i] * lax.rsqrt(half_sums(kku[i] * kku[i]) + 1e-6) for i in ids]
    alpha = [-(kk[i] * a_all[n][:, tiles[p]]) for i, (n, p) in enumerate(units)]
    lhs = [jnp.concatenate([kk[i] * p_excl[n][:, tiles[p]], r_all[n][:, tiles[p]] * p_incl[n][:, tiles[p]]], axis=0)
           for i, (n, p) in enumerate(units)]
    rhs = [jnp.concatenate([row_bd(alpha[i] * p_inv[n][:, tiles[p]]), row_bd(k2[i] * p_inv[n][:, tiles[p]])], axis=0)
           for i, (n, p) in enumerate(units)]
    m = [_dot_nt(lhs[i], rhs[i]) for i in ids]
    a_ab = [jnp.where(strict2, -x[:c, :LANES], 0.0) for x in m]
    a_ak = [jnp.where(strict2, x[:c, LANES:], 0.0) for x in m]
    a_r = [jnp.concatenate([jnp.where(causal2, x[c:, :LANES], 0.0), jnp.where(causal2, x[c:, LANES:], 0.0)], axis=1)
           for x in m]
    t = _unit_lower_inverse_pairs(a_ab, eye2, masks2, in_a)
    v_bd = [row_bd(x) for x in v]
    akv = [_dot(a_ak[i], v_bd[i]) for i in ids]
    tail = [jnp.concatenate([alpha[i] * p_tail[n][:, tiles[p]], k2[i] * p_tail[n][:, tiles[p]]], axis=0)
            for i, (n, p) in enumerate(units)]

    s_cur = [state[p] for p in range(pairs)]
    y = []
    for n in range(nc):
        idn = [n * pairs + p for p in range(pairs)]
        lhs_s = [_dot_nt(lhs[i], s_cur[p]) for p, i in enumerate(idn)]
        u = [_dot(t[i], row_bd(lhs_s[p][:c] + akv[i])) for p, i in enumerate(idn)]
        y += [lhs_s[p][c:] + _dot(a_r[i], jnp.concatenate([row_bd(u[p]), v_bd[i]], axis=0))
              for p, i in enumerate(idn)]
        s_cur = [s_cur[p] * p_last[n][:, tiles[p]]
                 + jnp.where(same_head, _dot_tn(jnp.concatenate([u[p], v[i]], axis=0), tail[i]), 0.0)
                 for p, i in enumerate(idn)]
    for p in range(pairs):
        state[p] = s_cur[p]
    inv_d = 1.0 / RW_HEAD
    for i, (n, p) in enumerate(units):
        mu = half_sums(y[i]) * inv_d
        yc = y[i] - mu
        var = half_sums(yc * yc) * inv_d
        yn = yc * lax.rsqrt(var + RW_GN_EPS) * gnw_ref[:, tiles[p]] + gnb_ref[:, tiles[p]]
        bonus = half_sums(rkr_all[n][:, tiles[p]]) * v[i]
        y_ref[n, p] = (yn + bonus) * _silu(z_all[n][:, tiles[p]])


def _rw_mixer(h, nc, wup, aup, w0, a0, k_k, k_a, r_k, gn_w, gn_b):
    b, s, n = h.shape
    nch = s // CHUNK // nc
    full = lambda arr: pl.BlockSpec(arr.shape, lambda i, j: (0,) * arr.ndim)
    params = (wup, aup, w0, a0, k_k, k_a, r_k, gn_w, gn_b)
    return pl.pallas_call(
        _rw_kernel,
        grid=(b, nch),
        in_specs=[pl.BlockSpec((1, nc * CHUNK, n), lambda i, j: (i, j, 0))] + [full(a) for a in params],
        out_specs=pl.BlockSpec((nc, RW_W // LANES, CHUNK, LANES), lambda i, j: (i * nch + j, 0, 0, 0)),
        out_shape=jax.ShapeDtypeStruct((b * nch * nc, RW_W // LANES, CHUNK, LANES), F32),
        scratch_shapes=[pltpu.VMEM((RW_W // LANES, LANES, LANES), F32)],
        compiler_params=pltpu.CompilerParams(dimension_semantics=("parallel", "arbitrary"),
                                             vmem_limit_bytes=VMEM_LIMIT),
        name="rw_mixer",
    )(h, *params)


def _time_shift(x, d, fill):
    if d < SUBLANES:
        last = x[CHUNK - d * SUBLANES:, :]
        sub = lax.broadcasted_iota(jnp.int32, last.shape, 0) & (SUBLANES - 1)
        rolled = jnp.concatenate([pltpu.roll(last[g * SUBLANES:(g + 1) * SUBLANES, :], 1, 0) for g in range(d)], axis=0)
        return jnp.concatenate([jnp.where(sub == 0, fill, rolled), x[:CHUNK - d * SUBLANES, :]], axis=0)
    k = d // SUBLANES
    sub = lax.broadcasted_iota(jnp.int32, x.shape, 0) & (SUBLANES - 1)
    rolled = jnp.concatenate([pltpu.roll(x[g * SUBLANES:(g + 1) * SUBLANES, :], k, 0)
                              for g in range(CHUNK // SUBLANES)], axis=0)
    return jnp.where(sub < k, fill, rolled)


def _time_scan(x, op, identity):
    d = 1
    while d < CHUNK:
        x = op(x, _time_shift(x, d, identity))
        d *= 2
    return x


def _ml_kernel(h_ref, convw_ref, ib_ref, fb_ref, gnw_ref, y_ref, hist, cstate, mstate):
    c = CHUNK
    nc = h_ref.shape[0]

    @pl.when(pl.program_id(1) == 0)
    def _():
        hist[...] = jnp.zeros(hist.shape, F32)
        cstate[...] = jnp.zeros(cstate.shape, F32)
        mstate[...] = jnp.full(mstate.shape, -jnp.inf, F32)

    qk_tiles = 2 * ML_QK // LANES
    v_tiles = ML_V // LANES
    _, row, col = _chunk_iota(c, permuted=True)
    causal = row >= col
    heads = range(ML_HEADS)
    units = [(n, h) for n in range(nc) for h in heads]
    ids = range(len(units))

    qk_all, col_part, key_t, inter_t, eneg_t, kw_t, carry_r = ([] for _ in range(7))
    tail = hist[...]
    m_row = mstate[0:1, :]
    for n in range(nc):
        y_n, tail = _causal_conv_silu(tail, _col_tiles(h_ref, n, 0, qk_tiles), convw_ref)
        qk_all.append(y_n)
        gates = _load_permuted(h_ref, n, qk_tiles + 3 * v_tiles)
        i_t = gates + ib_ref[...]
        f_t = pltpu.roll(jax.nn.log_sigmoid(gates + fb_ref[...]), LANES - ML_HEADS, 1)
        b = _time_scan(f_t, jnp.add, 0.0)
        key = i_t - b
        m_intra = b + _time_scan(key, jnp.maximum, -jnp.inf)
        m_state = m_row + b
        m_t = jnp.maximum(m_state, m_intra)
        m_new = m_t[c - 1:c, :]
        b_last = b[c - 1:c, :]
        inter_t.append(jnp.exp(m_state - m_t))
        eneg_t.append(jnp.exp(-m_t))
        carry_r.append(jnp.exp(m_row + b_last - m_new))
        kw_t.append(jnp.exp(key + b_last - m_new))
        col_part.append(b - m_t)
        key_t.append(key.T)
        m_row = m_new
    hist[...] = tail
    mstate[...] = jnp.broadcast_to(m_row, mstate.shape)

    q = [qk_all[n][:, h * ML_DQK:(h + 1) * ML_DQK] for n, h in units]
    k = [qk_all[n][:, ML_QK + h * ML_DQK:ML_QK + (h + 1) * ML_DQK] * ML_DQK ** -0.5 for n, h in units]
    v = [_load_permuted(h_ref, n, qk_tiles + h) for n, h in units]
    qk = [_dot_nt(q[i], k[i]) for i in ids]
    w_qk = [jnp.exp(jnp.where(causal, col_part[n][:, h:h + 1] + key_t[n][h:h + 1, :], -jnp.inf)) * qk[i]
            for i, (n, h) in enumerate(units)]
    k_w = [k[i] * kw_t[n][:, h:h + 1] for i, (n, h) in enumerate(units)]
    ones = jnp.ones((c, LANES), F32)
    v_ext = [jnp.concatenate([x, ones], axis=1) for x in v]
    kv = [_dot_tn(k_w[i], v_ext[i]) for i in ids]

    c_in = []
    c_cur = [cstate[h] for h in heads]
    for i, (n, h) in enumerate(units):
        c_in.append(c_cur[h])
        c_cur[h] = c_cur[h] * carry_r[n][:, h:h + 1] + kv[i]
    for h in heads:
        cstate[h] = c_cur[h]

    nd = [inter_t[n][:, h:h + 1] * _dot(q[i], c_in[i]) + _dot(w_qk[i], v_ext[i])
          for i, (n, h) in enumerate(units)]
    h_tilde = [nd[i][:, :ML_DV] / jnp.maximum(jnp.abs(nd[i][:, ML_DV:]), eneg_t[n][:, h:h + 1])
               for i, (n, h) in enumerate(units)]
    for i, (n, h) in enumerate(units):
        og = _load_permuted(h_ref, n, qk_tiles + v_tiles + h)
        z = _load_permuted(h_ref, n, qk_tiles + 2 * v_tiles + h)
        xg = jax.nn.sigmoid(og) * h_tilde[i]
        mu = jnp.mean(xg, -1, keepdims=True)
        var = jnp.mean(jnp.square(xg - mu), -1, keepdims=True)
        xn = (xg - mu) * lax.rsqrt(var + 1e-6) * gnw_ref[:, h * ML_DV:(h + 1) * ML_DV]
        _store_natural_order(y_ref, n, h, xn * _silu(z))


def _ml_mixer(h, b, nc, conv_w, ib_row, fb_row, gn_w):
    nch = h.shape[0] // b // nc
    return pl.pallas_call(
        _ml_kernel,
        grid=(b, nch),
        in_specs=[pl.BlockSpec((nc,) + h.shape[1:], lambda i, j: (i * nch + j, 0, 0, 0)),
                  pl.BlockSpec(conv_w.shape, lambda i, j: (0, 0)),
                  pl.BlockSpec(ib_row.shape, lambda i, j: (0, 0)),
                  pl.BlockSpec(fb_row.shape, lambda i, j: (0, 0)),
                  pl.BlockSpec(gn_w.shape, lambda i, j: (0, 0))],
        out_specs=pl.BlockSpec((nc, ML_V // LANES, CHUNK, LANES), lambda i, j: (i * nch + j, 0, 0, 0)),
        out_shape=jax.ShapeDtypeStruct((h.shape[0], ML_V // LANES, CHUNK, LANES), F32),
        scratch_shapes=[pltpu.VMEM(((CONV_K - 1) * SUBLANES, 2 * ML_QK), F32),
                        pltpu.VMEM((ML_HEADS, ML_DQK, ML_DV + LANES), F32),
                        pltpu.VMEM((SUBLANES, LANES), F32)],
        compiler_params=pltpu.CompilerParams(dimension_semantics=("parallel", "arbitrary"),
                                             vmem_limit_bytes=VMEM_LIMIT),
        name="ml_mixer",
    )(h, conv_w, ib_row, fb_row, gn_w)


def _post_kernel(x_ref, y_ref, p_ref, wout_ref, wg_ref, wp_ref, lng_ref, lnb_ref, pnw_ref, o_ref):
    sub_rows = min(POST_SUB_ROWS, x_ref.shape[0])
    n_sub = x_ref.shape[0] // sub_rows
    per = sub_rows // CHUNK
    r, xn, gate, pn = {}, {}, {}, {}

    def out_proj(s):
        rows = slice(s * sub_rows, (s + 1) * sub_rows)
        y = jnp.concatenate([jnp.concatenate([y_ref[ch, t] for t in range(y_ref.shape[1])], axis=1)
                             for ch in range(s * per, (s + 1) * per)], axis=0)
        r[s] = DEEPNORM_ALPHA * x_ref[rows, :] + _dot(y, wout_ref[...])

    def layer_norm(s):
        mu = jnp.mean(r[s], -1, keepdims=True)
        var = jnp.mean(jnp.square(r[s] - mu), -1, keepdims=True)
        xn[s] = (r[s] - mu) * lax.rsqrt(var + LN_EPS) * lng_ref[...] + lnb_ref[...]

    def gate_and_embed(s):
        rows = slice(s * sub_rows, (s + 1) * sub_rows)
        gate[s] = jax.nn.sigmoid(_dot(xn[s], wg_ref[...]))
        pp = _dot(p_ref[rows, :], wp_ref[...])
        pn[s] = pp * lax.rsqrt(jnp.mean(pp * pp, -1, keepdims=True) + 1e-6) * pnw_ref[...]

    def combine(s):
        rows = slice(s * sub_rows, (s + 1) * sub_rows)
        o_ref[rows, :] = xn[s] + gate[s] * pn[s]

    stages = (out_proj, layer_norm, gate_and_embed, combine)
    for slot in range(n_sub + len(stages) - 1):
        for k, stage in enumerate(stages):
            if 0 <= slot - k < n_sub:
                stage(slot - k)


def _post(x2d, y4d, p3d, layer, w_out, w_gate, w_proj, ln_g, ln_b, pn_w, tm):
    m, d = x2d.shape
    p_spec = pl.BlockSpec((None, tm, p3d.shape[2]), lambda i: (layer, i, 0))
    y_spec = pl.BlockSpec((tm // CHUNK,) + y4d.shape[1:], lambda i: (i, 0, 0, 0))
    tile = lambda arr: pl.BlockSpec((tm, arr.shape[1]), lambda i: (i, 0))
    full = lambda arr: pl.BlockSpec(arr.shape, lambda i: (0, 0))
    params = (w_out, w_gate, w_proj, ln_g, ln_b, pn_w)
    return pl.pallas_call(
        _post_kernel,
        grid=(m // tm,),
        in_specs=[tile(x2d), y_spec, p_spec] + [full(a) for a in params],
        out_specs=pl.BlockSpec((tm, d), lambda i: (i, 0)),
        out_shape=jax.ShapeDtypeStruct((m, d), F32),
        compiler_params=pltpu.CompilerParams(dimension_semantics=("parallel",),
                                             vmem_limit_bytes=VMEM_LIMIT),
        name="post_block",
    )(x2d, y4d, p3d, *params)


def _pad_cols(w, n):
    return jnp.pad(w, ((0, 0), (0, n - w.shape[1])))


def _lane_row(vec, offset):
    return jnp.zeros((1, LANES), F32).at[0, offset:offset + vec.shape[0]].set(vec.astype(F32))


def _row_tile(m, cap):
    t = cap
    while t >= CHUNK:
        if m % t == 0:
            return t
        t //= 2
    raise ValueError(f"token count {m} must be a multiple of {CHUNK}")


def kernel(x, p, ln_g, ln_b, ple_w_proj, ple_norm_w, ple_w_gate, dn_w_in, dn_conv_w, dn_a_log, dn_dt_bias, dn_norm_w, dn_w_out, rw_w_in, rw_mu, rw_w0, rw_w_lora_up, rw_a0, rw_a_lora_up, rw_k_k, rw_k_a, rw_r_k, rw_gn_w, rw_gn_b, rw_w_out, ml_w_in, ml_conv_w, ml_i_bias, ml_f_bias, ml_gn_w, ml_w_out):
    b, s, d = x.shape
    assert d == D_MODEL and s % CHUNK == 0
    m = b * s
    tm_proj = _row_tile(m, PROJ_ROWS)
    tm_post = _row_tile(m, POST_ROWS)
    tm_rw = _row_tile(s, PROJ_ROWS)
    chunks_per_step = lambda want: next(n for n in (want, 2, 1) if (s // CHUNK) % n == 0)
    bf = lambda w: w.astype(MXU_DTYPE)
    row = lambda v: v.reshape(1, -1).astype(F32)
    x2d = x.reshape(m, d)
    p3d = p.reshape(DEPTH, m, D_PLE)
    for i in range(DEPTH):
        kind, j = i % 3, i // 3
        if kind == 0:
            n_pad = 2 * DN_QK + 2 * DN_V + LANES
            h = _project(x2d, bf(_pad_cols(dn_w_in[j], n_pad)), tm_proj)
            y = _dn_mixer(h, b, chunks_per_step(DN_CHUNKS), dn_conv_w[j], _lane_row(dn_a_log[j], DN_HEADS),
                          _lane_row(dn_dt_bias[j], DN_HEADS), row(dn_norm_w[j]))
            w_out = dn_w_out[j]
        elif kind == 1:
            r_w, wl_w, k_w, v_w, al_w, z_w = jnp.split(
                rw_w_in[j], np.cumsum([RW_W, RW_LORA, RW_W, RW_W, RW_LORA]).tolist(), axis=1)
            w_cat = jnp.concatenate([r_w, k_w, v_w, z_w, wl_w, al_w], axis=1)
            mu = rw_mu[j]
            mu_cat = jnp.stack([mu[0], mu[2], mu[3], mu[5], mu[1], mu[4], mu[0], mu[0]], axis=0)
            h = _rw_project(x2d.reshape(b, s, d), bf(w_cat), mu_cat, tm_rw)
            y = _rw_mixer(h, chunks_per_step(RW_CHUNKS), bf(rw_w_lora_up[j]), bf(rw_a_lora_up[j]), row(rw_w0[j]),
                          row(rw_a0[j]), row(rw_k_k[j]), row(rw_k_a[j]), row(rw_r_k[j]), row(rw_gn_w[j]),
                          row(rw_gn_b[j]))
            w_out = rw_w_out[j]
        else:
            n_pad = 2 * ML_QK + 3 * ML_V + LANES
            h = _project(x2d, bf(_pad_cols(ml_w_in[j], n_pad)), tm_proj)
            y = _ml_mixer(h, b, chunks_per_step(ML_CHUNKS), ml_conv_w[j], _lane_row(ml_i_bias[j], 0),
                          _lane_row(ml_f_bias[j], ML_HEADS), row(ml_gn_w[j]))
            w_out = ml_w_out[j]
        x2d = _post(x2d, y, p3d, i, bf(w_out), bf(ple_w_gate[i]),
                    bf(ple_w_proj[i]), row(ln_g[i]), row(ln_b[i]), row(ple_norm_w[i]), tm_post)
    return x2d.reshape(b, s, d)
```

```python
import functools

import jax
import jax.numpy as jnp
import numpy as np
from jax import lax
from jax.experimental import pallas as pl
from jax.experimental.pallas import tpu as pltpu

F32 = jnp.float32
MXU_DTYPE = jnp.bfloat16

LANES = 128
SUBLANES = 8
VMEM_LIMIT = 48 * 1024 * 1024

DEPTH = 4
D_MODEL = 1024
D_PLE = 256
CONV_K = 4
CHUNK = 64
LN_EPS = 1e-5
DN_HEADS, DN_DK, DN_DV = 8, 128, 128
DN_QK = DN_HEADS * DN_DK
DN_V = DN_HEADS * DN_DV
RW_HEAD = 64
RW_HEADS = D_MODEL // RW_HEAD
RW_W = RW_HEADS * RW_HEAD
RW_LORA = 64
RW_GN_EPS = 64e-5
ML_HEADS, ML_DQK, ML_DV = 8, 64, 128
ML_QK = ML_HEADS * ML_DQK
ML_V = ML_HEADS * ML_DV
DEEPNORM_ALPHA = (2.0 * DEPTH) ** 0.25
HIST = SUBLANES
DN_CHUNKS = 4
RW_CHUNKS = 8
ML_CHUNKS = 1


def _dot(a, b):
    return lax.dot_general(a.astype(MXU_DTYPE), b.astype(MXU_DTYPE), (((1,), (0,)), ((), ())),
                           preferred_element_type=F32)


def _dot_nt(a, b):
    return lax.dot_general(a.astype(MXU_DTYPE), b.astype(MXU_DTYPE), (((1,), (1,)), ((), ())),
                           preferred_element_type=F32)


def _dot_tn(a, b):
    return lax.dot_general(a.astype(MXU_DTYPE), b.astype(MXU_DTYPE), (((0,), (0,)), ((), ())),
                           preferred_element_type=F32)


def _silu(x):
    return x * jax.nn.sigmoid(x)


def _perm_time(p):
    return ((p & (SUBLANES - 1)) << 3) | (p >> 3)


def _chunk_iota(n, permuted=False):
    row = lax.broadcasted_iota(jnp.int32, (n, n), 0)
    col = lax.broadcasted_iota(jnp.int32, (n, n), 1)
    eye = (row == col).astype(F32)
    if permuted:
        row, col = _perm_time(row), _perm_time(col)
    return eye, row, col


def _doubling_masks(row, col, n):
    masks = []
    s, shift = 1, 0
    while s < n:
        same = (row >> (shift + 1)) == (col >> (shift + 1))
        masks.append(same & ((row & s) != 0) & ((col & s) == 0))
        s, shift = 2 * s, shift + 1
    return masks


HALF = LANES // 2


def _pair_lane_masks(rows):
    lane = lax.broadcasted_iota(jnp.int32, (rows, LANES), 1)
    return lane < HALF, lane & (HALF - 1)


def _row_block_diag(x, in_a):
    return jnp.concatenate([jnp.where(in_a, x, 0.0), jnp.where(in_a, 0.0, x)], axis=0)


def _unit_lower_inverse_pairs(a_list, eye2, masks2, in_a):
    t = [eye2 - jnp.where(masks2[0], a, 0.0) for a in a_list]
    for m in masks2[1:]:
        ta = [_dot(ti, _row_block_diag(jnp.where(m, a, 0.0), in_a)) for ti, a in zip(t, a_list)]
        t = [ti - _dot(tai, _row_block_diag(ti, in_a)) for ti, tai in zip(t, ta)]
    return t


def _load_permuted(ref, chunk, tile):
    return jnp.concatenate([ref[chunk, tile, pl.ds(b, CHUNK // SUBLANES, stride=SUBLANES), :]
                            for b in range(SUBLANES)], axis=0)


def _col_tiles(ref, chunk, first, count):
    return jnp.concatenate([_load_permuted(ref, chunk, first + i) for i in range(count)], axis=1)


def _causal_conv_silu(prev_tail, x, w_ref):
    keep = CONV_K - 1
    tail = x[CHUNK - keep * SUBLANES:, :]
    sub = lax.broadcasted_iota(jnp.int32, tail.shape, 0) & (SUBLANES - 1)
    mixed = jnp.where(sub == SUBLANES - 1, prev_tail, tail)
    wrapped = jnp.concatenate([pltpu.roll(mixed[i * SUBLANES:(i + 1) * SUBLANES, :], 1, 0) for i in range(keep)],
                              axis=0)
    acc = w_ref[CONV_K - 1:CONV_K, :] * x
    for s in range(1, CONV_K):
        shifted = jnp.concatenate([wrapped[(keep - s) * SUBLANES:, :], x[:CHUNK - s * SUBLANES, :]], axis=0)
        acc = acc + w_ref[CONV_K - 1 - s:CONV_K - s, :] * shifted
    return _silu(acc), tail


def _store_natural_order(y_ref, chunk, tile, y):
    for b in range(SUBLANES):
        y_ref[chunk, tile, pl.ds(b, CHUNK // SUBLANES, stride=SUBLANES), :] = y[b * SUBLANES:(b + 1) * SUBLANES, :]


PROJ_COLS = 2 * LANES
PROJ_ROWS = 512
POST_ROWS = 1024
POST_SUB_ROWS = 256


def _proj_kernel(x_ref, w_ref, o_ref, *, tm):
    x = x_ref[...].astype(MXU_DTYPE)
    n = w_ref.shape[1]
    for c0 in range(0, n, PROJ_COLS):
        width = min(PROJ_COLS, n - c0)
        res = lax.dot_general(x, w_ref[:, c0:c0 + width], (((1,), (0,)), ((), ())), preferred_element_type=F32)
        for ch in range(tm // CHUNK):
            for t in range(width // LANES):
                o_ref[ch, c0 // LANES + t] = res[ch * CHUNK:(ch + 1) * CHUNK, t * LANES:(t + 1) * LANES]


def _project(x2d, w, tm):
    m, k = x2d.shape
    n = w.shape[1]
    return pl.pallas_call(
        functools.partial(_proj_kernel, tm=tm),
        grid=(m // tm,),
        in_specs=[pl.BlockSpec((tm, k), lambda i: (i, 0)),
                  pl.BlockSpec((k, n), lambda i: (0, 0))],
        out_specs=pl.BlockSpec((tm // CHUNK, n // LANES, CHUNK, LANES), lambda i: (i, 0, 0, 0)),
        out_shape=jax.ShapeDtypeStruct((m // CHUNK, n // LANES, CHUNK, LANES), F32),
        compiler_params=pltpu.CompilerParams(dimension_semantics=("parallel",),
                                             vmem_limit_bytes=VMEM_LIMIT),
        name="in_proj",
    )(x2d, w)


def _rw_proj_kernel(x_ref, prev_ref, w_ref, mu_ref, o_ref, buf, *, tm):
    x = x_ref[0]
    buf[HIST:HIST + tm, :] = x
    buf[0:HIST, :] = jnp.where(pl.program_id(1) == 0, 0.0, prev_ref[0])
    dx = buf[pl.ds(HIST - 1, tm), :] - x
    for g in range(4):
        lhs = x + mu_ref[g:g + 1, :] * dx
        o_ref[0, :, g * RW_W:(g + 1) * RW_W] = _dot(lhs, w_ref[:, g * RW_W:(g + 1) * RW_W])
    w_lo = w_ref[:, 4 * RW_W:4 * RW_W + LANES]
    lo_w = _dot(x + mu_ref[4:5, :] * dx, w_lo)
    lo_a = _dot(x + mu_ref[5:6, :] * dx, w_lo)
    lane = lax.broadcasted_iota(jnp.int32, lo_w.shape, 1)
    o_ref[0, :, 4 * RW_W:4 * RW_W + LANES] = jnp.where(lane < RW_LORA, lo_w, lo_a)


def _rw_project(x, w, mu, tm):
    b, s, d = x.shape
    n = w.shape[1]
    per = tm // HIST
    return pl.pallas_call(
        functools.partial(_rw_proj_kernel, tm=tm),
        grid=(b, s // tm),
        in_specs=[pl.BlockSpec((1, tm, d), lambda i, j: (i, j, 0)),
                  pl.BlockSpec((1, HIST, d), lambda i, j: (i, jnp.maximum(j * per - 1, 0), 0)),
                  pl.BlockSpec((d, n), lambda i, j: (0, 0)),
                  pl.BlockSpec(mu.shape, lambda i, j: (0, 0))],
        out_specs=pl.BlockSpec((1, tm, n), lambda i, j: (i, j, 0)),
        out_shape=jax.ShapeDtypeStruct((b, s, n), F32),
        scratch_shapes=[pltpu.VMEM((tm + HIST, d), F32)],
        compiler_params=pltpu.CompilerParams(dimension_semantics=("parallel", "parallel"),
                                             vmem_limit_bytes=VMEM_LIMIT),
        name="rw_in_proj",
    )(x, x, w, mu)


def _dn_kernel(h_ref, convw_ref, alog_ref, dtb_ref, normw_ref, y_ref, hist, state):
    c = CHUNK
    nc = h_ref.shape[0]

    @pl.when(pl.program_id(1) == 0)
    def _():
        hist[...] = jnp.zeros(hist.shape, F32)
        state[...] = jnp.zeros(state.shape, F32)

    qkv_tiles = (2 * DN_QK + DN_V) // LANES
    z_tile0 = qkv_tiles
    in_a, col2 = _pair_lane_masks(c)
    row2 = lax.broadcasted_iota(jnp.int32, (c, LANES), 0)
    eye2 = (row2 == col2).astype(F32)
    row2, col2 = _perm_time(row2), _perm_time(col2)
    causal2 = row2 >= col2
    strict2 = row2 > col2
    masks2 = _doubling_masks(row2, col2, c)

    heads = range(DN_HEADS)
    units = [(n, h) for n in range(nc) for h in heads]
    ids = range(len(units))
    qkv, gcum, gcum_t, beta_t, expg_t, kdec_t, glast_r = ([] for _ in range(7))
    tail = hist[...]
    for n in range(nc):
        y_n, tail = _causal_conv_silu(tail, _col_tiles(h_ref, n, 0, qkv_tiles), convw_ref)
        qkv.append(y_n)
        gates = _load_permuted(h_ref, n, z_tile0 + DN_V // LANES)
        beta_t.append(jax.nn.sigmoid(gates))
        g_t = -(jnp.exp(alog_ref[...]) * jax.nn.softplus(gates + dtb_ref[...]))
        gcum.append(_time_scan(g_t, jnp.add, 0.0))
        gcum_t.append(gcum[n].T)
        g_end = gcum[n][c - 1:c, :]
        expg_t.append(jnp.exp(gcum[n]))
        kdec_t.append(jnp.exp(g_end - gcum[n]))
        glast_r.append(jnp.exp(g_end))
    hist[...] = tail

    q = [qkv[n][:, h * DN_DK:(h + 1) * DN_DK] for n, h in units]
    k = [qkv[n][:, DN_QK + h * DN_DK:DN_QK + (h + 1) * DN_DK] for n, h in units]
    v = [qkv[n][:, 2 * DN_QK + h * DN_DV:2 * DN_QK + (h + 1) * DN_DV] for n, h in units]
    q = [x * (lax.rsqrt(jnp.sum(x * x, -1, keepdims=True) + 1e-6) * DN_DK ** -0.5) for x in q]
    k = [x * lax.rsqrt(jnp.sum(x * x, -1, keepdims=True) + 1e-6) for x in k]
    beta = [beta_t[n][:, h:h + 1] for n, h in units]
    g_col = [gcum[n][:, DN_HEADS + h:DN_HEADS + h + 1] for n, h in units]
    g_row = [gcum_t[n][DN_HEADS + h:DN_HEADS + h + 1, :] for n, h in units]
    exp_g = [expg_t[n][:, DN_HEADS + h:DN_HEADS + h + 1] for n, h in units]
    k_beta = [k[i] * beta[i] for i in ids]
    zeros = jnp.zeros((c, DN_DK), F32)
    m2, decay2 = [], []
    for i0 in range(0, len(units), 2):
        i1 = i0 + 1
        lhs = jnp.concatenate([jnp.concatenate([k_beta[i0], k_beta[i1]], axis=1),
                               jnp.concatenate([q[i0], q[i1]], axis=1)], axis=0)
        rhs = jnp.concatenate([jnp.concatenate([k[i0], zeros], axis=1),
                               jnp.concatenate([zeros, k[i1]], axis=1)], axis=0)
        m2.append(_dot_nt(lhs, rhs))
        g_col2 = jnp.where(in_a, g_col[i0], g_col[i1])
        g_row2 = jnp.concatenate([g_row[i0], g_row[i1]], axis=1)
        decay2.append(jnp.exp(jnp.where(causal2, g_col2 - g_row2, -jnp.inf)))
    a2 = [jnp.where(strict2, m[:c] * d, 0.0) for m, d in zip(m2, decay2)]
    qk2 = [m[c:] * d for m, d in zip(m2, decay2)]
    t2 = _unit_lower_inverse_pairs(a2, eye2, masks2, in_a)
    t = [t2[i // 2][:, (i % 2) * c:(i % 2 + 1) * c] for i in ids]
    qk = [qk2[i // 2][:, (i % 2) * c:(i % 2 + 1) * c] for i in ids]
    uw = [_dot(t[i], jnp.concatenate([v[i] * beta[i], k_beta[i] * exp_g[i]], axis=1)) for i in ids]
    q_dec = [q[i] * exp_g[i] for i in ids]
    k_dec = [k[i] * kdec_t[n][:, DN_HEADS + h:DN_HEADS + h + 1] for i, (n, h) in enumerate(units)]
    s_decay = [glast_r[n][:, DN_HEADS + h:DN_HEADS + h + 1] for n, h in units]
    s_cur = [state[h] for h in heads]
    o = []
    for n in range(nc):
        idn = [n * DN_HEADS + h for h in heads]
        ws = [_dot(jnp.concatenate([uw[i][:, DN_DV:], q_dec[i]], axis=0), s_cur[h]) for h, i in zip(heads, idn)]
        v_new = [uw[i][:, :DN_DV] - ws[h][:c] for h, i in zip(heads, idn)]
        o += [ws[h][c:] + _dot(qk[i], v_new[h]) for h, i in zip(heads, idn)]
        s_cur = [s_cur[h] * s_decay[i] + _dot_tn(k_dec[i], v_new[h]) for h, i in zip(heads, idn)]
    for h in heads:
        state[h] = s_cur[h]
    for i, (n, h) in enumerate(units):
        z = _load_permuted(h_ref, n, z_tile0 + h)
        on = o[i] * lax.rsqrt(jnp.mean(o[i] * o[i], -1, keepdims=True) + 1e-6) * normw_ref[...]
        _store_natural_order(y_ref, n, h, on * _silu(z))


def _dn_mixer(h, b, nc, conv_w, alog_row, dtb_row, norm_w):
    nch = h.shape[0] // b // nc
    ncols = 2 * DN_QK + DN_V
    return pl.pallas_call(
        _dn_kernel,
        grid=(b, nch),
        in_specs=[pl.BlockSpec((nc,) + h.shape[1:], lambda i, j: (i * nch + j, 0, 0, 0)),
                  pl.BlockSpec(conv_w.shape, lambda i, j: (0, 0)),
                  pl.BlockSpec(alog_row.shape, lambda i, j: (0, 0)),
                  pl.BlockSpec(dtb_row.shape, lambda i, j: (0, 0)),
                  pl.BlockSpec(norm_w.shape, lambda i, j: (0, 0))],
        out_specs=pl.BlockSpec((nc, DN_V // LANES, CHUNK, LANES), lambda i, j: (i * nch + j, 0, 0, 0)),
        out_shape=jax.ShapeDtypeStruct((h.shape[0], DN_V // LANES, CHUNK, LANES), F32),
        scratch_shapes=[pltpu.VMEM(((CONV_K - 1) * SUBLANES, ncols), F32),
                        pltpu.VMEM((DN_HEADS, DN_DK, DN_DV), F32)],
        compiler_params=pltpu.CompilerParams(dimension_semantics=("parallel", "arbitrary"),
                                             vmem_limit_bytes=VMEM_LIMIT),
        name="dn_mixer",
    )(h, conv_w, alog_row, dtb_row, norm_w)


def _cumsum_natural_time(x):
    row = lax.broadcasted_iota(jnp.int32, x.shape, 0)
    d = 1
    while d < CHUNK:
        if d < SUBLANES:
            shifted = jnp.where(row < d, 0.0, pltpu.roll(x, d, 0))
        else:
            shifted = jnp.concatenate([jnp.zeros((d, x.shape[1]), F32), x[:CHUNK - d, :]], axis=0)
        x = x + shifted
        d *= 2
    return x


def _rw_kernel(h_ref, wup_ref, aup_ref, w0_ref, a0_ref, kk_ref, ka_ref, rk_ref, gnw_ref, gnb_ref,
               y_ref, state):
    c = CHUNK
    nc = y_ref.shape[0]
    pairs = RW_W // LANES

    @pl.when(pl.program_id(1) == 0)
    def _():
        state[...] = jnp.zeros(state.shape, F32)

    in_a, col2 = _pair_lane_masks(c)
    row2 = lax.broadcasted_iota(jnp.int32, (c, LANES), 0)
    causal2 = row2 >= col2
    strict2 = row2 > col2
    eye2 = (row2 == col2).astype(F32)
    masks2 = _doubling_masks(row2, col2, c)
    row_big = lax.broadcasted_iota(jnp.int32, (LANES, LANES), 0)
    col_big = lax.broadcasted_iota(jnp.int32, (LANES, LANES), 1)
    same_head = (row_big < RW_HEAD) == (col_big < RW_HEAD)

    row_bd = functools.partial(_row_block_diag, in_a=in_a)

    def half_sums(x):
        sum_a = jnp.sum(jnp.where(in_a, x, 0.0), -1, keepdims=True)
        sum_b = jnp.sum(jnp.where(in_a, 0.0, x), -1, keepdims=True)
        return jnp.where(in_a, sum_a, sum_b)

    units = [(n, p) for n in range(nc) for p in range(pairs)]
    ids = range(len(units))
    tiles = [slice(p * LANES, (p + 1) * LANES) for p in range(pairs)]

    r_all, v_all, z_all, a_all, k2_all, rkr_all, kku_all = ([] for _ in range(7))
    p_incl, p_excl, p_inv, p_tail, p_last = ([] for _ in range(5))
    for n in range(nc):
        rows = slice(n * c, (n + 1) * c)
        r_all.append(h_ref[0, rows, 0:RW_W])
        k_n = h_ref[0, rows, RW_W:2 * RW_W]
        v_all.append(h_ref[0, rows, 2 * RW_W:3 * RW_W])
        z_all.append(h_ref[0, rows, 3 * RW_W:4 * RW_W])
        lo = h_ref[0, rows, 4 * RW_W:4 * RW_W + LANES]
        w_log = -jax.nn.softplus(-(w0_ref[...] + _dot(jnp.tanh(lo[:, :RW_LORA]), wup_ref[...]))) - 0.5
        a_all.append(jax.nn.sigmoid(a0_ref[...] + _dot(lo[:, RW_LORA:], aup_ref[...])))
        logd = -jnp.exp(w_log)
        lcum = _cumsum_natural_time(logd)
        l_last = lcum[c - 1:c, :]
        p_incl.append(jnp.exp(lcum))
        p_excl.append(jnp.exp(lcum - logd))
        p_inv.append(jnp.exp(-lcum))
        p_tail.append(jnp.exp(l_last - lcum))
        p_last.append(jnp.exp(l_last))
        kku_all.append(k_n * kk_ref[...])
        k2_all.append(k_n * (1.0 + (a_all[n] - 1.0) * ka_ref[...]))
        rkr_all.append(r_all[n] * k2_all[n] * rk_ref[...])

    v = [v_all[n][:, tiles[p]] for n, p in units]
    k2 = [k2_all[n][:, tiles[p]] for n, p in units]
    kku = [kku_all[n][:, tiles[p]] for n, p in units]
    kk = [kku[i] * lax.rsqrt(half_sums(kku[i] * kku[i]) + 1e-6) for i in ids]
    alpha = [-(kk[i] * a_all[n][:, tiles[p]]) for i, (n, p) in enumerate(units)]
    lhs = [jnp.concatenate([kk[i] * p_excl[n][:, tiles[p]], r_all[n][:, tiles[p]] * p_incl[n][:, tiles[p]]], axis=0)
           for i, (n, p) in enumerate(units)]
    rhs = [jnp.concatenate([row_bd(alpha[i] * p_inv[n][:, tiles[p]]), row_bd(k2[i] * p_inv[n][:, tiles[p]])], axis=0)
           for i, (n, p) in enumerate(units)]
    m = [_dot_nt(lhs[i], rhs[i]) for i in ids]
    a_ab = [jnp.where(strict2, -x[:c, :LANES], 0.0) for x in m]
    a_ak = [jnp.where(strict2, x[:c, LANES:], 0.0) for x in m]
    a_r = [jnp.concatenate([jnp.where(causal2, x[c:, :LANES], 0.0), jnp.where(causal2, x[c:, LANES:], 0.0)], axis=1)
           for x in m]
    t = _unit_lower_inverse_pairs(a_ab, eye2, masks2, in_a)
    v_bd = [row_bd(x) for x in v]
    akv = [_dot(a_ak[i], v_bd[i]) for i in ids]
    tail = [jnp.concatenate([alpha[i] * p_tail[n][:, tiles[p]], k2[i] * p_tail[n][:, tiles[p]]], axis=0)
            for i, (n, p) in enumerate(units)]

    s_cur = [state[p] for p in range(pairs)]
    y = []
    for n in range(nc):
        idn = [n * pairs + p for p in range(pairs)]
        lhs_s = [_dot_nt(lhs[i], s_cur[p]) for p, i in enumerate(idn)]
        u = [_dot(t[i], row_bd(lhs_s[p][:c] + akv[i])) for p, i in enumerate(idn)]
        y += [lhs_s[p][c:] + _dot(a_r[i], jnp.concatenate([row_bd(u[p]), v_bd[i]], axis=0))
              for p, i in enumerate(idn)]
        s_cur = [s_cur[p] * p_last[n][:, tiles[p]]
                 + jnp.where(same_head, _dot_tn(jnp.concatenate([u[p], v[i]], axis=0), tail[i]), 0.0)
                 for p, i in enumerate(idn)]
    for p in range(pairs):
        state[p] = s_cur[p]
    inv_d = 1.0 / RW_HEAD
    for i, (n, p) in enumerate(units):
        mu = half_sums(y[i]) * inv_d
        yc = y[i] - mu
        var = half_sums(yc * yc) * inv_d
        yn = yc * lax.rsqrt(var + RW_GN_EPS) * gnw_ref[:, tiles[p]] + gnb_ref[:, tiles[p]]
        bonus = half_sums(rkr_all[n][:, tiles[p]]) * v[i]
        y_ref[n, p] = (yn + bonus) * _silu(z_all[n][:, tiles[p]])


def _rw_mixer(h, nc, wup, aup, w0, a0, k_k, k_a, r_k, gn_w, gn_b):
    b, s, n = h.shape
    nch = s // CHUNK // nc
    full = lambda arr: pl.BlockSpec(arr.shape, lambda i, j: (0,) * arr.ndim)
    params = (wup, aup, w0, a0, k_k, k_a, r_k, gn_w, gn_b)
    return pl.pallas_call(
        _rw_kernel,
        grid=(b, nch),
        in_specs=[pl.BlockSpec((1, nc * CHUNK, n), lambda i, j: (i, j, 0))] + [full(a) for a in params],
        out_specs=pl.BlockSpec((nc, RW_W // LANES, CHUNK, LANES), lambda i, j: (i * nch + j, 0, 0, 0)),
        out_shape=jax.ShapeDtypeStruct((b * nch * nc, RW_W // LANES, CHUNK, LANES), F32),
        scratch_shapes=[pltpu.VMEM((RW_W // LANES, LANES, LANES), F32)],
        compiler_params=pltpu.CompilerParams(dimension_semantics=("parallel", "arbitrary"),
                                             vmem_limit_bytes=VMEM_LIMIT),
        name="rw_mixer",
    )(h, *params)


def _time_shift(x, d, fill):
    if d < SUBLANES:
        last = x[CHUNK - d * SUBLANES:, :]
        sub = lax.broadcasted_iota(jnp.int32, last.shape, 0) & (SUBLANES - 1)
        rolled = jnp.concatenate([pltpu.roll(last[g * SUBLANES:(g + 1) * SUBLANES, :], 1, 0) for g in range(d)], axis=0)
        return jnp.concatenate([jnp.where(sub == 0, fill, rolled), x[:CHUNK - d * SUBLANES, :]], axis=0)
    k = d // SUBLANES
    sub = lax.broadcasted_iota(jnp.int32, x.shape, 0) & (SUBLANES - 1)
    rolled = jnp.concatenate([pltpu.roll(x[g * SUBLANES:(g + 1) * SUBLANES, :], k, 0)
                              for g in range(CHUNK // SUBLANES)], axis=0)
    return jnp.where(sub < k, fill, rolled)


def _time_scan(x, op, identity):
    d = 1
    while d < CHUNK:
        x = op(x, _time_shift(x, d, identity))
        d *= 2
    return x


def _ml_kernel(h_ref, convw_ref, ib_ref, fb_ref, gnw_ref, y_ref, hist, cstate, mstate):
    c = CHUNK
    nc = h_ref.shape[0]

    @pl.when(pl.program_id(1) == 0)
    def _():
        hist[...] = jnp.zeros(hist.shape, F32)
        cstate[...] = jnp.zeros(cstate.shape, F32)
        mstate[...] = jnp.full(mstate.shape, -jnp.inf, F32)

    qk_tiles = 2 * ML_QK // LANES
    v_tiles = ML_V // LANES
    _, row, col = _chunk_iota(c, permuted=True)
    causal = row >= col
    heads = range(ML_HEADS)
    units = [(n, h) for n in range(nc) for h in heads]
    ids = range(len(units))

    qk_all, col_part, key_t, inter_t, eneg_t, kw_t, carry_r = ([] for _ in range(7))
    tail = hist[...]
    m_row = mstate[0:1, :]
    for n in range(nc):
        y_n, tail = _causal_conv_silu(tail, _col_tiles(h_ref, n, 0, qk_tiles), convw_ref)
        qk_all.append(y_n)
        gates = _load_permuted(h_ref, n, qk_tiles + 3 * v_tiles)
        i_t = gates + ib_ref[...]
        f_t = pltpu.roll(jax.nn.log_sigmoid(gates + fb_ref[...]), LANES - ML_HEADS, 1)
        b = _time_scan(f_t, jnp.add, 0.0)
        key = i_t - b
        m_intra = b + _time_scan(key, jnp.maximum, -jnp.inf)
        m_state = m_row + b
        m_t = jnp.maximum(m_state, m_intra)
        m_new = m_t[c - 1:c, :]
        b_last = b[c - 1:c, :]
        inter_t.append(jnp.exp(m_state - m_t))
        eneg_t.append(jnp.exp(-m_t))
        carry_r.append(jnp.exp(m_row + b_last - m_new))
        kw_t.append(jnp.exp(key + b_last - m_new))
        col_part.append(b - m_t)
        key_t.append(key.T)
        m_row = m_new
    hist[...] = tail
    mstate[...] = jnp.broadcast_to(m_row, mstate.shape)

    q = [qk_all[n][:, h * ML_DQK:(h + 1) * ML_DQK] for n, h in units]
    k = [qk_all[n][:, ML_QK + h * ML_DQK:ML_QK + (h + 1) * ML_DQK] * ML_DQK ** -0.5 for n, h in units]
    v = [_load_permuted(h_ref, n, qk_tiles + h) for n, h in units]
    qk = [_dot_nt(q[i], k[i]) for i in ids]
    w_qk = [jnp.exp(jnp.where(causal, col_part[n][:, h:h + 1] + key_t[n][h:h + 1, :], -jnp.inf)) * qk[i]
            for i, (n, h) in enumerate(units)]
    k_w = [k[i] * kw_t[n][:, h:h + 1] for i, (n, h) in enumerate(units)]
    ones = jnp.ones((c, LANES), F32)
    v_ext = [jnp.concatenate([x, ones], axis=1) for x in v]
    kv = [_dot_tn(k_w[i], v_ext[i]) for i in ids]

    c_in = []
    c_cur = [cstate[h] for h in heads]
    for i, (n, h) in enumerate(units):
        c_in.append(c_cur[h])
        c_cur[h] = c_cur[h] * carry_r[n][:, h:h + 1] + kv[i]
    for h in heads:
        cstate[h] = c_cur[h]

    nd = [inter_t[n][:, h:h + 1] * _dot(q[i], c_in[i]) + _dot(w_qk[i], v_ext[i])
          for i, (n, h) in enumerate(units)]
    h_tilde = [nd[i][:, :ML_DV] / jnp.maximum(jnp.abs(nd[i][:, ML_DV:]), eneg_t[n][:, h:h + 1])
               for i, (n, h) in enumerate(units)]
    for i, (n, h) in enumerate(units):
        og = _load_permuted(h_ref, n, qk_tiles + v_tiles + h)
        z = _load_permuted(h_ref, n, qk_tiles + 2 * v_tiles + h)
        xg = jax.nn.sigmoid(og) * h_tilde[i]
        mu = jnp.mean(xg, -1, keepdims=True)
        var = jnp.mean(jnp.square(xg - mu), -1, keepdims=True)
        xn = (xg - mu) * lax.rsqrt(var + 1e-6) * gnw_ref[:, h * ML_DV:(h + 1) * ML_DV]
        _store_natural_order(y_ref, n, h, xn * _silu(z))


def _ml_mixer(h, b, nc, conv_w, ib_row, fb_row, gn_w):
    nch = h.shape[0] // b // nc
    return pl.pallas_call(
        _ml_kernel,
        grid=(b, nch),
        in_specs=[pl.BlockSpec((nc,) + h.shape[1:], lambda i, j: (i * nch + j, 0, 0, 0)),
                  pl.BlockSpec(conv_w.shape, lambda i, j: (0, 0)),
                  pl.BlockSpec(ib_row.shape, lambda i, j: (0, 0)),
                  pl.BlockSpec(fb_row.shape, lambda i, j: (0, 0)),
                  pl.BlockSpec(gn_w.shape, lambda i, j: (0, 0))],
        out_specs=pl.BlockSpec((nc, ML_V // LANES, CHUNK, LANES), lambda i, j: (i * nch + j, 0, 0, 0)),
        out_shape=jax.ShapeDtypeStruct((h.shape[0], ML_V // LANES, CHUNK, LANES), F32),
        scratch_shapes=[pltpu.VMEM(((CONV_K - 1) * SUBLANES, 2 * ML_QK), F32),
                        pltpu.VMEM((ML_HEADS, ML_DQK, ML_DV + LANES), F32),
                        pltpu.VMEM((SUBLANES, LANES), F32)],
        compiler_params=pltpu.CompilerParams(dimension_semantics=("parallel", "arbitrary"),
                                             vmem_limit_bytes=VMEM_LIMIT),
        name="ml_mixer",
    )(h, conv_w, ib_row, fb_row, gn_w)


def _post_kernel(x_ref, y_ref, p_ref, wout_ref, wg_ref, wp_ref, lng_ref, lnb_ref, pnw_ref, o_ref):
    sub_rows = min(POST_SUB_ROWS, x_ref.shape[0])
    n_sub = x_ref.shape[0] // sub_rows
    per = sub_rows // CHUNK
    r, xn, gate, pn = {}, {}, {}, {}

    def out_proj(s):
        rows = slice(s * sub_rows, (s + 1) * sub_rows)
        y = jnp.concatenate([jnp.concatenate([y_ref[ch, t] for t in range(y_ref.shape[1])], axis=1)
                             for ch in range(s * per, (s + 1) * per)], axis=0)
        r[s] = DEEPNORM_ALPHA * x_ref[rows, :] + _dot(y, wout_ref[...])

    def layer_norm(s):
        mu = jnp.mean(r[s], -1, keepdims=True)
        var = jnp.mean(jnp.square(r[s] - mu), -1, keepdims=True)
        xn[s] = (r[s] - mu) * lax.rsqrt(var + LN_EPS) * lng_ref[...] + lnb_ref[...]

    def gate_and_embed(s):
        rows = slice(s * sub_rows, (s + 1) * sub_rows)
        gate[s] = jax.nn.sigmoid(_dot(xn[s], wg_ref[...]))
        pp = _dot(p_ref[rows, :], wp_ref[...])
        pn[s] = pp * lax.rsqrt(jnp.mean(pp * pp, -1, keepdims=True) + 1e-6) * pnw_ref[...]

    def combine(s):
        rows = slice(s * sub_rows, (s + 1) * sub_rows)
        o_ref[rows, :] = xn[s] + gate[s] * pn[s]

    stages = (out_proj, layer_norm, gate_and_embed, combine)
    for slot in range(n_sub + len(stages) - 1):
        for k, stage in enumerate(stages):
            if 0 <= slot - k < n_sub:
                stage(slot - k)


def _post(x2d, y4d, p3d, layer, w_out, w_gate, w_proj, ln_g, ln_b, pn_w, tm):
    m, d = x2d.shape
    p_spec = pl.BlockSpec((None, tm, p3d.shape[2]), lambda i: (layer, i, 0))
    y_spec = pl.BlockSpec((tm // CHUNK,) + y4d.shape[1:], lambda i: (i, 0, 0, 0))
    tile = lambda arr: pl.BlockSpec((tm, arr.shape[1]), lambda i: (i, 0))
    full = lambda arr: pl.BlockSpec(arr.shape, lambda i: (0, 0))
    params = (w_out, w_gate, w_proj, ln_g, ln_b, pn_w)
    return pl.pallas_call(
        _post_kernel,
        grid=(m // tm,),
        in_specs=[tile(x2d), y_spec, p_spec] + [full(a) for a in params],
        out_specs=pl.BlockSpec((tm, d), lambda i: (i, 0)),
        out_shape=jax.ShapeDtypeStruct((m, d), F32),
        compiler_params=pltpu.CompilerParams(dimension_semantics=("parallel",),
                                             vmem_limit_bytes=VMEM_LIMIT),
        name="post_block",
    )(x2d, y4d, p3d, *params)


def _pad_cols(w, n):
    return jnp.pad(w, ((0, 0), (0, n - w.shape[1])))


def _lane_row(vec, offset):
    return jnp.zeros((1, LANES), F32).at[0, offset:offset + vec.shape[0]].set(vec.astype(F32))


def _row_tile(m, cap):
    t = cap
    while t >= CHUNK:
        if m % t == 0:
            return t
        t //= 2
    raise ValueError(f"token count {m} must be a multiple of {CHUNK}")


def kernel(x, p, ln_g, ln_b, ple_w_proj, ple_norm_w, ple_w_gate, dn_w_in, dn_conv_w, dn_a_log, dn_dt_bias, dn_norm_w, dn_w_out, rw_w_in, rw_mu, rw_w0, rw_w_lora_up, rw_a0, rw_a_lora_up, rw_k_k, rw_k_a, rw_r_k, rw_gn_w, rw_gn_b, rw_w_out, ml_w_in, ml_conv_w, ml_i_bias, ml_f_bias, ml_gn_w, ml_w_out):
    b, s, d = x.shape
    assert d == D_MODEL and s % CHUNK == 0
    m = b * s
    tm_proj = _row_tile(m, PROJ_ROWS)
    tm_post = _row_tile(m, POST_ROWS)
    tm_rw = _row_tile(s, PROJ_ROWS)
    chunks_per_step = lambda want: next(n for n in (want, 2, 1) if (s // CHUNK) % n == 0)
    bf = lambda w: w.astype(MXU_DTYPE)
    row = lambda v: v.reshape(1, -1).astype(F32)
    x2d = x.reshape(m, d)
    p3d = p.reshape(DEPTH, m, D_PLE)
    for i in range(DEPTH):
        kind, j = i % 3, i // 3
        if kind == 0:
            n_pad = 2 * DN_QK + 2 * DN_V + LANES
            h = _project(x2d, bf(_pad_cols(dn_w_in[j], n_pad)), tm_proj)
            y = _dn_mixer(h, b, chunks_per_step(DN_CHUNKS), dn_conv_w[j], _lane_row(dn_a_log[j], DN_HEADS),
                          _lane_row(dn_dt_bias[j], DN_HEADS), row(dn_norm_w[j]))
            w_out = dn_w_out[j]
        elif kind == 1:
            r_w, wl_w, k_w, v_w, al_w, z_w = jnp.split(
                rw_w_in[j], np.cumsum([RW_W, RW_LORA, RW_W, RW_W, RW_LORA]).tolist(), axis=1)
            w_cat = jnp.concatenate([r_w, k_w, v_w, z_w, wl_w, al_w], axis=1)
            mu = rw_mu[j]
            mu_cat = jnp.stack([mu[0], mu[2], mu[3], mu[5], mu[1], mu[4], mu[0], mu[0]], axis=0)
            h = _rw_project(x2d.reshape(b, s, d), bf(w_cat), mu_cat, tm_rw)
            y = _rw_mixer(h, chunks_per_step(RW_CHUNKS), bf(rw_w_lora_up[j]), bf(rw_a_lora_up[j]), row(rw_w0[j]),
                          row(rw_a0[j]), row(rw_k_k[j]), row(rw_k_a[j]), row(rw_r_k[j]), row(rw_gn_w[j]),
                          row(rw_gn_b[j]))
            w_out = rw_w_out[j]
        else:
            n_pad = 2 * ML_QK + 3 * ML_V + LANES
            h = _project(x2d, bf(_pad_cols(ml_w_in[j], n_pad)), tm_proj)
            y = _ml_mixer(h, b, chunks_per_step(ML_CHUNKS), ml_conv_w[j], _lane_row(ml_i_bias[j], 0),
                          _lane_row(ml_f_bias[j], ML_HEADS), row(ml_gn_w[j]))
            w_out = ml_w_out[j]
        x2d = _post(x2d, y, p3d, i, bf(w_out), bf(ple_w_gate[i]),
                    bf(ple_w_proj[i]), row(ln_g[i]), row(ln_b[i]), row(ple_norm_w[i]), tm_post)
    return x2d.reshape(b, s, d)
```

```python
import functools

import jax
import jax.numpy as jnp
import numpy as np
from jax import lax
from jax.experimental import pallas as pl
from jax.experimental.pallas import tpu as pltpu

F32 = jnp.float32
MXU_DTYPE = jnp.bfloat16

LANES = 128
SUBLANES = 8
VMEM_LIMIT = 48 * 1024 * 1024

DEPTH = 4
D_MODEL = 1024
D_PLE = 256
CONV_K = 4
CHUNK = 64
LN_EPS = 1e-5
DN_HEADS, DN_DK, DN_DV = 8, 128, 128
DN_QK = DN_HEADS * DN_DK
DN_V = DN_HEADS * DN_DV
RW_HEAD = 64
RW_HEADS = D_MODEL // RW_HEAD
RW_W = RW_HEADS * RW_HEAD
RW_LORA = 64
RW_GN_EPS = 64e-5
ML_HEADS, ML_DQK, ML_DV = 8, 64, 128
ML_QK = ML_HEADS * ML_DQK
ML_V = ML_HEADS * ML_DV
DEEPNORM_ALPHA = (2.0 * DEPTH) ** 0.25
HIST = SUBLANES
DN_CHUNKS = 8
RW_CHUNKS = 8
ML_CHUNKS = 1


def _dot(a, b):
    return lax.dot_general(a.astype(MXU_DTYPE), b.astype(MXU_DTYPE), (((1,), (0,)), ((), ())),
                           preferred_element_type=F32)


def _dot_nt(a, b):
    return lax.dot_general(a.astype(MXU_DTYPE), b.astype(MXU_DTYPE), (((1,), (1,)), ((), ())),
                           preferred_element_type=F32)


def _dot_tn(a, b):
    return lax.dot_general(a.astype(MXU_DTYPE), b.astype(MXU_DTYPE), (((0,), (0,)), ((), ())),
                           preferred_element_type=F32)


def _silu(x):
    return x * jax.nn.sigmoid(x)


def _perm_time(p):
    return ((p & (SUBLANES - 1)) << 3) | (p >> 3)


def _chunk_iota(n, permuted=False):
    row = lax.broadcasted_iota(jnp.int32, (n, n), 0)
    col = lax.broadcasted_iota(jnp.int32, (n, n), 1)
    eye = (row == col).astype(F32)
    if permuted:
        row, col = _perm_time(row), _perm_time(col)
    return eye, row, col


def _doubling_masks(row, col, n):
    masks = []
    s, shift = 1, 0
    while s < n:
        same = (row >> (shift + 1)) == (col >> (shift + 1))
        masks.append(same & ((row & s) != 0) & ((col & s) == 0))
        s, shift = 2 * s, shift + 1
    return masks


HALF = LANES // 2


def _pair_lane_masks(rows):
    lane = lax.broadcasted_iota(jnp.int32, (rows, LANES), 1)
    return lane < HALF, lane & (HALF - 1)


def _row_block_diag(x, in_a):
    return jnp.concatenate([jnp.where(in_a, x, 0.0), jnp.where(in_a, 0.0, x)], axis=0)


def _unit_lower_inverse_pairs(a_list, eye2, masks2, in_a):
    t = [eye2 - jnp.where(masks2[0], a, 0.0) for a in a_list]
    for m in masks2[1:]:
        ta = [_dot(ti, _row_block_diag(jnp.where(m, a, 0.0), in_a)) for ti, a in zip(t, a_list)]
        t = [ti - _dot(tai, _row_block_diag(ti, in_a)) for ti, tai in zip(t, ta)]
    return t


def _load_permuted(ref, chunk, tile):
    return jnp.concatenate([ref[chunk, tile, pl.ds(b, CHUNK // SUBLANES, stride=SUBLANES), :]
                            for b in range(SUBLANES)], axis=0)


def _col_tiles(ref, chunk, first, count):
    return jnp.concatenate([_load_permuted(ref, chunk, first + i) for i in range(count)], axis=1)


def _causal_conv_silu(prev_tail, x, w_ref):
    keep = CONV_K - 1
    tail = x[CHUNK - keep * SUBLANES:, :]
    sub = lax.broadcasted_iota(jnp.int32, tail.shape, 0) & (SUBLANES - 1)
    mixed = jnp.where(sub == SUBLANES - 1, prev_tail, tail)
    wrapped = jnp.concatenate([pltpu.roll(mixed[i * SUBLANES:(i + 1) * SUBLANES, :], 1, 0) for i in range(keep)],
                              axis=0)
    acc = w_ref[CONV_K - 1:CONV_K, :] * x
    for s in range(1, CONV_K):
        shifted = jnp.concatenate([wrapped[(keep - s) * SUBLANES:, :], x[:CHUNK - s * SUBLANES, :]], axis=0)
        acc = acc + w_ref[CONV_K - 1 - s:CONV_K - s, :] * shifted
    return _silu(acc), tail


def _store_natural_order(y_ref, chunk, tile, y):
    for b in range(SUBLANES):
        y_ref[chunk, tile, pl.ds(b, CHUNK // SUBLANES, stride=SUBLANES), :] = y[b * SUBLANES:(b + 1) * SUBLANES, :]


PROJ_COLS = 2 * LANES
PROJ_ROWS = 512
POST_ROWS = 1024
POST_SUB_ROWS = 256


def _proj_kernel(x_ref, w_ref, o_ref, *, tm):
    x = x_ref[...].astype(MXU_DTYPE)
    n = w_ref.shape[1]
    for c0 in range(0, n, PROJ_COLS):
        width = min(PROJ_COLS, n - c0)
        res = lax.dot_general(x, w_ref[:, c0:c0 + width], (((1,), (0,)), ((), ())), preferred_element_type=F32)
        for ch in range(tm // CHUNK):
            for t in range(width // LANES):
                o_ref[ch, c0 // LANES + t] = res[ch * CHUNK:(ch + 1) * CHUNK, t * LANES:(t + 1) * LANES]


def _project(x2d, w, tm):
    m, k = x2d.shape
    n = w.shape[1]
    return pl.pallas_call(
        functools.partial(_proj_kernel, tm=tm),
        grid=(m // tm,),
        in_specs=[pl.BlockSpec((tm, k), lambda i: (i, 0)),
                  pl.BlockSpec((k, n), lambda i: (0, 0))],
        out_specs=pl.BlockSpec((tm // CHUNK, n // LANES, CHUNK, LANES), lambda i: (i, 0, 0, 0)),
        out_shape=jax.ShapeDtypeStruct((m // CHUNK, n // LANES, CHUNK, LANES), F32),
        compiler_params=pltpu.CompilerParams(dimension_semantics=("parallel",),
                                             vmem_limit_bytes=VMEM_LIMIT),
        name="in_proj",
    )(x2d, w)


def _rw_proj_kernel(x_ref, prev_ref, w_ref, mu_ref, o_ref, buf, *, tm):
    x = x_ref[0]
    buf[HIST:HIST + tm, :] = x
    buf[0:HIST, :] = jnp.where(pl.program_id(1) == 0, 0.0, prev_ref[0])
    dx = buf[pl.ds(HIST - 1, tm), :] - x
    for g in range(4):
        lhs = x + mu_ref[g:g + 1, :] * dx
        o_ref[0, :, g * RW_W:(g + 1) * RW_W] = _dot(lhs, w_ref[:, g * RW_W:(g + 1) * RW_W])
    w_lo = w_ref[:, 4 * RW_W:4 * RW_W + LANES]
    lo_w = _dot(x + mu_ref[4:5, :] * dx, w_lo)
    lo_a = _dot(x + mu_ref[5:6, :] * dx, w_lo)
    lane = lax.broadcasted_iota(jnp.int32, lo_w.shape, 1)
    o_ref[0, :, 4 * RW_W:4 * RW_W + LANES] = jnp.where(lane < RW_LORA, lo_w, lo_a)


def _rw_project(x, w, mu, tm):
    b, s, d = x.shape
    n = w.shape[1]
    per = tm // HIST
    return pl.pallas_call(
        functools.partial(_rw_proj_kernel, tm=tm),
        grid=(b, s // tm),
        in_specs=[pl.BlockSpec((1, tm, d), lambda i, j: (i, j, 0)),
                  pl.BlockSpec((1, HIST, d), lambda i, j: (i, jnp.maximum(j * per - 1, 0), 0)),
                  pl.BlockSpec((d, n), lambda i, j: (0, 0)),
                  pl.BlockSpec(mu.shape, lambda i, j: (0, 0))],
        out_specs=pl.BlockSpec((1, tm, n), lambda i, j: (i, j, 0)),
        out_shape=jax.ShapeDtypeStruct((b, s, n), F32),
        scratch_shapes=[pltpu.VMEM((tm + HIST, d), F32)],
        compiler_params=pltpu.CompilerParams(dimension_semantics=("parallel", "parallel"),
                                             vmem_limit_bytes=VMEM_LIMIT),
        name="rw_in_proj",
    )(x, x, w, mu)


def _dn_kernel(h_ref, convw_ref, alog_ref, dtb_ref, normw_ref, y_ref, hist, state):
    c = CHUNK
    nc = h_ref.shape[0]

    @pl.when(pl.program_id(1) == 0)
    def _():
        hist[...] = jnp.zeros(hist.shape, F32)
        state[...] = jnp.zeros(state.shape, F32)

    qkv_tiles = (2 * DN_QK + DN_V) // LANES
    z_tile0 = qkv_tiles
    in_a, col2 = _pair_lane_masks(c)
    row2 = lax.broadcasted_iota(jnp.int32, (c, LANES), 0)
    eye2 = (row2 == col2).astype(F32)
    row2, col2 = _perm_time(row2), _perm_time(col2)
    causal2 = row2 >= col2
    strict2 = row2 > col2
    masks2 = _doubling_masks(row2, col2, c)

    heads = range(DN_HEADS)
    units = [(n, h) for n in range(nc) for h in heads]
    ids = range(len(units))
    qkv, gcum, gcum_t, beta_t, expg_t, kdec_t, glast_r = ([] for _ in range(7))
    tail = hist[...]
    for n in range(nc):
        y_n, tail = _causal_conv_silu(tail, _col_tiles(h_ref, n, 0, qkv_tiles), convw_ref)
        qkv.append(y_n)
        gates = _load_permuted(h_ref, n, z_tile0 + DN_V // LANES)
        beta_t.append(jax.nn.sigmoid(gates))
        g_t = -(jnp.exp(alog_ref[...]) * jax.nn.softplus(gates + dtb_ref[...]))
        gcum.append(_time_scan(g_t, jnp.add, 0.0))
        gcum_t.append(gcum[n].T)
        g_end = gcum[n][c - 1:c, :]
        expg_t.append(jnp.exp(gcum[n]))
        kdec_t.append(jnp.exp(g_end - gcum[n]))
        glast_r.append(jnp.exp(g_end))
    hist[...] = tail

    q = [qkv[n][:, h * DN_DK:(h + 1) * DN_DK] for n, h in units]
    k = [qkv[n][:, DN_QK + h * DN_DK:DN_QK + (h + 1) * DN_DK] for n, h in units]
    v = [qkv[n][:, 2 * DN_QK + h * DN_DV:2 * DN_QK + (h + 1) * DN_DV] for n, h in units]
    q = [x * (lax.rsqrt(jnp.sum(x * x, -1, keepdims=True) + 1e-6) * DN_DK ** -0.5) for x in q]
    k = [x * lax.rsqrt(jnp.sum(x * x, -1, keepdims=True) + 1e-6) for x in k]
    beta = [beta_t[n][:, h:h + 1] for n, h in units]
    g_col = [gcum[n][:, DN_HEADS + h:DN_HEADS + h + 1] for n, h in units]
    g_row = [gcum_t[n][DN_HEADS + h:DN_HEADS + h + 1, :] for n, h in units]
    exp_g = [expg_t[n][:, DN_HEADS + h:DN_HEADS + h + 1] for n, h in units]
    k_beta = [k[i] * beta[i] for i in ids]
    zeros = jnp.zeros((c, DN_DK), F32)
    m2, decay2 = [], []
    for i0 in range(0, len(units), 2):
        i1 = i0 + 1
        lhs = jnp.concatenate([jnp.concatenate([k_beta[i0], k_beta[i1]], axis=1),
                               jnp.concatenate([q[i0], q[i1]], axis=1)], axis=0)
        rhs = jnp.concatenate([jnp.concatenate([k[i0], zeros], axis=1),
                               jnp.concatenate([zeros, k[i1]], axis=1)], axis=0)
        m2.append(_dot_nt(lhs, rhs))
        g_col2 = jnp.where(in_a, g_col[i0], g_col[i1])
        g_row2 = jnp.concatenate([g_row[i0], g_row[i1]], axis=1)
        decay2.append(jnp.exp(jnp.where(causal2, g_col2 - g_row2, -jnp.inf)))
    a2 = [jnp.where(strict2, m[:c] * d, 0.0) for m, d in zip(m2, decay2)]
    qk2 = [m[c:] * d for m, d in zip(m2, decay2)]
    t2 = _unit_lower_inverse_pairs(a2, eye2, masks2, in_a)
    t = [t2[i // 2][:, (i % 2) * c:(i % 2 + 1) * c] for i in ids]
    qk = [qk2[i // 2][:, (i % 2) * c:(i % 2 + 1) * c] for i in ids]
    uw = [_dot(t[i], jnp.concatenate([v[i] * beta[i], k_beta[i] * exp_g[i]], axis=1)) for i in ids]
    q_dec = [q[i] * exp_g[i] for i in ids]
    k_dec = [k[i] * kdec_t[n][:, DN_HEADS + h:DN_HEADS + h + 1] for i, (n, h) in enumerate(units)]
    s_decay = [glast_r[n][:, DN_HEADS + h:DN_HEADS + h + 1] for n, h in units]
    s_cur = [state[h] for h in heads]
    o = []
    for n in range(nc):
        idn = [n * DN_HEADS + h for h in heads]
        ws = [_dot(jnp.concatenate([uw[i][:, DN_DV:], q_dec[i]], axis=0), s_cur[h]) for h, i in zip(heads, idn)]
        v_new = [uw[i][:, :DN_DV] - ws[h][:c] for h, i in zip(heads, idn)]
        o += [ws[h][c:] + _dot(qk[i], v_new[h]) for h, i in zip(heads, idn)]
        s_cur = [s_cur[h] * s_decay[i] + _dot_tn(k_dec[i], v_new[h]) for h, i in zip(heads, idn)]
    for h in heads:
        state[h] = s_cur[h]
    for i, (n, h) in enumerate(units):
        z = _load_permuted(h_ref, n, z_tile0 + h)
        on = o[i] * lax.rsqrt(jnp.mean(o[i] * o[i], -1, keepdims=True) + 1e-6) * normw_ref[...]
        _store_natural_order(y_ref, n, h, on * _silu(z))


def _dn_mixer(h, b, nc, conv_w, alog_row, dtb_row, norm_w):
    nch = h.shape[0] // b // nc
    ncols = 2 * DN_QK + DN_V
    return pl.pallas_call(
        _dn_kernel,
        grid=(b, nch),
        in_specs=[pl.BlockSpec((nc,) + h.shape[1:], lambda i, j: (i * nch + j, 0, 0, 0)),
                  pl.BlockSpec(conv_w.shape, lambda i, j: (0, 0)),
                  pl.BlockSpec(alog_row.shape, lambda i, j: (0, 0)),
                  pl.BlockSpec(dtb_row.shape, lambda i, j: (0, 0)),
                  pl.BlockSpec(norm_w.shape, lambda i, j: (0, 0))],
        out_specs=pl.BlockSpec((nc, DN_V // LANES, CHUNK, LANES), lambda i, j: (i * nch + j, 0, 0, 0)),
        out_shape=jax.ShapeDtypeStruct((h.shape[0], DN_V // LANES, CHUNK, LANES), F32),
        scratch_shapes=[pltpu.VMEM(((CONV_K - 1) * SUBLANES, ncols), F32),
                        pltpu.VMEM((DN_HEADS, DN_DK, DN_DV), F32)],
        compiler_params=pltpu.CompilerParams(dimension_semantics=("parallel", "arbitrary"),
                                             vmem_limit_bytes=VMEM_LIMIT),
        name="dn_mixer",
    )(h, conv_w, alog_row, dtb_row, norm_w)


def _cumsum_natural_time(x):
    row = lax.broadcasted_iota(jnp.int32, x.shape, 0)
    d = 1
    while d < CHUNK:
        if d < SUBLANES:
            shifted = jnp.where(row < d, 0.0, pltpu.roll(x, d, 0))
        else:
            shifted = jnp.concatenate([jnp.zeros((d, x.shape[1]), F32), x[:CHUNK - d, :]], axis=0)
        x = x + shifted
        d *= 2
    return x


def _rw_kernel(h_ref, wup_ref, aup_ref, w0_ref, a0_ref, kk_ref, ka_ref, rk_ref, gnw_ref, gnb_ref,
               y_ref, state):
    c = CHUNK
    nc = y_ref.shape[0]
    pairs = RW_W // LANES

    @pl.when(pl.program_id(1) == 0)
    def _():
        state[...] = jnp.zeros(state.shape, F32)

    in_a, col2 = _pair_lane_masks(c)
    row2 = lax.broadcasted_iota(jnp.int32, (c, LANES), 0)
    causal2 = row2 >= col2
    strict2 = row2 > col2
    eye2 = (row2 == col2).astype(F32)
    masks2 = _doubling_masks(row2, col2, c)
    row_big = lax.broadcasted_iota(jnp.int32, (LANES, LANES), 0)
    col_big = lax.broadcasted_iota(jnp.int32, (LANES, LANES), 1)
    same_head = (row_big < RW_HEAD) == (col_big < RW_HEAD)

    row_bd = functools.partial(_row_block_diag, in_a=in_a)

    def half_sums(x):
        sum_a = jnp.sum(jnp.where(in_a, x, 0.0), -1, keepdims=True)
        sum_b = jnp.sum(jnp.where(in_a, 0.0, x), -1, keepdims=True)
        return jnp.where(in_a, sum_a, sum_b)

    units = [(n, p) for n in range(nc) for p in range(pairs)]
    ids = range(len(units))
    tiles = [slice(p * LANES, (p + 1) * LANES) for p in range(pairs)]

    r_all, v_all, z_all, a_all, k2_all, rkr_all, kku_all = ([] for _ in range(7))
    p_incl, p_excl, p_inv, p_tail, p_last = ([] for _ in range(5))
    for n in range(nc):
        rows = slice(n * c, (n + 1) * c)
        r_all.append(h_ref[0, rows, 0:RW_W])
        k_n = h_ref[0, rows, RW_W:2 * RW_W]
        v_all.append(h_ref[0, rows, 2 * RW_W:3 * RW_W])
        z_all.append(h_ref[0, rows, 3 * RW_W:4 * RW_W])
        lo = h_ref[0, rows, 4 * RW_W:4 * RW_W + LANES]
        w_log = -jax.nn.softplus(-(w0_ref[...] + _dot(jnp.tanh(lo[:, :RW_LORA]), wup_ref[...]))) - 0.5
        a_all.append(jax.nn.sigmoid(a0_ref[...] + _dot(lo[:, RW_LORA:], aup_ref[...])))
        logd = -jnp.exp(w_log)
        lcum = _cumsum_natural_time(logd)
        l_last = lcum[c - 1:c, :]
        p_incl.append(jnp.exp(lcum))
        p_excl.append(jnp.exp(lcum - logd))
        p_inv.append(jnp.exp(-lcum))
        p_tail.append(jnp.exp(l_last - lcum))
        p_last.append(jnp.exp(l_last))
        kku_all.append(k_n * kk_ref[...])
        k2_all.append(k_n * (1.0 + (a_all[n] - 1.0) * ka_ref[...]))
        rkr_all.append(r_all[n] * k2_all[n] * rk_ref[...])

    v = [v_all[n][:, tiles[p]] for n, p in units]
    k2 = [k2_all[n][:, tiles[p]] for n, p in units]
    kku = [kku_all[n][:, tiles[p]] for n, p in units]
    kk = [kku[i] * lax.rsqrt(half_sums(kku[i] * kku[i]) + 1e-6) for i in ids]
    alpha = [-(kk[i] * a_all[n][:, tiles[p]]) for i, (n, p) in enumerate(units)]
    lhs = [jnp.concatenate([kk[i] * p_excl[n][:, tiles[p]], r_all[n][:, tiles[p]] * p_incl[n][:, tiles[p]]], axis=0)
           for i, (n, p) in enumerate(units)]
    rhs = [jnp.concatenate([row_bd(alpha[i] * p_inv[n][:, tiles[p]]), row_bd(k2[i] * p_inv[n][:, tiles[p]])], axis=0)
           for i, (n, p) in enumerate(units)]
    m = [_dot_nt(lhs[i], rhs[i]) for i in ids]
    a_ab = [jnp.where(strict2, -x[:c, :LANES], 0.0) for x in m]
    a_ak = [jnp.where(strict2, x[:c, LANES:], 0.0) for x in m]
    a_r = [jnp.concatenate([jnp.where(causal2, x[c:, :LANES], 0.0), jnp.where(causal2, x[c:, LANES:], 0.0)], axis=1)
           for x in m]
    t = _unit_lower_inverse_pairs(a_ab, eye2, masks2, in_a)
    v_bd = [row_bd(x) for x in v]
    akv = [_dot(a_ak[i], v_bd[i]) for i in ids]
    tail = [jnp.concatenate([alpha[i] * p_tail[n][:, tiles[p]], k2[i] * p_tail[n][:, tiles[p]]], axis=0)
            for i, (n, p) in enumerate(units)]

    s_cur = [state[p] for p in range(pairs)]
    y = []
    for n in range(nc):
        idn = [n * pairs + p for p in range(pairs)]
        lhs_s = [_dot_nt(lhs[i], s_cur[p]) for p, i in enumerate(idn)]
        u = [_dot(t[i], row_bd(lhs_s[p][:c] + akv[i])) for p, i in enumerate(idn)]
        y += [lhs_s[p][c:] + _dot(a_r[i], jnp.concatenate([row_bd(u[p]), v_bd[i]], axis=0))
              for p, i in enumerate(idn)]
        s_cur = [s_cur[p] * p_last[n][:, tiles[p]]
                 + jnp.where(same_head, _dot_tn(jnp.concatenate([u[p], v[i]], axis=0), tail[i]), 0.0)
                 for p, i in enumerate(idn)]
    for p in range(pairs):
        state[p] = s_cur[p]
    inv_d = 1.0 / RW_HEAD
    for i, (n, p) in enumerate(units):
        mu = half_sums(y[i]) * inv_d
        yc = y[i] - mu
        var = half_sums(yc * yc) * inv_d
        yn = yc * lax.rsqrt(var + RW_GN_EPS) * gnw_ref[:, tiles[p]] + gnb_ref[:, tiles[p]]
        bonus = half_sums(rkr_all[n][:, tiles[p]]) * v[i]
        y_ref[n, p] = (yn + bonus) * _silu(z_all[n][:, tiles[p]])


def _rw_mixer(h, nc, wup, aup, w0, a0, k_k, k_a, r_k, gn_w, gn_b):
    b, s, n = h.shape
    nch = s // CHUNK // nc
    full = lambda arr: pl.BlockSpec(arr.shape, lambda i, j: (0,) * arr.ndim)
    params = (wup, aup, w0, a0, k_k, k_a, r_k, gn_w, gn_b)
    return pl.pallas_call(
        _rw_kernel,
        grid=(b, nch),
        in_specs=[pl.BlockSpec((1, nc * CHUNK, n), lambda i, j: (i, j, 0))] + [full(a) for a in params],
        out_specs=pl.BlockSpec((nc, RW_W // LANES, CHUNK, LANES), lambda i, j: (i * nch + j, 0, 0, 0)),
        out_shape=jax.ShapeDtypeStruct((b * nch * nc, RW_W // LANES, CHUNK, LANES), F32),
        scratch_shapes=[pltpu.VMEM((RW_W // LANES, LANES, LANES), F32)],
        compiler_params=pltpu.CompilerParams(dimension_semantics=("parallel", "arbitrary"),
                                             vmem_limit_bytes=VMEM_LIMIT),
        name="rw_mixer",
    )(h, *params)


def _time_shift(x, d, fill):
    if d < SUBLANES:
        last = x[CHUNK - d * SUBLANES:, :]
        sub = lax.broadcasted_iota(jnp.int32, last.shape, 0) & (SUBLANES - 1)
        rolled = jnp.concatenate([pltpu.roll(last[g * SUBLANES:(g + 1) * SUBLANES, :], 1, 0) for g in range(d)], axis=0)
        return jnp.concatenate([jnp.where(sub == 0, fill, rolled), x[:CHUNK - d * SUBLANES, :]], axis=0)
    k = d // SUBLANES
    sub = lax.broadcasted_iota(jnp.int32, x.shape, 0) & (SUBLANES - 1)
    rolled = jnp.concatenate([pltpu.roll(x[g * SUBLANES:(g + 1) * SUBLANES, :], k, 0)
                              for g in range(CHUNK // SUBLANES)], axis=0)
    return jnp.where(sub < k, fill, rolled)


def _time_scan(x, op, identity):
    d = 1
    while d < CHUNK:
        x = op(x, _time_shift(x, d, identity))
        d *= 2
    return x


def _ml_kernel(h_ref, convw_ref, ib_ref, fb_ref, gnw_ref, y_ref, hist, cstate, mstate):
    c = CHUNK
    nc = h_ref.shape[0]

    @pl.when(pl.program_id(1) == 0)
    def _():
        hist[...] = jnp.zeros(hist.shape, F32)
        cstate[...] = jnp.zeros(cstate.shape, F32)
        mstate[...] = jnp.full(mstate.shape, -jnp.inf, F32)

    qk_tiles = 2 * ML_QK // LANES
    v_tiles = ML_V // LANES
    _, row, col = _chunk_iota(c, permuted=True)
    causal = row >= col
    heads = range(ML_HEADS)
    units = [(n, h) for n in range(nc) for h in heads]
    ids = range(len(units))

    qk_all, col_part, key_t, inter_t, eneg_t, kw_t, carry_r = ([] for _ in range(7))
    tail = hist[...]
    m_row = mstate[0:1, :]
    for n in range(nc):
        y_n, tail = _causal_conv_silu(tail, _col_tiles(h_ref, n, 0, qk_tiles), convw_ref)
        qk_all.append(y_n)
        gates = _load_permuted(h_ref, n, qk_tiles + 3 * v_tiles)
        i_t = gates + ib_ref[...]
        f_t = pltpu.roll(jax.nn.log_sigmoid(gates + fb_ref[...]), LANES - ML_HEADS, 1)
        b = _time_scan(f_t, jnp.add, 0.0)
        key = i_t - b
        m_intra = b + _time_scan(key, jnp.maximum, -jnp.inf)
        m_state = m_row + b
        m_t = jnp.maximum(m_state, m_intra)
        m_new = m_t[c - 1:c, :]
        b_last = b[c - 1:c, :]
        inter_t.append(jnp.exp(m_state - m_t))
        eneg_t.append(jnp.exp(-m_t))
        carry_r.append(jnp.exp(m_row + b_last - m_new))
        kw_t.append(jnp.exp(key + b_last - m_new))
        col_part.append(b - m_t)
        key_t.append(key.T)
        m_row = m_new
    hist[...] = tail
    mstate[...] = jnp.broadcast_to(m_row, mstate.shape)

    q = [qk_all[n][:, h * ML_DQK:(h + 1) * ML_DQK] for n, h in units]
    k = [qk_all[n][:, ML_QK + h * ML_DQK:ML_QK + (h + 1) * ML_DQK] * ML_DQK ** -0.5 for n, h in units]
    v = [_load_permuted(h_ref, n, qk_tiles + h) for n, h in units]
    qk = [_dot_nt(q[i], k[i]) for i in ids]
    w_qk = [jnp.exp(jnp.where(causal, col_part[n][:, h:h + 1] + key_t[n][h:h + 1, :], -jnp.inf)) * qk[i]
            for i, (n, h) in enumerate(units)]
    k_w = [k[i] * kw_t[n][:, h:h + 1] for i, (n, h) in enumerate(units)]
    ones = jnp.ones((c, LANES), F32)
    v_ext = [jnp.concatenate([x, ones], axis=1) for x in v]
    kv = [_dot_tn(k_w[i], v_ext[i]) for i in ids]

    c_in = []
    c_cur = [cstate[h] for h in heads]
    for i, (n, h) in enumerate(units):
        c_in.append(c_cur[h])
        c_cur[h] = c_cur[h] * carry_r[n][:, h:h + 1] + kv[i]
    for h in heads:
        cstate[h] = c_cur[h]

    nd = [inter_t[n][:, h:h + 1] * _dot(q[i], c_in[i]) + _dot(w_qk[i], v_ext[i])
          for i, (n, h) in enumerate(units)]
    h_tilde = [nd[i][:, :ML_DV] / jnp.maximum(jnp.abs(nd[i][:, ML_DV:]), eneg_t[n][:, h:h + 1])
               for i, (n, h) in enumerate(units)]
    for i, (n, h) in enumerate(units):
        og = _load_permuted(h_ref, n, qk_tiles + v_tiles + h)
        z = _load_permuted(h_ref, n, qk_tiles + 2 * v_tiles + h)
        xg = jax.nn.sigmoid(og) * h_tilde[i]
        mu = jnp.mean(xg, -1, keepdims=True)
        var = jnp.mean(jnp.square(xg - mu), -1, keepdims=True)
        xn = (xg - mu) * lax.rsqrt(var + 1e-6) * gnw_ref[:, h * ML_DV:(h + 1) * ML_DV]
        _store_natural_order(y_ref, n, h, xn * _silu(z))


def _ml_mixer(h, b, nc, conv_w, ib_row, fb_row, gn_w):
    nch = h.shape[0] // b // nc
    return pl.pallas_call(
        _ml_kernel,
        grid=(b, nch),
        in_specs=[pl.BlockSpec((nc,) + h.shape[1:], lambda i, j: (i * nch + j, 0, 0, 0)),
                  pl.BlockSpec(conv_w.shape, lambda i, j: (0, 0)),
                  pl.BlockSpec(ib_row.shape, lambda i, j: (0, 0)),
                  pl.BlockSpec(fb_row.shape, lambda i, j: (0, 0)),
                  pl.BlockSpec(gn_w.shape, lambda i, j: (0, 0))],
        out_specs=pl.BlockSpec((nc, ML_V // LANES, CHUNK, LANES), lambda i, j: (i * nch + j, 0, 0, 0)),
        out_shape=jax.ShapeDtypeStruct((h.shape[0], ML_V // LANES, CHUNK, LANES), F32),
        scratch_shapes=[pltpu.VMEM(((CONV_K - 1) * SUBLANES, 2 * ML_QK), F32),
                        pltpu.VMEM((ML_HEADS, ML_DQK, ML_DV + LANES), F32),
                        pltpu.VMEM((SUBLANES, LANES), F32)],
        compiler_params=pltpu.CompilerParams(dimension_semantics=("parallel", "arbitrary"),
                                             vmem_limit_bytes=VMEM_LIMIT),
        name="ml_mixer",
    )(h, conv_w, ib_row, fb_row, gn_w)


def _post_kernel(x_ref, y_ref, p_ref, wout_ref, wg_ref, wp_ref, lng_ref, lnb_ref, pnw_ref, o_ref):
    sub_rows = min(POST_SUB_ROWS, x_ref.shape[0])
    n_sub = x_ref.shape[0] // sub_rows
    per = sub_rows // CHUNK
    r, xn, gate, pn = {}, {}, {}, {}

    def out_proj(s):
        rows = slice(s * sub_rows, (s + 1) * sub_rows)
        y = jnp.concatenate([jnp.concatenate([y_ref[ch, t] for t in range(y_ref.shape[1])], axis=1)
                             for ch in range(s * per, (s + 1) * per)], axis=0)
        r[s] = DEEPNORM_ALPHA * x_ref[rows, :] + _dot(y, wout_ref[...])

    def layer_norm(s):
        mu = jnp.mean(r[s], -1, keepdims=True)
        var = jnp.mean(jnp.square(r[s] - mu), -1, keepdims=True)
        xn[s] = (r[s] - mu) * lax.rsqrt(var + LN_EPS) * lng_ref[...] + lnb_ref[...]

    def gate_and_embed(s):
        rows = slice(s * sub_rows, (s + 1) * sub_rows)
        gate[s] = jax.nn.sigmoid(_dot(xn[s], wg_ref[...]))
        pp = _dot(p_ref[rows, :], wp_ref[...])
        pn[s] = pp * lax.rsqrt(jnp.mean(pp * pp, -1, keepdims=True) + 1e-6) * pnw_ref[...]

    def combine(s):
        rows = slice(s * sub_rows, (s + 1) * sub_rows)
        o_ref[rows, :] = xn[s] + gate[s] * pn[s]

    stages = (out_proj, layer_norm, gate_and_embed, combine)
    for slot in range(n_sub + len(stages) - 1):
        for k, stage in enumerate(stages):
            if 0 <= slot - k < n_sub:
                stage(slot - k)


def _post(x2d, y4d, p3d, layer, w_out, w_gate, w_proj, ln_g, ln_b, pn_w, tm):
    m, d = x2d.shape
    p_spec = pl.BlockSpec((None, tm, p3d.shape[2]), lambda i: (layer, i, 0))
    y_spec = pl.BlockSpec((tm // CHUNK,) + y4d.shape[1:], lambda i: (i, 0, 0, 0))
    tile = lambda arr: pl.BlockSpec((tm, arr.shape[1]), lambda i: (i, 0))
    full = lambda arr: pl.BlockSpec(arr.shape, lambda i: (0, 0))
    params = (w_out, w_gate, w_proj, ln_g, ln_b, pn_w)
    return pl.pallas_call(
        _post_kernel,
        grid=(m // tm,),
        in_specs=[tile(x2d), y_spec, p_spec] + [full(a) for a in params],
        out_specs=pl.BlockSpec((tm, d), lambda i: (i, 0)),
        out_shape=jax.ShapeDtypeStruct((m, d), F32),
        compiler_params=pltpu.CompilerParams(dimension_semantics=("parallel",),
                                             vmem_limit_bytes=VMEM_LIMIT),
        name="post_block",
    )(x2d, y4d, p3d, *params)


def _pad_cols(w, n):
    return jnp.pad(w, ((0, 0), (0, n - w.shape[1])))


def _lane_row(vec, offset):
    return jnp.zeros((1, LANES), F32).at[0, offset:offset + vec.shape[0]].set(vec.astype(F32))


def _row_tile(m, cap):
    t = cap
    while t >= CHUNK:
        if m % t == 0:
            return t
        t //= 2
    raise ValueError(f"token count {m} must be a multiple of {CHUNK}")


def kernel(x, p, ln_g, ln_b, ple_w_proj, ple_norm_w, ple_w_gate, dn_w_in, dn_conv_w, dn_a_log, dn_dt_bias, dn_norm_w, dn_w_out, rw_w_in, rw_mu, rw_w0, rw_w_lora_up, rw_a0, rw_a_lora_up, rw_k_k, rw_k_a, rw_r_k, rw_gn_w, rw_gn_b, rw_w_out, ml_w_in, ml_conv_w, ml_i_bias, ml_f_bias, ml_gn_w, ml_w_out):
    b, s, d = x.shape
    assert d == D_MODEL and s % CHUNK == 0
    m = b * s
    tm_proj = _row_tile(m, PROJ_ROWS)
    tm_post = _row_tile(m, POST_ROWS)
    tm_rw = _row_tile(s, PROJ_ROWS)
    chunks_per_step = lambda want: next(n for n in (want, 2, 1) if (s // CHUNK) % n == 0)
    bf = lambda w: w.astype(MXU_DTYPE)
    row = lambda v: v.reshape(1, -1).astype(F32)
    x2d = x.reshape(m, d)
    p3d = p.reshape(DEPTH, m, D_PLE)
    for i in range(DEPTH):
        kind, j = i % 3, i // 3
        if kind == 0:
            n_pad = 2 * DN_QK + 2 * DN_V + LANES
            h = _project(x2d, bf(_pad_cols(dn_w_in[j], n_pad)), tm_proj)
            y = _dn_mixer(h, b, chunks_per_step(DN_CHUNKS), dn_conv_w[j], _lane_row(dn_a_log[j], DN_HEADS),
                          _lane_row(dn_dt_bias[j], DN_HEADS), row(dn_norm_w[j]))
            w_out = dn_w_out[j]
        elif kind == 1:
            r_w, wl_w, k_w, v_w, al_w, z_w = jnp.split(
                rw_w_in[j], np.cumsum([RW_W, RW_LORA, RW_W, RW_W, RW_LORA]).tolist(), axis=1)
            w_cat = jnp.concatenate([r_w, k_w, v_w, z_w, wl_w, al_w], axis=1)
            mu = rw_mu[j]
            mu_cat = jnp.stack([mu[0], mu[2], mu[3], mu[5], mu[1], mu[4], mu[0], mu[0]], axis=0)
            h = _rw_project(x2d.reshape(b, s, d), bf(w_cat), mu_cat, tm_rw)
            y = _rw_mixer(h, chunks_per_step(RW_CHUNKS), bf(rw_w_lora_up[j]), bf(rw_a_lora_up[j]), row(rw_w0[j]),
                          row(rw_a0[j]), row(rw_k_k[j]), row(rw_k_a[j]), row(rw_r_k[j]), row(rw_gn_w[j]),
                          row(rw_gn_b[j]))
            w_out = rw_w_out[j]
        else:
            n_pad = 2 * ML_QK + 3 * ML_V + LANES
            h = _project(x2d, bf(_pad_cols(ml_w_in[j], n_pad)), tm_proj)
            y = _ml_mixer(h, b, chunks_per_step(ML_CHUNKS), ml_conv_w[j], _lane_row(ml_i_bias[j], 0),
                          _lane_row(ml_f_bias[j], ML_HEADS), row(ml_gn_w[j]))
            w_out = ml_w_out[j]
        x2d = _post(x2d, y, p3d, i, bf(w_out), bf(ple_w_gate[i]),
                    bf(ple_w_proj[i]), row(ln_g[i]), row(ln_b[i]), row(ple_norm_w[i]), tm_post)
    return x2d.reshape(b, s, d)
```

```python
import functools

import jax
import jax.numpy as jnp
import numpy as np
from jax import lax
from jax.experimental import pallas as pl
from jax.experimental.pallas import tpu as pltpu

F32 = jnp.float32
MXU_DTYPE = jnp.bfloat16

LANES = 128
SUBLANES = 8
VMEM_LIMIT = 48 * 1024 * 1024

DEPTH = 4
D_MODEL = 1024
D_PLE = 256
CONV_K = 4
CHUNK = 64
LN_EPS = 1e-5
DN_HEADS, DN_DK, DN_DV = 8, 128, 128
DN_QK = DN_HEADS * DN_DK
DN_V = DN_HEADS * DN_DV
RW_HEAD = 64
RW_HEADS = D_MODEL // RW_HEAD
RW_W = RW_HEADS * RW_HEAD
RW_LORA = 64
RW_GN_EPS = 64e-5
RW_DECAY_SCALE = float(np.exp(-0.5))
ML_HEADS, ML_DQK, ML_DV = 8, 64, 128
ML_QK = ML_HEADS * ML_DQK
ML_V = ML_HEADS * ML_DV
DEEPNORM_ALPHA = (2.0 * DEPTH) ** 0.25
HIST = SUBLANES
DN_CHUNKS = 8
RW_CHUNKS = 8
ML_CHUNKS = 1


def _dot(a, b):
    return lax.dot_general(a.astype(MXU_DTYPE), b.astype(MXU_DTYPE), (((1,), (0,)), ((), ())),
                           preferred_element_type=F32)


def _dot_nt(a, b):
    return lax.dot_general(a.astype(MXU_DTYPE), b.astype(MXU_DTYPE), (((1,), (1,)), ((), ())),
                           preferred_element_type=F32)


def _dot_tn(a, b):
    return lax.dot_general(a.astype(MXU_DTYPE), b.astype(MXU_DTYPE), (((0,), (0,)), ((), ())),
                           preferred_element_type=F32)


def _silu(x):
    return x * jax.nn.sigmoid(x)


def _perm_time(p):
    return ((p & (SUBLANES - 1)) << 3) | (p >> 3)


def _chunk_iota(n, permuted=False):
    row = lax.broadcasted_iota(jnp.int32, (n, n), 0)
    col = lax.broadcasted_iota(jnp.int32, (n, n), 1)
    eye = (row == col).astype(F32)
    if permuted:
        row, col = _perm_time(row), _perm_time(col)
    return eye, row, col


def _doubling_masks(row, col, n):
    masks = []
    s, shift = 1, 0
    while s < n:
        same = (row >> (shift + 1)) == (col >> (shift + 1))
        masks.append(same & ((row & s) != 0) & ((col & s) == 0))
        s, shift = 2 * s, shift + 1
    return masks


HALF = LANES // 2


def _pair_lane_masks(rows):
    lane = lax.broadcasted_iota(jnp.int32, (rows, LANES), 1)
    return lane < HALF, lane & (HALF - 1)


def _row_block_diag(x, in_a):
    return jnp.concatenate([jnp.where(in_a, x, 0.0), jnp.where(in_a, 0.0, x)], axis=0)


def _unit_lower_inverse_pairs(a_list, eye2, masks2, in_a):
    t = [eye2 - jnp.where(masks2[0], a, 0.0) for a in a_list]
    for m in masks2[1:]:
        ta = [_dot(ti, _row_block_diag(jnp.where(m, a, 0.0), in_a)) for ti, a in zip(t, a_list)]
        t = [ti - _dot(tai, _row_block_diag(ti, in_a)) for ti, tai in zip(t, ta)]
    return t


def _load_permuted(ref, chunk, tile):
    return jnp.concatenate([ref[chunk, tile, pl.ds(b, CHUNK // SUBLANES, stride=SUBLANES), :]
                            for b in range(SUBLANES)], axis=0)


def _col_tiles(ref, chunk, first, count):
    return jnp.concatenate([_load_permuted(ref, chunk, first + i) for i in range(count)], axis=1)


def _causal_conv_silu(prev_tail, x, w_ref):
    keep = CONV_K - 1
    tail = x[CHUNK - keep * SUBLANES:, :]
    sub = lax.broadcasted_iota(jnp.int32, tail.shape, 0) & (SUBLANES - 1)
    mixed = jnp.where(sub == SUBLANES - 1, prev_tail, tail)
    wrapped = jnp.concatenate([pltpu.roll(mixed[i * SUBLANES:(i + 1) * SUBLANES, :], 1, 0) for i in range(keep)],
                              axis=0)
    acc = w_ref[CONV_K - 1:CONV_K, :] * x
    for s in range(1, CONV_K):
        shifted = jnp.concatenate([wrapped[(keep - s) * SUBLANES:, :], x[:CHUNK - s * SUBLANES, :]], axis=0)
        acc = acc + w_ref[CONV_K - 1 - s:CONV_K - s, :] * shifted
    return _silu(acc), tail


def _store_natural_order(y_ref, chunk, tile, y):
    for b in range(SUBLANES):
        y_ref[chunk, tile, pl.ds(b, CHUNK // SUBLANES, stride=SUBLANES), :] = y[b * SUBLANES:(b + 1) * SUBLANES, :]


PROJ_COLS = 2 * LANES
PROJ_ROWS = 512
POST_ROWS = 1024
POST_SUB_ROWS = 256


def _proj_kernel(x_ref, w_ref, o_ref, *, tm):
    x = x_ref[...].astype(MXU_DTYPE)
    n = w_ref.shape[1]
    for c0 in range(0, n, PROJ_COLS):
        width = min(PROJ_COLS, n - c0)
        res = lax.dot_general(x, w_ref[:, c0:c0 + width], (((1,), (0,)), ((), ())), preferred_element_type=F32)
        for ch in range(tm // CHUNK):
            for t in range(width // LANES):
                o_ref[ch, c0 // LANES + t] = res[ch * CHUNK:(ch + 1) * CHUNK, t * LANES:(t + 1) * LANES]


def _project(x2d, w, tm):
    m, k = x2d.shape
    n = w.shape[1]
    return pl.pallas_call(
        functools.partial(_proj_kernel, tm=tm),
        grid=(m // tm,),
        in_specs=[pl.BlockSpec((tm, k), lambda i: (i, 0)),
                  pl.BlockSpec((k, n), lambda i: (0, 0))],
        out_specs=pl.BlockSpec((tm // CHUNK, n // LANES, CHUNK, LANES), lambda i: (i, 0, 0, 0)),
        out_shape=jax.ShapeDtypeStruct((m // CHUNK, n // LANES, CHUNK, LANES), F32),
        compiler_params=pltpu.CompilerParams(dimension_semantics=("parallel",),
                                             vmem_limit_bytes=VMEM_LIMIT),
        name="in_proj",
    )(x2d, w)


def _rw_proj_kernel(x_ref, prev_ref, w_ref, mu_ref, o_ref, buf, *, tm):
    x = x_ref[0]
    buf[HIST:HIST + tm, :] = x
    buf[0:HIST, :] = jnp.where(pl.program_id(1) == 0, 0.0, prev_ref[0])
    dx = buf[pl.ds(HIST - 1, tm), :] - x
    for g in range(4):
        lhs = x + mu_ref[g:g + 1, :] * dx
        o_ref[0, :, g * RW_W:(g + 1) * RW_W] = _dot(lhs, w_ref[:, g * RW_W:(g + 1) * RW_W])
    w_lo = w_ref[:, 4 * RW_W:4 * RW_W + LANES]
    lo_w = _dot(x + mu_ref[4:5, :] * dx, w_lo)
    lo_a = _dot(x + mu_ref[5:6, :] * dx, w_lo)
    lane = lax.broadcasted_iota(jnp.int32, lo_w.shape, 1)
    o_ref[0, :, 4 * RW_W:4 * RW_W + LANES] = jnp.where(lane < RW_LORA, lo_w, lo_a)


def _rw_project(x, w, mu, tm):
    b, s, d = x.shape
    n = w.shape[1]
    per = tm // HIST
    return pl.pallas_call(
        functools.partial(_rw_proj_kernel, tm=tm),
        grid=(b, s // tm),
        in_specs=[pl.BlockSpec((1, tm, d), lambda i, j: (i, j, 0)),
                  pl.BlockSpec((1, HIST, d), lambda i, j: (i, jnp.maximum(j * per - 1, 0), 0)),
                  pl.BlockSpec((d, n), lambda i, j: (0, 0)),
                  pl.BlockSpec(mu.shape, lambda i, j: (0, 0))],
        out_specs=pl.BlockSpec((1, tm, n), lambda i, j: (i, j, 0)),
        out_shape=jax.ShapeDtypeStruct((b, s, n), F32),
        scratch_shapes=[pltpu.VMEM((tm + HIST, d), F32)],
        compiler_params=pltpu.CompilerParams(dimension_semantics=("parallel", "parallel"),
                                             vmem_limit_bytes=VMEM_LIMIT),
        name="rw_in_proj",
    )(x, x, w, mu)


def _dn_kernel(h_ref, convw_ref, alog_ref, dtb_ref, normw_ref, y_ref, hist, state):
    c = CHUNK
    nc = h_ref.shape[0]

    @pl.when(pl.program_id(1) == 0)
    def _():
        hist[...] = jnp.zeros(hist.shape, F32)
        state[...] = jnp.zeros(state.shape, F32)

    qkv_tiles = (2 * DN_QK + DN_V) // LANES
    z_tile0 = qkv_tiles
    in_a, col2 = _pair_lane_masks(c)
    row2 = lax.broadcasted_iota(jnp.int32, (c, LANES), 0)
    eye2 = (row2 == col2).astype(F32)
    row2, col2 = _perm_time(row2), _perm_time(col2)
    causal2 = row2 >= col2
    strict2 = row2 > col2
    masks2 = _doubling_masks(row2, col2, c)

    heads = range(DN_HEADS)
    units = [(n, h) for n in range(nc) for h in heads]
    ids = range(len(units))
    qkv, gcum, gcum_t, beta_t, expg_t, kdec_t, glast_r = ([] for _ in range(7))
    tail = hist[...]
    for n in range(nc):
        y_n, tail = _causal_conv_silu(tail, _col_tiles(h_ref, n, 0, qkv_tiles), convw_ref)
        qkv.append(y_n)
        gates = _load_permuted(h_ref, n, z_tile0 + DN_V // LANES)
        beta_t.append(jax.nn.sigmoid(gates))
        g_t = -(jnp.exp(alog_ref[...]) * jax.nn.softplus(gates + dtb_ref[...]))
        gcum.append(_time_scan(g_t, jnp.add, 0.0))
        gcum_t.append(gcum[n].T)
        g_end = gcum[n][c - 1:c, :]
        expg_t.append(jnp.exp(gcum[n]))
        kdec_t.append(jnp.exp(g_end - gcum[n]))
        glast_r.append(jnp.exp(g_end))
    hist[...] = tail

    q = [qkv[n][:, h * DN_DK:(h + 1) * DN_DK] for n, h in units]
    k = [qkv[n][:, DN_QK + h * DN_DK:DN_QK + (h + 1) * DN_DK] for n, h in units]
    v = [qkv[n][:, 2 * DN_QK + h * DN_DV:2 * DN_QK + (h + 1) * DN_DV] for n, h in units]
    q = [x * (lax.rsqrt(jnp.sum(x * x, -1, keepdims=True) + 1e-6) * DN_DK ** -0.5) for x in q]
    k = [x * lax.rsqrt(jnp.sum(x * x, -1, keepdims=True) + 1e-6) for x in k]
    beta = [beta_t[n][:, h:h + 1] for n, h in units]
    g_col = [gcum[n][:, DN_HEADS + h:DN_HEADS + h + 1] for n, h in units]
    g_row = [gcum_t[n][DN_HEADS + h:DN_HEADS + h + 1, :] for n, h in units]
    exp_g = [expg_t[n][:, DN_HEADS + h:DN_HEADS + h + 1] for n, h in units]
    k_beta = [k[i] * beta[i] for i in ids]
    zeros = jnp.zeros((c, DN_DK), F32)
    m2, decay2 = [], []
    for i0 in range(0, len(units), 2):
        i1 = i0 + 1
        lhs = jnp.concatenate([jnp.concatenate([k_beta[i0], k_beta[i1]], axis=1),
                               jnp.concatenate([q[i0], q[i1]], axis=1)], axis=0)
        rhs = jnp.concatenate([jnp.concatenate([k[i0], zeros], axis=1),
                               jnp.concatenate([zeros, k[i1]], axis=1)], axis=0)
        m2.append(_dot_nt(lhs, rhs))
        g_col2 = jnp.where(in_a, g_col[i0], g_col[i1])
        g_row2 = jnp.concatenate([g_row[i0], g_row[i1]], axis=1)
        decay2.append(jnp.exp(jnp.where(causal2, g_col2 - g_row2, -jnp.inf)))
    a2 = [jnp.where(strict2, m[:c] * d, 0.0) for m, d in zip(m2, decay2)]
    qk2 = [m[c:] * d for m, d in zip(m2, decay2)]
    t2 = _unit_lower_inverse_pairs(a2, eye2, masks2, in_a)
    t = [t2[i // 2][:, (i % 2) * c:(i % 2 + 1) * c] for i in ids]
    qk = [qk2[i // 2][:, (i % 2) * c:(i % 2 + 1) * c] for i in ids]
    uw = [_dot(t[i], jnp.concatenate([v[i] * beta[i], k_beta[i] * exp_g[i]], axis=1)) for i in ids]
    q_dec = [q[i] * exp_g[i] for i in ids]
    k_dec = [k[i] * kdec_t[n][:, DN_HEADS + h:DN_HEADS + h + 1] for i, (n, h) in enumerate(units)]
    s_decay = [glast_r[n][:, DN_HEADS + h:DN_HEADS + h + 1] for n, h in units]
    s_cur = [state[h] for h in heads]
    o = []
    for n in range(nc):
        idn = [n * DN_HEADS + h for h in heads]
        ws = [_dot(jnp.concatenate([uw[i][:, DN_DV:], q_dec[i]], axis=0), s_cur[h]) for h, i in zip(heads, idn)]
        v_new = [uw[i][:, :DN_DV] - ws[h][:c] for h, i in zip(heads, idn)]
        o += [ws[h][c:] + _dot(qk[i], v_new[h]) for h, i in zip(heads, idn)]
        s_cur = [s_cur[h] * s_decay[i] + _dot_tn(k_dec[i], v_new[h]) for h, i in zip(heads, idn)]
    for h in heads:
        state[h] = s_cur[h]
    for i, (n, h) in enumerate(units):
        z = _load_permuted(h_ref, n, z_tile0 + h)
        on = o[i] * lax.rsqrt(jnp.mean(o[i] * o[i], -1, keepdims=True) + 1e-6) * normw_ref[...]
        _store_natural_order(y_ref, n, h, on * _silu(z))


def _dn_mixer(h, b, nc, conv_w, alog_row, dtb_row, norm_w):
    nch = h.shape[0] // b // nc
    ncols = 2 * DN_QK + DN_V
    return pl.pallas_call(
        _dn_kernel,
        grid=(b, nch),
        in_specs=[pl.BlockSpec((nc,) + h.shape[1:], lambda i, j: (i * nch + j, 0, 0, 0)),
                  pl.BlockSpec(conv_w.shape, lambda i, j: (0, 0)),
                  pl.BlockSpec(alog_row.shape, lambda i, j: (0, 0)),
                  pl.BlockSpec(dtb_row.shape, lambda i, j: (0, 0)),
                  pl.BlockSpec(norm_w.shape, lambda i, j: (0, 0))],
        out_specs=pl.BlockSpec((nc, DN_V // LANES, CHUNK, LANES), lambda i, j: (i * nch + j, 0, 0, 0)),
        out_shape=jax.ShapeDtypeStruct((h.shape[0], DN_V // LANES, CHUNK, LANES), F32),
        scratch_shapes=[pltpu.VMEM(((CONV_K - 1) * SUBLANES, ncols), F32),
                        pltpu.VMEM((DN_HEADS, DN_DK, DN_DV), F32)],
        compiler_params=pltpu.CompilerParams(dimension_semantics=("parallel", "arbitrary"),
                                             vmem_limit_bytes=VMEM_LIMIT),
        name="dn_mixer",
    )(h, conv_w, alog_row, dtb_row, norm_w)


def _cumsum_natural_time(x):
    row = lax.broadcasted_iota(jnp.int32, x.shape, 0)
    d = 1
    while d < CHUNK:
        if d < SUBLANES:
            shifted = jnp.where(row < d, 0.0, pltpu.roll(x, d, 0))
        else:
            shifted = jnp.concatenate([jnp.zeros((d, x.shape[1]), F32), x[:CHUNK - d, :]], axis=0)
        x = x + shifted
        d *= 2
    return x


def _rw_kernel(h_ref, wup_ref, aup_ref, w0_ref, a0_ref, kk_ref, ka_ref, rk_ref, gnw_ref, gnb_ref,
               y_ref, state):
    c = CHUNK
    nc = y_ref.shape[0]
    pairs = RW_W // LANES

    @pl.when(pl.program_id(1) == 0)
    def _():
        state[...] = jnp.zeros(state.shape, F32)

    in_a, col2 = _pair_lane_masks(c)
    row2 = lax.broadcasted_iota(jnp.int32, (c, LANES), 0)
    causal2 = row2 >= col2
    strict2 = row2 > col2
    eye2 = (row2 == col2).astype(F32)
    masks2 = _doubling_masks(row2, col2, c)
    row_big = lax.broadcasted_iota(jnp.int32, (LANES, LANES), 0)
    col_big = lax.broadcasted_iota(jnp.int32, (LANES, LANES), 1)
    same_head = (row_big < RW_HEAD) == (col_big < RW_HEAD)

    row_bd = functools.partial(_row_block_diag, in_a=in_a)

    def half_sums(x):
        sum_a = jnp.sum(jnp.where(in_a, x, 0.0), -1, keepdims=True)
        sum_b = jnp.sum(jnp.where(in_a, 0.0, x), -1, keepdims=True)
        return jnp.where(in_a, sum_a, sum_b)

    units = [(n, p) for n in range(nc) for p in range(pairs)]
    ids = range(len(units))
    tiles = [slice(p * LANES, (p + 1) * LANES) for p in range(pairs)]

    r_all, v_all, z_all, a_all, k2_all, rkr_all, kku_all = ([] for _ in range(7))
    p_incl, p_excl, p_inv, p_tail, p_last = ([] for _ in range(5))
    for n in range(nc):
        rows = slice(n * c, (n + 1) * c)
        r_all.append(h_ref[0, rows, 0:RW_W])
        k_n = h_ref[0, rows, RW_W:2 * RW_W]
        v_all.append(h_ref[0, rows, 2 * RW_W:3 * RW_W])
        z_all.append(h_ref[0, rows, 3 * RW_W:4 * RW_W])
        lo = h_ref[0, rows, 4 * RW_W:4 * RW_W + LANES]
        z_w = w0_ref[...] + _dot(jnp.tanh(lo[:, :RW_LORA]), wup_ref[...])
        a_all.append(jax.nn.sigmoid(a0_ref[...] + _dot(lo[:, RW_LORA:], aup_ref[...])))
        logd = -RW_DECAY_SCALE * jax.nn.sigmoid(z_w)
        lcum = _cumsum_natural_time(logd)
        l_last = lcum[c - 1:c, :]
        p_incl.append(jnp.exp(lcum))
        p_excl.append(jnp.exp(lcum - logd))
        p_inv.append(jnp.exp(-lcum))
        p_last.append(jnp.exp(l_last))
        p_tail.append(p_last[n] * p_inv[n])
        kku_all.append(k_n * kk_ref[...])
        k2_all.append(k_n * (1.0 + (a_all[n] - 1.0) * ka_ref[...]))
        rkr_all.append(r_all[n] * k2_all[n] * rk_ref[...])

    v = [v_all[n][:, tiles[p]] for n, p in units]
    k2 = [k2_all[n][:, tiles[p]] for n, p in units]
    kku = [kku_all[n][:, tiles[p]] for n, p in units]
    kk = [kku[i] * lax.rsqrt(half_sums(kku[i] * kku[i]) + 1e-6) for i in ids]
    alpha = [-(kk[i] * a_all[n][:, tiles[p]]) for i, (n, p) in enumerate(units)]
    lhs = [jnp.concatenate([kk[i] * p_excl[n][:, tiles[p]], r_all[n][:, tiles[p]] * p_incl[n][:, tiles[p]]], axis=0)
           for i, (n, p) in enumerate(units)]
    rhs = [jnp.concatenate([row_bd(alpha[i] * p_inv[n][:, tiles[p]]), row_bd(k2[i] * p_inv[n][:, tiles[p]])], axis=0)
           for i, (n, p) in enumerate(units)]
    m = [_dot_nt(lhs[i], rhs[i]) for i in ids]
    a_ab = [jnp.where(strict2, -x[:c, :LANES], 0.0) for x in m]
    a_ak = [jnp.where(strict2, x[:c, LANES:], 0.0) for x in m]
    a_r = [jnp.concatenate([jnp.where(causal2, x[c:, :LANES], 0.0), jnp.where(causal2, x[c:, LANES:], 0.0)], axis=1)
           for x in m]
    t = _unit_lower_inverse_pairs(a_ab, eye2, masks2, in_a)
    v_bd = [row_bd(x) for x in v]
    akv = [_dot(a_ak[i], v_bd[i]) for i in ids]
    tail = [jnp.concatenate([alpha[i] * p_tail[n][:, tiles[p]], k2[i] * p_tail[n][:, tiles[p]]], axis=0)
            for i, (n, p) in enumerate(units)]

    s_cur = [state[p] for p in range(pairs)]
    y = []
    for n in range(nc):
        idn = [n * pairs + p for p in range(pairs)]
        lhs_s = [_dot_nt(lhs[i], s_cur[p]) for p, i in enumerate(idn)]
        u = [_dot(t[i], row_bd(lhs_s[p][:c] + akv[i])) for p, i in enumerate(idn)]
        y += [lhs_s[p][c:] + _dot(a_r[i], jnp.concatenate([row_bd(u[p]), v_bd[i]], axis=0))
              for p, i in enumerate(idn)]
        s_cur = [s_cur[p] * p_last[n][:, tiles[p]]
                 + jnp.where(same_head, _dot_tn(jnp.concatenate([u[p], v[i]], axis=0), tail[i]), 0.0)
                 for p, i in enumerate(idn)]
    for p in range(pairs):
        state[p] = s_cur[p]
    inv_d = 1.0 / RW_HEAD
    for i, (n, p) in enumerate(units):
        mu = half_sums(y[i]) * inv_d
        yc = y[i] - mu
        var = half_sums(yc * yc) * inv_d
        yn = yc * lax.rsqrt(var + RW_GN_EPS) * gnw_ref[:, tiles[p]] + gnb_ref[:, tiles[p]]
        bonus = half_sums(rkr_all[n][:, tiles[p]]) * v[i]
        y_ref[n, p] = (yn + bonus) * _silu(z_all[n][:, tiles[p]])


def _rw_mixer(h, nc, wup, aup, w0, a0, k_k, k_a, r_k, gn_w, gn_b):
    b, s, n = h.shape
    nch = s // CHUNK // nc
    full = lambda arr: pl.BlockSpec(arr.shape, lambda i, j: (0,) * arr.ndim)
    params = (wup, aup, w0, a0, k_k, k_a, r_k, gn_w, gn_b)
    return pl.pallas_call(
        _rw_kernel,
        grid=(b, nch),
        in_specs=[pl.BlockSpec((1, nc * CHUNK, n), lambda i, j: (i, j, 0))] + [full(a) for a in params],
        out_specs=pl.BlockSpec((nc, RW_W // LANES, CHUNK, LANES), lambda i, j: (i * nch + j, 0, 0, 0)),
        out_shape=jax.ShapeDtypeStruct((b * nch * nc, RW_W // LANES, CHUNK, LANES), F32),
        scratch_shapes=[pltpu.VMEM((RW_W // LANES, LANES, LANES), F32)],
        compiler_params=pltpu.CompilerParams(dimension_semantics=("parallel", "arbitrary"),
                                             vmem_limit_bytes=VMEM_LIMIT),
        name="rw_mixer",
    )(h, *params)


def _time_shift(x, d, fill):
    if d < SUBLANES:
        last = x[CHUNK - d * SUBLANES:, :]
        sub = lax.broadcasted_iota(jnp.int32, last.shape, 0) & (SUBLANES - 1)
        rolled = jnp.concatenate([pltpu.roll(last[g * SUBLANES:(g + 1) * SUBLANES, :], 1, 0) for g in range(d)], axis=0)
        return jnp.concatenate([jnp.where(sub == 0, fill, rolled), x[:CHUNK - d * SUBLANES, :]], axis=0)
    k = d // SUBLANES
    sub = lax.broadcasted_iota(jnp.int32, x.shape, 0) & (SUBLANES - 1)
    rolled = jnp.concatenate([pltpu.roll(x[g * SUBLANES:(g + 1) * SUBLANES, :], k, 0)
                              for g in range(CHUNK // SUBLANES)], axis=0)
    return jnp.where(sub < k, fill, rolled)


def _time_scan(x, op, identity):
    d = 1
    while d < CHUNK:
        x = op(x, _time_shift(x, d, identity))
        d *= 2
    return x


def _ml_kernel(h_ref, convw_ref, ib_ref, fb_ref, gnw_ref, y_ref, hist, cstate, mstate):
    c = CHUNK
    nc = h_ref.shape[0]

    @pl.when(pl.program_id(1) == 0)
    def _():
        hist[...] = jnp.zeros(hist.shape, F32)
        cstate[...] = jnp.zeros(cstate.shape, F32)
        mstate[...] = jnp.full(mstate.shape, -jnp.inf, F32)

    qk_tiles = 2 * ML_QK // LANES
    v_tiles = ML_V // LANES
    _, row, col = _chunk_iota(c, permuted=True)
    causal = row >= col
    heads = range(ML_HEADS)
    units = [(n, h) for n in range(nc) for h in heads]
    ids = range(len(units))

    qk_all, col_part, key_t, inter_t, eneg_t, kw_t, carry_r = ([] for _ in range(7))
    tail = hist[...]
    m_row = mstate[0:1, :]
    for n in range(nc):
        y_n, tail = _causal_conv_silu(tail, _col_tiles(h_ref, n, 0, qk_tiles), convw_ref)
        qk_all.append(y_n)
        gates = _load_permuted(h_ref, n, qk_tiles + 3 * v_tiles)
        i_t = gates + ib_ref[...]
        f_t = pltpu.roll(jax.nn.log_sigmoid(gates + fb_ref[...]), LANES - ML_HEADS, 1)
        b = _time_scan(f_t, jnp.add, 0.0)
        key = i_t - b
        m_intra = b + _time_scan(key, jnp.maximum, -jnp.inf)
        m_state = m_row + b
        m_t = jnp.maximum(m_state, m_intra)
        m_new = m_t[c - 1:c, :]
        b_last = b[c - 1:c, :]
        inter_t.append(jnp.exp(m_state - m_t))
        eneg_t.append(jnp.exp(-m_t))
        carry_r.append(jnp.exp(m_row + b_last - m_new))
        kw_t.append(jnp.exp(key + b_last - m_new))
        col_part.append(b - m_t)
        key_t.append(key.T)
        m_row = m_new
    hist[...] = tail
    mstate[...] = jnp.broadcast_to(m_row, mstate.shape)

    q = [qk_all[n][:, h * ML_DQK:(h + 1) * ML_DQK] for n, h in units]
    k = [qk_all[n][:, ML_QK + h * ML_DQK:ML_QK + (h + 1) * ML_DQK] * ML_DQK ** -0.5 for n, h in units]
    v = [_load_permuted(h_ref, n, qk_tiles + h) for n, h in units]
    qk = [_dot_nt(q[i], k[i]) for i in ids]
    w_qk = [jnp.exp(jnp.where(causal, col_part[n][:, h:h + 1] + key_t[n][h:h + 1, :], -jnp.inf)) * qk[i]
            for i, (n, h) in enumerate(units)]
    k_w = [k[i] * kw_t[n][:, h:h + 1] for i, (n, h) in enumerate(units)]
    ones = jnp.ones((c, LANES), F32)
    v_ext = [jnp.concatenate([x, ones], axis=1) for x in v]
    kv = [_dot_tn(k_w[i], v_ext[i]) for i in ids]

    c_in = []
    c_cur = [cstate[h] for h in heads]
    for i, (n, h) in enumerate(units):
        c_in.append(c_cur[h])
        c_cur[h] = c_cur[h] * carry_r[n][:, h:h + 1] + kv[i]
    for h in heads:
        cstate[h] = c_cur[h]

    nd = [inter_t[n][:, h:h + 1] * _dot(q[i], c_in[i]) + _dot(w_qk[i], v_ext[i])
          for i, (n, h) in enumerate(units)]
    h_tilde = [nd[i][:, :ML_DV] / jnp.maximum(jnp.abs(nd[i][:, ML_DV:]), eneg_t[n][:, h:h + 1])
               for i, (n, h) in enumerate(units)]
    for i, (n, h) in enumerate(units):
        og = _load_permuted(h_ref, n, qk_tiles + v_tiles + h)
        z = _load_permuted(h_ref, n, qk_tiles + 2 * v_tiles + h)
        xg = jax.nn.sigmoid(og) * h_tilde[i]
        mu = jnp.mean(xg, -1, keepdims=True)
        var = jnp.mean(jnp.square(xg - mu), -1, keepdims=True)
        xn = (xg - mu) * lax.rsqrt(var + 1e-6) * gnw_ref[:, h * ML_DV:(h + 1) * ML_DV]
        _store_natural_order(y_ref, n, h, xn * _silu(z))


def _ml_mixer(h, b, nc, conv_w, ib_row, fb_row, gn_w):
    nch = h.shape[0] // b // nc
    return pl.pallas_call(
        _ml_kernel,
        grid=(b, nch),
        in_specs=[pl.BlockSpec((nc,) + h.shape[1:], lambda i, j: (i * nch + j, 0, 0, 0)),
                  pl.BlockSpec(conv_w.shape, lambda i, j: (0, 0)),
                  pl.BlockSpec(ib_row.shape, lambda i, j: (0, 0)),
                  pl.BlockSpec(fb_row.shape, lambda i, j: (0, 0)),
                  pl.BlockSpec(gn_w.shape, lambda i, j: (0, 0))],
        out_specs=pl.BlockSpec((nc, ML_V // LANES, CHUNK, LANES), lambda i, j: (i * nch + j, 0, 0, 0)),
        out_shape=jax.ShapeDtypeStruct((h.shape[0], ML_V // LANES, CHUNK, LANES), F32),
        scratch_shapes=[pltpu.VMEM(((CONV_K - 1) * SUBLANES, 2 * ML_QK), F32),
                        pltpu.VMEM((ML_HEADS, ML_DQK, ML_DV + LANES), F32),
                        pltpu.VMEM((SUBLANES, LANES), F32)],
        compiler_params=pltpu.CompilerParams(dimension_semantics=("parallel", "arbitrary"),
                                             vmem_limit_bytes=VMEM_LIMIT),
        name="ml_mixer",
    )(h, conv_w, ib_row, fb_row, gn_w)


def _post_kernel(x_ref, y_ref, p_ref, wout_ref, wg_ref, wp_ref, lng_ref, lnb_ref, pnw_ref, o_ref):
    sub_rows = min(POST_SUB_ROWS, x_ref.shape[0])
    n_sub = x_ref.shape[0] // sub_rows
    per = sub_rows // CHUNK
    r, xn, gate, pn = {}, {}, {}, {}

    def out_proj(s):
        rows = slice(s * sub_rows, (s + 1) * sub_rows)
        y = jnp.concatenate([jnp.concatenate([y_ref[ch, t] for t in range(y_ref.shape[1])], axis=1)
                             for ch in range(s * per, (s + 1) * per)], axis=0)
        r[s] = DEEPNORM_ALPHA * x_ref[rows, :] + _dot(y, wout_ref[...])

    def layer_norm(s):
        mu = jnp.mean(r[s], -1, keepdims=True)
        var = jnp.mean(jnp.square(r[s] - mu), -1, keepdims=True)
        xn[s] = (r[s] - mu) * lax.rsqrt(var + LN_EPS) * lng_ref[...] + lnb_ref[...]

    def gate_and_embed(s):
        rows = slice(s * sub_rows, (s + 1) * sub_rows)
        gate[s] = jax.nn.sigmoid(_dot(xn[s], wg_ref[...]))
        pp = _dot(p_ref[rows, :], wp_ref[...])
        pn[s] = pp * lax.rsqrt(jnp.mean(pp * pp, -1, keepdims=True) + 1e-6) * pnw_ref[...]

    def combine(s):
        rows = slice(s * sub_rows, (s + 1) * sub_rows)
        o_ref[rows, :] = xn[s] + gate[s] * pn[s]

    stages = (out_proj, layer_norm, gate_and_embed, combine)
    for slot in range(n_sub + len(stages) - 1):
        for k, stage in enumerate(stages):
            if 0 <= slot - k < n_sub:
                stage(slot - k)


def _post(x2d, y4d, p3d, layer, w_out, w_gate, w_proj, ln_g, ln_b, pn_w, tm):
    m, d = x2d.shape
    p_spec = pl.BlockSpec((None, tm, p3d.shape[2]), lambda i: (layer, i, 0))
    y_spec = pl.BlockSpec((tm // CHUNK,) + y4d.shape[1:], lambda i: (i, 0, 0, 0))
    tile = lambda arr: pl.BlockSpec((tm, arr.shape[1]), lambda i: (i, 0))
    full = lambda arr: pl.BlockSpec(arr.shape, lambda i: (0, 0))
    params = (w_out, w_gate, w_proj, ln_g, ln_b, pn_w)
    return pl.pallas_call(
        _post_kernel,
        grid=(m // tm,),
        in_specs=[tile(x2d), y_spec, p_spec] + [full(a) for a in params],
        out_specs=pl.BlockSpec((tm, d), lambda i: (i, 0)),
        out_shape=jax.ShapeDtypeStruct((m, d), F32),
        compiler_params=pltpu.CompilerParams(dimension_semantics=("parallel",),
                                             vmem_limit_bytes=VMEM_LIMIT),
        name="post_block",
    )(x2d, y4d, p3d, *params)


def _pad_cols(w, n):
    return jnp.pad(w, ((0, 0), (0, n - w.shape[1])))


def _lane_row(vec, offset):
    return jnp.zeros((1, LANES), F32).at[0, offset:offset + vec.shape[0]].set(vec.astype(F32))


def _row_tile(m, cap):
    t = cap
    while t >= CHUNK:
        if m % t == 0:
            return t
        t //= 2
    raise ValueError(f"token count {m} must be a multiple of {CHUNK}")


def kernel(x, p, ln_g, ln_b, ple_w_proj, ple_norm_w, ple_w_gate, dn_w_in, dn_conv_w, dn_a_log, dn_dt_bias, dn_norm_w, dn_w_out, rw_w_in, rw_mu, rw_w0, rw_w_lora_up, rw_a0, rw_a_lora_up, rw_k_k, rw_k_a, rw_r_k, rw_gn_w, rw_gn_b, rw_w_out, ml_w_in, ml_conv_w, ml_i_bias, ml_f_bias, ml_gn_w, ml_w_out):
    b, s, d = x.shape
    assert d == D_MODEL and s % CHUNK == 0
    m = b * s
    tm_proj = _row_tile(m, PROJ_ROWS)
    tm_post = _row_tile(m, POST_ROWS)
    tm_rw = _row_tile(s, PROJ_ROWS)
    chunks_per_step = lambda want: next(n for n in (want, 2, 1) if (s // CHUNK) % n == 0)
    bf = lambda w: w.astype(MXU_DTYPE)
    row = lambda v: v.reshape(1, -1).astype(F32)
    x2d = x.reshape(m, d)
    p3d = p.reshape(DEPTH, m, D_PLE)
    for i in range(DEPTH):
        kind, j = i % 3, i // 3
        if kind == 0:
            n_pad = 2 * DN_QK + 2 * DN_V + LANES
            h = _project(x2d, bf(_pad_cols(dn_w_in[j], n_pad)), tm_proj)
            y = _dn_mixer(h, b, chunks_per_step(DN_CHUNKS), dn_conv_w[j], _lane_row(dn_a_log[j], DN_HEADS),
                          _lane_row(dn_dt_bias[j], DN_HEADS), row(dn_norm_w[j]))
            w_out = dn_w_out[j]
        elif kind == 1:
            r_w, wl_w, k_w, v_w, al_w, z_w = jnp.split(
                rw_w_in[j], np.cumsum([RW_W, RW_LORA, RW_W, RW_W, RW_LORA]).tolist(), axis=1)
            w_cat = jnp.concatenate([r_w, k_w, v_w, z_w, wl_w, al_w], axis=1)
            mu = rw_mu[j]
            mu_cat = jnp.stack([mu[0], mu[2], mu[3], mu[5], mu[1], mu[4], mu[0], mu[0]], axis=0)
            h = _rw_project(x2d.reshape(b, s, d), bf(w_cat), mu_cat, tm_rw)
            y = _rw_mixer(h, chunks_per_step(RW_CHUNKS), bf(rw_w_lora_up[j]), bf(rw_a_lora_up[j]), row(rw_w0[j]),
                          row(rw_a0[j]), row(rw_k_k[j]), row(rw_k_a[j]), row(rw_r_k[j]), row(rw_gn_w[j]),
                          row(rw_gn_b[j]))
            w_out = rw_w_out[j]
        else:
            n_pad = 2 * ML_QK + 3 * ML_V + LANES
            h = _project(x2d, bf(_pad_cols(ml_w_in[j], n_pad)), tm_proj)
            y = _ml_mixer(h, b, chunks_per_step(ML_CHUNKS), ml_conv_w[j], _lane_row(ml_i_bias[j], 0),
                          _lane_row(ml_f_bias[j], ML_HEADS), row(ml_gn_w[j]))
            w_out = ml_w_out[j]
        x2d = _post(x2d, y, p3d, i, bf(w_out), bf(ple_w_gate[i]),
                    bf(ple_w_proj[i]), row(ln_g[i]), row(ln_b[i]), row(ple_norm_w[i]), tm_post)
    return x2d.reshape(b, s, d)
```

```python
import functools

import jax
import jax.numpy as jnp
import numpy as np
from jax import lax
from jax.experimental import pallas as pl
from jax.experimental.pallas import tpu as pltpu

F32 = jnp.float32
MXU_DTYPE = jnp.bfloat16

LANES = 128
SUBLANES = 8
VMEM_LIMIT = 48 * 1024 * 1024

DEPTH = 4
D_MODEL = 1024
D_PLE = 256
CONV_K = 4
CHUNK = 64
LN_EPS = 1e-5
DN_HEADS, DN_DK, DN_DV = 8, 128, 128
DN_QK = DN_HEADS * DN_DK
DN_V = DN_HEADS * DN_DV
RW_HEAD = 64
RW_HEADS = D_MODEL // RW_HEAD
RW_W = RW_HEADS * RW_HEAD
RW_LORA = 64
RW_GN_EPS = 64e-5
RW_DECAY_SCALE = float(np.exp(-0.5))
ML_HEADS, ML_DQK, ML_DV = 8, 64, 128
ML_QK = ML_HEADS * ML_DQK
ML_V = ML_HEADS * ML_DV
DEEPNORM_ALPHA = (2.0 * DEPTH) ** 0.25
HIST = SUBLANES
DN_CHUNKS = 8
RW_CHUNKS = 8
ML_CHUNKS = 1


def _dot(a, b):
    return lax.dot_general(a.astype(MXU_DTYPE), b.astype(MXU_DTYPE), (((1,), (0,)), ((), ())),
                           preferred_element_type=F32)


def _dot_nt(a, b):
    return lax.dot_general(a.astype(MXU_DTYPE), b.astype(MXU_DTYPE), (((1,), (1,)), ((), ())),
                           preferred_element_type=F32)


def _dot_tn(a, b):
    return lax.dot_general(a.astype(MXU_DTYPE), b.astype(MXU_DTYPE), (((0,), (0,)), ((), ())),
                           preferred_element_type=F32)


def _silu(x):
    return x * jax.nn.sigmoid(x)


def _perm_time(p):
    return ((p & (SUBLANES - 1)) << 3) | (p >> 3)


def _chunk_iota(n, permuted=False):
    row = lax.broadcasted_iota(jnp.int32, (n, n), 0)
    col = lax.broadcasted_iota(jnp.int32, (n, n), 1)
    eye = (row == col).astype(F32)
    if permuted:
        row, col = _perm_time(row), _perm_time(col)
    return eye, row, col


def _doubling_masks(row, col, n):
    masks = []
    s, shift = 1, 0
    while s < n:
        same = (row >> (shift + 1)) == (col >> (shift + 1))
        masks.append(same & ((row & s) != 0) & ((col & s) == 0))
        s, shift = 2 * s, shift + 1
    return masks


HALF = LANES // 2


def _pair_lane_masks(rows):
    lane = lax.broadcasted_iota(jnp.int32, (rows, LANES), 1)
    return lane < HALF, lane & (HALF - 1)


def _row_block_diag(x, in_a):
    return jnp.concatenate([jnp.where(in_a, x, 0.0), jnp.where(in_a, 0.0, x)], axis=0)


def _unit_lower_inverse_pairs(a_list, eye2, masks2, in_a):
    t = [eye2 - jnp.where(masks2[0], a, 0.0) for a in a_list]
    for m in masks2[1:]:
        ta = [_dot(ti, _row_block_diag(jnp.where(m, a, 0.0), in_a)) for ti, a in zip(t, a_list)]
        t = [ti - _dot(tai, _row_block_diag(ti, in_a)) for ti, tai in zip(t, ta)]
    return t


def _load_permuted(ref, chunk, tile):
    return jnp.concatenate([ref[chunk, tile, pl.ds(b, CHUNK // SUBLANES, stride=SUBLANES), :]
                            for b in range(SUBLANES)], axis=0)


def _col_tiles(ref, chunk, first, count):
    return jnp.concatenate([_load_permuted(ref, chunk, first + i) for i in range(count)], axis=1)


def _causal_conv_silu(prev_tail, x, w_ref):
    keep = CONV_K - 1
    tail = x[CHUNK - keep * SUBLANES:, :]
    sub = lax.broadcasted_iota(jnp.int32, tail.shape, 0) & (SUBLANES - 1)
    mixed = jnp.where(sub == SUBLANES - 1, prev_tail, tail)
    wrapped = jnp.concatenate([pltpu.roll(mixed[i * SUBLANES:(i + 1) * SUBLANES, :], 1, 0) for i in range(keep)],
                              axis=0)
    acc = w_ref[CONV_K - 1:CONV_K, :] * x
    for s in range(1, CONV_K):
        shifted = jnp.concatenate([wrapped[(keep - s) * SUBLANES:, :], x[:CHUNK - s * SUBLANES, :]], axis=0)
        acc = acc + w_ref[CONV_K - 1 - s:CONV_K - s, :] * shifted
    return _silu(acc), tail


def _store_natural_order(y_ref, chunk, tile, y):
    for b in range(SUBLANES):
        y_ref[chunk, tile, pl.ds(b, CHUNK // SUBLANES, stride=SUBLANES), :] = y[b * SUBLANES:(b + 1) * SUBLANES, :]


PROJ_COLS = 2 * LANES
PROJ_ROWS = 512
POST_ROWS = 1024
POST_SUB_ROWS = 256


def _proj_kernel(x_ref, w_ref, o_ref, *, tm):
    x = x_ref[...].astype(MXU_DTYPE)
    n = w_ref.shape[1]
    for c0 in range(0, n, PROJ_COLS):
        width = min(PROJ_COLS, n - c0)
        res = lax.dot_general(x, w_ref[:, c0:c0 + width], (((1,), (0,)), ((), ())), preferred_element_type=F32)
        for ch in range(tm // CHUNK):
            for t in range(width // LANES):
                o_ref[ch, c0 // LANES + t] = res[ch * CHUNK:(ch + 1) * CHUNK, t * LANES:(t + 1) * LANES]


def _project(x2d, w, tm):
    m, k = x2d.shape
    n = w.shape[1]
    return pl.pallas_call(
        functools.partial(_proj_kernel, tm=tm),
        grid=(m // tm,),
        in_specs=[pl.BlockSpec((tm, k), lambda i: (i, 0)),
                  pl.BlockSpec((k, n), lambda i: (0, 0))],
        out_specs=pl.BlockSpec((tm // CHUNK, n // LANES, CHUNK, LANES), lambda i: (i, 0, 0, 0)),
        out_shape=jax.ShapeDtypeStruct((m // CHUNK, n // LANES, CHUNK, LANES), F32),
        compiler_params=pltpu.CompilerParams(dimension_semantics=("parallel",),
                                             vmem_limit_bytes=VMEM_LIMIT),
        name="in_proj",
    )(x2d, w)


def _rw_proj_kernel(x_ref, prev_ref, w_ref, mu_ref, o_ref, buf, *, tm):
    x = x_ref[0]
    buf[HIST:HIST + tm, :] = x
    buf[0:HIST, :] = jnp.where(pl.program_id(1) == 0, 0.0, prev_ref[0])
    dx = buf[pl.ds(HIST - 1, tm), :] - x
    for g in range(4):
        lhs = x + mu_ref[g:g + 1, :] * dx
        o_ref[0, :, g * RW_W:(g + 1) * RW_W] = _dot(lhs, w_ref[:, g * RW_W:(g + 1) * RW_W])
    w_lo = w_ref[:, 4 * RW_W:4 * RW_W + LANES]
    lo_w = _dot(x + mu_ref[4:5, :] * dx, w_lo)
    lo_a = _dot(x + mu_ref[5:6, :] * dx, w_lo)
    lane = lax.broadcasted_iota(jnp.int32, lo_w.shape, 1)
    o_ref[0, :, 4 * RW_W:4 * RW_W + LANES] = jnp.where(lane < RW_LORA, lo_w, lo_a)


def _rw_project(x, w, mu, tm):
    b, s, d = x.shape
    n = w.shape[1]
    per = tm // HIST
    return pl.pallas_call(
        functools.partial(_rw_proj_kernel, tm=tm),
        grid=(b, s // tm),
        in_specs=[pl.BlockSpec((1, tm, d), lambda i, j: (i, j, 0)),
                  pl.BlockSpec((1, HIST, d), lambda i, j: (i, jnp.maximum(j * per - 1, 0), 0)),
                  pl.BlockSpec((d, n), lambda i, j: (0, 0)),
                  pl.BlockSpec(mu.shape, lambda i, j: (0, 0))],
        out_specs=pl.BlockSpec((1, tm, n), lambda i, j: (i, j, 0)),
        out_shape=jax.ShapeDtypeStruct((b, s, n), F32),
        scratch_shapes=[pltpu.VMEM((tm + HIST, d), F32)],
        compiler_params=pltpu.CompilerParams(dimension_semantics=("parallel", "parallel"),
                                             vmem_limit_bytes=VMEM_LIMIT),
        name="rw_in_proj",
    )(x, x, w, mu)


def _dn_kernel(h_ref, convw_ref, alog_ref, dtb_ref, normw_ref, y_ref, hist, state):
    c = CHUNK
    nc = h_ref.shape[0]

    @pl.when(pl.program_id(1) == 0)
    def _():
        hist[...] = jnp.zeros(hist.shape, F32)
        state[...] = jnp.zeros(state.shape, F32)

    qkv_tiles = (2 * DN_QK + DN_V) // LANES
    z_tile0 = qkv_tiles
    in_a, col2 = _pair_lane_masks(c)
    row2 = lax.broadcasted_iota(jnp.int32, (c, LANES), 0)
    eye2 = (row2 == col2).astype(F32)
    row2, col2 = _perm_time(row2), _perm_time(col2)
    causal2 = row2 >= col2
    strict2 = row2 > col2
    masks2 = _doubling_masks(row2, col2, c)

    heads = range(DN_HEADS)
    units = [(n, h) for n in range(nc) for h in heads]
    ids = range(len(units))
    qkv, gcum, gcum_t, beta_t, expg_t, kdec_t, glast_r = ([] for _ in range(7))
    tail = hist[...]
    for n in range(nc):
        y_n, tail = _causal_conv_silu(tail, _col_tiles(h_ref, n, 0, qkv_tiles), convw_ref)
        qkv.append(y_n)
        gates = _load_permuted(h_ref, n, z_tile0 + DN_V // LANES)
        beta_t.append(jax.nn.sigmoid(gates))
        g_t = -(jnp.exp(alog_ref[...]) * jax.nn.softplus(gates + dtb_ref[...]))
        gcum.append(_time_scan(g_t, jnp.add, 0.0))
        gcum_t.append(gcum[n].T)
        g_end = gcum[n][c - 1:c, :]
        expg_t.append(jnp.exp(gcum[n]))
        kdec_t.append(jnp.exp(g_end - gcum[n]))
        glast_r.append(jnp.exp(g_end))
    hist[...] = tail

    q = [qkv[n][:, h * DN_DK:(h + 1) * DN_DK] for n, h in units]
    k = [qkv[n][:, DN_QK + h * DN_DK:DN_QK + (h + 1) * DN_DK] for n, h in units]
    v = [qkv[n][:, 2 * DN_QK + h * DN_DV:2 * DN_QK + (h + 1) * DN_DV] for n, h in units]
    q = [x * (lax.rsqrt(jnp.sum(x * x, -1, keepdims=True) + 1e-6) * DN_DK ** -0.5) for x in q]
    k = [x * lax.rsqrt(jnp.sum(x * x, -1, keepdims=True) + 1e-6) for x in k]
    beta = [beta_t[n][:, h:h + 1] for n, h in units]
    g_col = [gcum[n][:, DN_HEADS + h:DN_HEADS + h + 1] for n, h in units]
    g_row = [gcum_t[n][DN_HEADS + h:DN_HEADS + h + 1, :] for n, h in units]
    exp_g = [expg_t[n][:, DN_HEADS + h:DN_HEADS + h + 1] for n, h in units]
    k_beta = [k[i] * beta[i] for i in ids]
    zeros = jnp.zeros((c, DN_DK), F32)
    m2, decay2 = [], []
    for i0 in range(0, len(units), 2):
        i1 = i0 + 1
        lhs = jnp.concatenate([jnp.concatenate([k_beta[i0], k_beta[i1]], axis=1),
                               jnp.concatenate([q[i0], q[i1]], axis=1)], axis=0)
        rhs = jnp.concatenate([jnp.concatenate([k[i0], zeros], axis=1),
                               jnp.concatenate([zeros, k[i1]], axis=1)], axis=0)
        m2.append(_dot_nt(lhs, rhs))
        g_col2 = jnp.where(in_a, g_col[i0], g_col[i1])
        g_row2 = jnp.concatenate([g_row[i0], g_row[i1]], axis=1)
        decay2.append(jnp.exp(jnp.where(causal2, g_col2 - g_row2, -jnp.inf)))
    a2 = [jnp.where(strict2, m[:c] * d, 0.0) for m, d in zip(m2, decay2)]
    qk2 = [m[c:] * d for m, d in zip(m2, decay2)]
    t2 = _unit_lower_inverse_pairs(a2, eye2, masks2, in_a)
    t = [t2[i // 2][:, (i % 2) * c:(i % 2 + 1) * c] for i in ids]
    qk = [qk2[i // 2][:, (i % 2) * c:(i % 2 + 1) * c] for i in ids]
    uw = [_dot(t[i], jnp.concatenate([v[i] * beta[i], k_beta[i] * exp_g[i]], axis=1)) for i in ids]
    q_dec = [q[i] * exp_g[i] for i in ids]
    k_dec = [k[i] * kdec_t[n][:, DN_HEADS + h:DN_HEADS + h + 1] for i, (n, h) in enumerate(units)]
    s_decay = [glast_r[n][:, DN_HEADS + h:DN_HEADS + h + 1] for n, h in units]
    s_cur = [state[h] for h in heads]
    o = []
    for n in range(nc):
        idn = [n * DN_HEADS + h for h in heads]
        ws = [_dot(jnp.concatenate([uw[i][:, DN_DV:], q_dec[i]], axis=0), s_cur[h]) for h, i in zip(heads, idn)]
        v_new = [uw[i][:, :DN_DV] - ws[h][:c] for h, i in zip(heads, idn)]
        o += [ws[h][c:] + _dot(qk[i], v_new[h]) for h, i in zip(heads, idn)]
        s_cur = [s_cur[h] * s_decay[i] + _dot_tn(k_dec[i], v_new[h]) for h, i in zip(heads, idn)]
    for h in heads:
        state[h] = s_cur[h]
    for i, (n, h) in enumerate(units):
        z = _load_permuted(h_ref, n, z_tile0 + h)
        on = o[i] * lax.rsqrt(jnp.mean(o[i] * o[i], -1, keepdims=True) + 1e-6) * normw_ref[...]
        _store_natural_order(y_ref, n, h, on * _silu(z))


def _dn_mixer(h, b, nc, conv_w, alog_row, dtb_row, norm_w):
    nch = h.shape[0] // b // nc
    ncols = 2 * DN_QK + DN_V
    return pl.pallas_call(
        _dn_kernel,
        grid=(b, nch),
        in_specs=[pl.BlockSpec((nc,) + h.shape[1:], lambda i, j: (i * nch + j, 0, 0, 0)),
                  pl.BlockSpec(conv_w.shape, lambda i, j: (0, 0)),
                  pl.BlockSpec(alog_row.shape, lambda i, j: (0, 0)),
                  pl.BlockSpec(dtb_row.shape, lambda i, j: (0, 0)),
                  pl.BlockSpec(norm_w.shape, lambda i, j: (0, 0))],
        out_specs=pl.BlockSpec((nc, DN_V // LANES, CHUNK, LANES), lambda i, j: (i * nch + j, 0, 0, 0)),
        out_shape=jax.ShapeDtypeStruct((h.shape[0], DN_V // LANES, CHUNK, LANES), F32),
        scratch_shapes=[pltpu.VMEM(((CONV_K - 1) * SUBLANES, ncols), F32),
                        pltpu.VMEM((DN_HEADS, DN_DK, DN_DV), F32)],
        compiler_params=pltpu.CompilerParams(dimension_semantics=("parallel", "arbitrary"),
                                             vmem_limit_bytes=VMEM_LIMIT),
        name="dn_mixer",
    )(h, conv_w, alog_row, dtb_row, norm_w)


def _cumsum_natural_time(x):
    row = lax.broadcasted_iota(jnp.int32, x.shape, 0)
    d = 1
    while d < CHUNK:
        if d < SUBLANES:
            shifted = jnp.where(row < d, 0.0, pltpu.roll(x, d, 0))
        else:
            shifted = jnp.concatenate([jnp.zeros((d, x.shape[1]), F32), x[:CHUNK - d, :]], axis=0)
        x = x + shifted
        d *= 2
    return x


def _rw_kernel(h_ref, wup_ref, aup_ref, w0_ref, a0_ref, kk_ref, ka_ref, rk_ref, gnw_ref, gnb_ref,
               y_ref, state):
    c = CHUNK
    nc = y_ref.shape[0]
    pairs = RW_W // LANES

    @pl.when(pl.program_id(1) == 0)
    def _():
        state[...] = jnp.zeros(state.shape, F32)

    in_a, col2 = _pair_lane_masks(c)
    row2 = lax.broadcasted_iota(jnp.int32, (c, LANES), 0)
    causal2 = row2 >= col2
    strict2 = row2 > col2
    eye2 = (row2 == col2).astype(F32)
    masks2 = _doubling_masks(row2, col2, c)
    row_big = lax.broadcasted_iota(jnp.int32, (LANES, LANES), 0)
    col_big = lax.broadcasted_iota(jnp.int32, (LANES, LANES), 1)
    same_head = (row_big < RW_HEAD) == (col_big < RW_HEAD)

    row_bd = functools.partial(_row_block_diag, in_a=in_a)

    def half_sums(x):
        sum_a = jnp.sum(jnp.where(in_a, x, 0.0), -1, keepdims=True)
        sum_b = jnp.sum(jnp.where(in_a, 0.0, x), -1, keepdims=True)
        return jnp.where(in_a, sum_a, sum_b)

    units = [(n, p) for n in range(nc) for p in range(pairs)]
    ids = range(len(units))
    tiles = [slice(p * LANES, (p + 1) * LANES) for p in range(pairs)]

    r_all, v_all, z_all, a_all, k2_all, rkr_all, kku_all = ([] for _ in range(7))
    p_incl, p_excl, p_inv, p_tail, p_last = ([] for _ in range(5))
    for n in range(nc):
        rows = slice(n * c, (n + 1) * c)
        r_all.append(h_ref[0, rows, 0:RW_W])
        k_n = h_ref[0, rows, RW_W:2 * RW_W]
        v_all.append(h_ref[0, rows, 2 * RW_W:3 * RW_W])
        z_all.append(h_ref[0, rows, 3 * RW_W:4 * RW_W])
        lo = h_ref[0, rows, 4 * RW_W:4 * RW_W + LANES]
        z_w = w0_ref[...] + _dot(jnp.tanh(lo[:, :RW_LORA]), wup_ref[...])
        a_all.append(jax.nn.sigmoid(a0_ref[...] + _dot(lo[:, RW_LORA:], aup_ref[...])))
        logd = -RW_DECAY_SCALE * jax.nn.sigmoid(z_w)
        lcum = _cumsum_natural_time(logd)
        l_last = lcum[c - 1:c, :]
        p_incl.append(jnp.exp(lcum))
        p_excl.append(jnp.exp(lcum - logd))
        p_inv.append(jnp.exp(-lcum))
        p_last.append(jnp.exp(l_last))
        p_tail.append(p_last[n] * p_inv[n])
        kku_all.append(k_n * kk_ref[...])
        k2_all.append(k_n * (1.0 + (a_all[n] - 1.0) * ka_ref[...]))
        rkr_all.append(r_all[n] * k2_all[n] * rk_ref[...])

    v = [v_all[n][:, tiles[p]] for n, p in units]
    k2 = [k2_all[n][:, tiles[p]] for n, p in units]
    kku = [kku_all[n][:, tiles[p]] for n, p in units]
    kk = [kku[i] * lax.rsqrt(half_sums(kku[i] * kku[i]) + 1e-6) for i in ids]
    alpha = [-(kk[i] * a_all[n][:, tiles[p]]) for i, (n, p) in enumerate(units)]
    lhs = [jnp.concatenate([kk[i] * p_excl[n][:, tiles[p]], r_all[n][:, tiles[p]] * p_incl[n][:, tiles[p]]], axis=0)
           for i, (n, p) in enumerate(units)]
    rhs = [jnp.concatenate([row_bd(alpha[i] * p_inv[n][:, tiles[p]]), row_bd(k2[i] * p_inv[n][:, tiles[p]])], axis=0)
           for i, (n, p) in enumerate(units)]
    m = [_dot_nt(lhs[i], rhs[i]) for i in ids]
    a_ab = [jnp.where(strict2, -x[:c, :LANES], 0.0) for x in m]
    a_ak = [jnp.where(strict2, x[:c, LANES:], 0.0) for x in m]
    a_r = [jnp.concatenate([jnp.where(causal2, x[c:, :LANES], 0.0), jnp.where(causal2, x[c:, LANES:], 0.0)], axis=1)
           for x in m]
    t = _unit_lower_inverse_pairs(a_ab, eye2, masks2, in_a)
    v_bd = [row_bd(x) for x in v]
    akv = [_dot(a_ak[i], v_bd[i]) for i in ids]
    tail = [jnp.concatenate([alpha[i] * p_tail[n][:, tiles[p]], k2[i] * p_tail[n][:, tiles[p]]], axis=0)
            for i, (n, p) in enumerate(units)]

    s_cur = [state[p] for p in range(pairs)]
    y = []
    for n in range(nc):
        idn = [n * pairs + p for p in range(pairs)]
        lhs_s = [_dot_nt(lhs[i], s_cur[p]) for p, i in enumerate(idn)]
        u = [_dot(t[i], row_bd(lhs_s[p][:c] + akv[i])) for p, i in enumerate(idn)]
        y += [lhs_s[p][c:] + _dot(a_r[i], jnp.concatenate([row_bd(u[p]), v_bd[i]], axis=0))
              for p, i in enumerate(idn)]
        s_cur = [s_cur[p] * p_last[n][:, tiles[p]]
                 + jnp.where(same_head, _dot_tn(jnp.concatenate([u[p], v[i]], axis=0), tail[i]), 0.0)
                 for p, i in enumerate(idn)]
    for p in range(pairs):
        state[p] = s_cur[p]
    inv_d = 1.0 / RW_HEAD
    for i, (n, p) in enumerate(units):
        mu = half_sums(y[i]) * inv_d
        yc = y[i] - mu
        var = half_sums(yc * yc) * inv_d
        yn = yc * lax.rsqrt(var + RW_GN_EPS) * gnw_ref[:, tiles[p]] + gnb_ref[:, tiles[p]]
        bonus = half_sums(rkr_all[n][:, tiles[p]]) * v[i]
        y_ref[n, p] = (yn + bonus) * _silu(z_all[n][:, tiles[p]])


def _rw_mixer(h, nc, wup, aup, w0, a0, k_k, k_a, r_k, gn_w, gn_b):
    b, s, n = h.shape
    nch = s // CHUNK // nc
    full = lambda arr: pl.BlockSpec(arr.shape, lambda i, j: (0,) * arr.ndim)
    params = (wup, aup, w0, a0, k_k, k_a, r_k, gn_w, gn_b)
    return pl.pallas_call(
        _rw_kernel,
        grid=(b, nch),
        in_specs=[pl.BlockSpec((1, nc * CHUNK, n), lambda i, j: (i, j, 0))] + [full(a) for a in params],
        out_specs=pl.BlockSpec((nc, RW_W // LANES, CHUNK, LANES), lambda i, j: (i * nch + j, 0, 0, 0)),
        out_shape=jax.ShapeDtypeStruct((b * nch * nc, RW_W // LANES, CHUNK, LANES), F32),
        scratch_shapes=[pltpu.VMEM((RW_W // LANES, LANES, LANES), F32)],
        compiler_params=pltpu.CompilerParams(dimension_semantics=("parallel", "arbitrary"),
                                             vmem_limit_bytes=VMEM_LIMIT),
        name="rw_mixer",
    )(h, *params)


def _time_shift(x, d, fill):
    if d < SUBLANES:
        last = x[CHUNK - d * SUBLANES:, :]
        sub = lax.broadcasted_iota(jnp.int32, last.shape, 0) & (SUBLANES - 1)
        rolled = jnp.concatenate([pltpu.roll(last[g * SUBLANES:(g + 1) * SUBLANES, :], 1, 0) for g in range(d)], axis=0)
        return jnp.concatenate([jnp.where(sub == 0, fill, rolled), x[:CHUNK - d * SUBLANES, :]], axis=0)
    k = d // SUBLANES
    sub = lax.broadcasted_iota(jnp.int32, x.shape, 0) & (SUBLANES - 1)
    rolled = jnp.concatenate([pltpu.roll(x[g * SUBLANES:(g + 1) * SUBLANES, :], k, 0)
                              for g in range(CHUNK // SUBLANES)], axis=0)
    return jnp.where(sub < k, fill, rolled)


def _time_scan(x, op, identity):
    d = 1
    while d < CHUNK:
        x = op(x, _time_shift(x, d, identity))
        d *= 2
    return x


def _ml_kernel(h_ref, convw_ref, ib_ref, fb_ref, gnw_ref, y_ref, hist, cstate, mstate):
    c = CHUNK
    nc = h_ref.shape[0]

    @pl.when(pl.program_id(1) == 0)
    def _():
        hist[...] = jnp.zeros(hist.shape, F32)
        cstate[...] = jnp.zeros(cstate.shape, F32)
        mstate[...] = jnp.full(mstate.shape, -jnp.inf, F32)

    qk_tiles = 2 * ML_QK // LANES
    v_tiles = ML_V // LANES
    _, row, col = _chunk_iota(c, permuted=True)
    causal = row >= col
    heads = range(ML_HEADS)
    units = [(n, h) for n in range(nc) for h in heads]
    ids = range(len(units))

    qk_all, col_part, key_t, inter_t, eneg_t, kw_t, carry_r = ([] for _ in range(7))
    tail = hist[...]
    m_row = mstate[0:1, :]
    for n in range(nc):
        y_n, tail = _causal_conv_silu(tail, _col_tiles(h_ref, n, 0, qk_tiles), convw_ref)
        qk_all.append(y_n)
        gates = _load_permuted(h_ref, n, qk_tiles + 3 * v_tiles)
        i_t = gates + ib_ref[...]
        f_t = pltpu.roll(jax.nn.log_sigmoid(gates + fb_ref[...]), LANES - ML_HEADS, 1)
        b = _time_scan(f_t, jnp.add, 0.0)
        key = i_t - b
        m_intra = b + _time_scan(key, jnp.maximum, -jnp.inf)
        m_state = m_row + b
        m_t = jnp.maximum(m_state, m_intra)
        m_new = m_t[c - 1:c, :]
        b_last = b[c - 1:c, :]
        inter_t.append(jnp.exp(m_state - m_t))
        eneg_t.append(jnp.exp(-m_t))
        carry_r.append(jnp.exp(m_row + b_last - m_new))
        kw_t.append(jnp.exp(key + b_last - m_new))
        col_part.append(b - m_t)
        key_t.append(key.T)
        m_row = m_new
    hist[...] = tail
    mstate[...] = jnp.broadcast_to(m_row, mstate.shape)

    q = [qk_all[n][:, h * ML_DQK:(h + 1) * ML_DQK] for n, h in units]
    k = [qk_all[n][:, ML_QK + h * ML_DQK:ML_QK + (h + 1) * ML_DQK] * ML_DQK ** -0.5 for n, h in units]
    v = [_load_permuted(h_ref, n, qk_tiles + h) for n, h in units]
    qk = [_dot_nt(q[i], k[i]) for i in ids]
    w_qk = [jnp.exp(jnp.where(causal, col_part[n][:, h:h + 1] + key_t[n][h:h + 1, :], -jnp.inf)) * qk[i]
            for i, (n, h) in enumerate(units)]
    k_w = [k[i] * kw_t[n][:, h:h + 1] for i, (n, h) in enumerate(units)]
    ones = jnp.ones((c, LANES), F32)
    v_ext = [jnp.concatenate([x, ones], axis=1) for x in v]
    kv = [_dot_tn(k_w[i], v_ext[i]) for i in ids]

    c_in = []
    c_cur = [cstate[h] for h in heads]
    for i, (n, h) in enumerate(units):
        c_in.append(c_cur[h])
        c_cur[h] = c_cur[h] * carry_r[n][:, h:h + 1] + kv[i]
    for h in heads:
        cstate[h] = c_cur[h]

    nd = [inter_t[n][:, h:h + 1] * _dot(q[i], c_in[i]) + _dot(w_qk[i], v_ext[i])
          for i, (n, h) in enumerate(units)]
    h_tilde = [nd[i][:, :ML_DV] / jnp.maximum(jnp.abs(nd[i][:, ML_DV:]), eneg_t[n][:, h:h + 1])
               for i, (n, h) in enumerate(units)]
    for i, (n, h) in enumerate(units):
        og = _load_permuted(h_ref, n, qk_tiles + v_tiles + h)
        z = _load_permuted(h_ref, n, qk_tiles + 2 * v_tiles + h)
        xg = jax.nn.sigmoid(og) * h_tilde[i]
        mu = jnp.mean(xg, -1, keepdims=True)
        var = jnp.mean(jnp.square(xg - mu), -1, keepdims=True)
        xn = (xg - mu) * lax.rsqrt(var + 1e-6) * gnw_ref[:, h * ML_DV:(h + 1) * ML_DV]
        _store_natural_order(y_ref, n, h, xn * _silu(z))


def _ml_mixer(h, b, nc, conv_w, ib_row, fb_row, gn_w):
    nch = h.shape[0] // b // nc
    return pl.pallas_call(
        _ml_kernel,
        grid=(b, nch),
        in_specs=[pl.BlockSpec((nc,) + h.shape[1:], lambda i, j: (i * nch + j, 0, 0, 0)),
                  pl.BlockSpec(conv_w.shape, lambda i, j: (0, 0)),
                  pl.BlockSpec(ib_row.shape, lambda i, j: (0, 0)),
                  pl.BlockSpec(fb_row.shape, lambda i, j: (0, 0)),
                  pl.BlockSpec(gn_w.shape, lambda i, j: (0, 0))],
        out_specs=pl.BlockSpec((nc, ML_V // LANES, CHUNK, LANES), lambda i, j: (i * nch + j, 0, 0, 0)),
        out_shape=jax.ShapeDtypeStruct((h.shape[0], ML_V // LANES, CHUNK, LANES), F32),
        scratch_shapes=[pltpu.VMEM(((CONV_K - 1) * SUBLANES, 2 * ML_QK), F32),
                        pltpu.VMEM((ML_HEADS, ML_DQK, ML_DV + LANES), F32),
                        pltpu.VMEM((SUBLANES, LANES), F32)],
        compiler_params=pltpu.CompilerParams(dimension_semantics=("parallel", "arbitrary"),
                                             vmem_limit_bytes=VMEM_LIMIT),
        name="ml_mixer",
    )(h, conv_w, ib_row, fb_row, gn_w)


def _post_kernel(x_ref, y_ref, p_ref, wout_ref, wg_ref, wp_ref, lng_ref, lnb_ref, pnw_ref, o_ref):
    sub_rows = min(POST_SUB_ROWS, x_ref.shape[0])
    n_sub = x_ref.shape[0] // sub_rows
    per = sub_rows // CHUNK
    r, xn, gate, pn = {}, {}, {}, {}

    def out_proj(s):
        rows = slice(s * sub_rows, (s + 1) * sub_rows)
        y = jnp.concatenate([jnp.concatenate([y_ref[ch, t] for t in range(y_ref.shape[1])], axis=1)
                             for ch in range(s * per, (s + 1) * per)], axis=0)
        r[s] = DEEPNORM_ALPHA * x_ref[rows, :] + _dot(y, wout_ref[...])

    def layer_norm(s):
        mu = jnp.mean(r[s], -1, keepdims=True)
        var = jnp.mean(jnp.square(r[s] - mu), -1, keepdims=True)
        xn[s] = (r[s] - mu) * lax.rsqrt(var + LN_EPS) * lng_ref[...] + lnb_ref[...]

    def gate_and_embed(s):
        rows = slice(s * sub_rows, (s + 1) * sub_rows)
        gate[s] = jax.nn.sigmoid(_dot(xn[s], wg_ref[...]))
        pp = _dot(p_ref[rows, :], wp_ref[...])
        pn[s] = pp * lax.rsqrt(jnp.mean(pp * pp, -1, keepdims=True) + 1e-6) * pnw_ref[...]

    def combine(s):
        rows = slice(s * sub_rows, (s + 1) * sub_rows)
        o_ref[rows, :] = xn[s] + gate[s] * pn[s]

    stages = (out_proj, layer_norm, gate_and_embed, combine)
    for slot in range(n_sub + len(stages) - 1):
        for k, stage in enumerate(stages):
            if 0 <= slot - k < n_sub:
                stage(slot - k)


def _post(x2d, y4d, p3d, layer, w_out, w_gate, w_proj, ln_g, ln_b, pn_w, tm):
    m, d = x2d.shape
    p_spec = pl.BlockSpec((None, tm, p3d.shape[2]), lambda i: (layer, i, 0))
    y_spec = pl.BlockSpec((tm // CHUNK,) + y4d.shape[1:], lambda i: (i, 0, 0, 0))
    tile = lambda arr: pl.BlockSpec((tm, arr.shape[1]), lambda i: (i, 0))
    full = lambda arr: pl.BlockSpec(arr.shape, lambda i: (0, 0))
    params = (w_out, w_gate, w_proj, ln_g, ln_b, pn_w)
    return pl.pallas_call(
        _post_kernel,
        grid=(m // tm,),
        in_specs=[tile(x2d), y_spec, p_spec] + [full(a) for a in params],
        out_specs=pl.BlockSpec((tm, d), lambda i: (i, 0)),
        out_shape=jax.ShapeDtypeStruct((m, d), F32),
        compiler_params=pltpu.CompilerParams(dimension_semantics=("parallel",),
                                             vmem_limit_bytes=VMEM_LIMIT),
        name="post_block",
    )(x2d, y4d, p3d, *params)


def _pad_cols(w, n):
    return jnp.pad(w, ((0, 0),) * (w.ndim - 1) + ((0, n - w.shape[-1]),))


def _lane_row(vec, offset):
    return jnp.zeros((1, LANES), F32).at[0, offset:offset + vec.shape[0]].set(vec.astype(F32))


def _row_tile(m, cap):
    t = cap
    while t >= CHUNK:
        if m % t == 0:
            return t
        t //= 2
    raise ValueError(f"token count {m} must be a multiple of {CHUNK}")


def kernel(x, p, ln_g, ln_b, ple_w_proj, ple_norm_w, ple_w_gate, dn_w_in, dn_conv_w, dn_a_log, dn_dt_bias, dn_norm_w, dn_w_out, rw_w_in, rw_mu, rw_w0, rw_w_lora_up, rw_a0, rw_a_lora_up, rw_k_k, rw_k_a, rw_r_k, rw_gn_w, rw_gn_b, rw_w_out, ml_w_in, ml_conv_w, ml_i_bias, ml_f_bias, ml_gn_w, ml_w_out):
    b, s, d = x.shape
    assert d == D_MODEL and s % CHUNK == 0
    m = b * s
    tm_proj = _row_tile(m, PROJ_ROWS)
    tm_post = _row_tile(m, POST_ROWS)
    tm_rw = _row_tile(s, PROJ_ROWS)
    chunks_per_step = lambda want: next(n for n in (want, 2, 1) if (s // CHUNK) % n == 0)
    bf = lambda w: w.astype(MXU_DTYPE)
    row = lambda v: v.reshape(1, -1).astype(F32)
    x2d = x.reshape(m, d)
    p3d = p.reshape(DEPTH, m, D_PLE)
    dn_w_pad = bf(_pad_cols(dn_w_in, 2 * DN_QK + 2 * DN_V + LANES))
    ml_w_pad = bf(_pad_cols(ml_w_in, 2 * ML_QK + 3 * ML_V + LANES))
    for i in range(DEPTH):
        kind, j = i % 3, i // 3
        if kind == 0:
            h = _project(x2d, dn_w_pad[j], tm_proj)
            y = _dn_mixer(h, b, chunks_per_step(DN_CHUNKS), dn_conv_w[j], _lane_row(dn_a_log[j], DN_HEADS),
                          _lane_row(dn_dt_bias[j], DN_HEADS), row(dn_norm_w[j]))
            w_out = dn_w_out[j]
        elif kind == 1:
            r_w, wl_w, k_w, v_w, al_w, z_w = jnp.split(
                rw_w_in[j], np.cumsum([RW_W, RW_LORA, RW_W, RW_W, RW_LORA]).tolist(), axis=1)
            w_cat = jnp.concatenate([r_w, k_w, v_w, z_w, wl_w, al_w], axis=1)
            mu = rw_mu[j]
            mu_cat = jnp.stack([mu[0], mu[2], mu[3], mu[5], mu[1], mu[4], mu[0], mu[0]], axis=0)
            h = _rw_project(x2d.reshape(b, s, d), bf(w_cat), mu_cat, tm_rw)
            y = _rw_mixer(h, chunks_per_step(RW_CHUNKS), bf(rw_w_lora_up[j]), bf(rw_a_lora_up[j]), row(rw_w0[j]),
                          row(rw_a0[j]), row(rw_k_k[j]), row(rw_k_a[j]), row(rw_r_k[j]), row(rw_gn_w[j]),
                          row(rw_gn_b[j]))
            w_out = rw_w_out[j]
        else:
            h = _project(x2d, ml_w_pad[j], tm_proj)
            y = _ml_mixer(h, b, chunks_per_step(ML_CHUNKS), ml_conv_w[j], _lane_row(ml_i_bias[j], 0),
                          _lane_row(ml_f_bias[j], ML_HEADS), row(ml_gn_w[j]))
            w_out = ml_w_out[j]
        x2d = _post(x2d, y, p3d, i, bf(w_out), bf(ple_w_gate[i]),
                    bf(ple_w_proj[i]), row(ln_g[i]), row(ln_b[i]), row(ple_norm_w[i]), tm_post)
    return x2d.reshape(b, s, d)
```
